```python
import math
import jax, jax.numpy as jnp
from jax import lax
import numpy as np

D_MODEL = 1024
BATCH = 8
SEQ = 4096
DEPTH = 2

GRID_W = 64
CTX_LEN = 256
EPS = 1e-6
ROPE_BASE = 10000.0
D_FF = 4 * D_MODEL

RET_HEADS = 4
RET_DIM = 128
RET_W = RET_HEADS * RET_DIM
RET_CHUNK = 128

NA_HEADS = 8
NA_DIM = 64
NA_W = NA_HEADS * NA_DIM
NA_WIN_R = 8
NA_WIN_C = 16
NA_QCB = 16
NA_KCB = 32

EVEN_IN = 4 * RET_W + 3 * NA_W
MIX_W = RET_W + NA_W

SSM_INNER = 2 * D_MODEL
SSM_HEAD_DIM = 64
SSM_HEADS = SSM_INNER // SSM_HEAD_DIM
SSM_GROUPS = 4
SSM_HPG = SSM_HEADS // SSM_GROUPS
SSM_STATE = 128
SSM_CONV = 7
SSM_CHUNK = 128
SSM_GN = SSM_GROUPS * SSM_STATE
SSM_CONV_CH = SSM_INNER + 2 * SSM_GN
ODD_IN = SSM_INNER + SSM_CONV_CH + 2 * SSM_HEADS

kernel_name = 'hybrid_retention_natten_ssd_dit_block'


def _rmsnorm(x, g):
    xf = x.astype(jnp.float32)
    y = xf * lax.rsqrt(jnp.mean(xf * xf, axis=-1, keepdims=True) + EPS)
    return (y * g.astype(jnp.float32)).astype(x.dtype)


def _head_norm(y):
    mu = jnp.mean(y, axis=-1, keepdims=True)
    var = jnp.mean(jnp.square(y - mu), axis=-1, keepdims=True)
    return (y - mu) * lax.rsqrt(var + EPS)


def _mlp(h, w1, w2):
    return jnp.square(jax.nn.relu(h @ w1)) @ w2


def _flip(t):
    return jnp.flip(t, axis=1)


def _rope_2d(t, rpos, cpos):
    half = t.shape[-1] // 2
    nf = half // 2
    inv = ROPE_BASE ** (-jnp.arange(nf, dtype=jnp.float32) / nf)

    def rot(u, pos):
        ang = pos.astype(jnp.float32)[:, None] * inv[None, :]
        cos = jnp.cos(ang)[None, :, None, :]
        sin = jnp.sin(ang)[None, :, None, :]
        u1, u2 = u[..., :nf], u[..., nf:]
        return jnp.concatenate([u1 * cos - u2 * sin, u1 * sin + u2 * cos], axis=-1)

    return jnp.concatenate([rot(t[..., :half], rpos), rot(t[..., half:], cpos)], axis=-1)


def _chunks(t, size):
    b, l = t.shape[:2]
    return t.reshape(b, l // size, size, *t.shape[2:]).swapaxes(0, 1)


def _unchunk(t):
    n, b, cs = t.shape[:3]
    return t.swapaxes(0, 1).reshape(b, n * cs, *t.shape[3:])


def _retention_scan(q, k, v, log_g, s0, with_y):
    idx = jnp.arange(RET_CHUNK, dtype=jnp.float32)
    diff = idx[:, None] - idx[None, :]
    dmat = jnp.where(diff[None] >= 0, jnp.exp(jnp.maximum(diff, 0.0)[None] * log_g[:, None, None]), 0.0)
    q_dec = jnp.exp((idx[:, None] + 1.0) * log_g[None, :])
    k_dec = jnp.exp((RET_CHUNK - 1.0 - idx)[:, None] * log_g[None, :])
    c_dec = jnp.exp(RET_CHUNK * log_g)

    def body(s, inp):
        qc, kc, vc = inp
        s_new = c_dec[None, :, None, None] * s + jnp.einsum('bjhd,bjhe->bhde', kc * k_dec[None, :, :, None], vc)
        if not with_y:
            return s_new, None
        att = jnp.einsum('bihd,bjhd->bhij', qc, kc) * dmat[None]
        y = jnp.einsum('bhij,bjhe->bihe', att, vc) + jnp.einsum('bihd,bhde->bihe', qc, s) * q_dec[None, :, :, None]
        return s_new, y

    s, ys = lax.scan(body, s0, (_chunks(q, RET_CHUNK), _chunks(k, RET_CHUNK), _chunks(v, RET_CHUNK)))
    return (_unchunk(ys) if with_y else None), s


def _retention_bidir(q, k, v, log_g, s0_f, s0_b, with_y):
    y_f, s_f = _retention_scan(q, k, v, log_g[0], s0_f, with_y)
    y_b, s_b = _retention_scan(_flip(q), _flip(k), _flip(v), log_g[1], s0_b, with_y)
    y = y_f + _flip(y_b) if with_y else None
    return y, s_f, s_b


def _na_col_tables():
    nqb = GRID_W // NA_QCB
    qcol = np.arange(GRID_W).reshape(nqb, NA_QCB)
    kstart = np.clip(np.arange(nqb) * NA_QCB - NA_WIN_C // 2, 0, GRID_W - NA_KCB)
    kcol = kstart[:, None] + np.arange(NA_KCB)[None, :]
    cstart = np.clip(qcol - NA_WIN_C // 2, 0, GRID_W - NA_WIN_C)
    kk = kcol[:, None, :]
    cmask = (kk >= cstart[:, :, None]) & (kk < cstart[:, :, None] + NA_WIN_C)
    dcol = np.clip(kk - qcol[:, :, None], -(NA_WIN_C - 1), NA_WIN_C - 1) + (NA_WIN_C - 1)
    return kcol, cmask, dcol


def _na_latent(q, k, v, kc, vc, rpb, rows):
    b, _, nh, dh = q.shape
    wr = min(NA_WIN_R, rows)
    nqb = GRID_W // NA_QCB
    kcol, cmask_np, dcol = _na_col_tables()
    scale = dh ** -0.5
    qg = q.reshape(b, rows, nqb, NA_QCB, nh, dh)
    kg = k.reshape(b, rows, GRID_W, nh, dh)
    vg = v.reshape(b, rows, GRID_W, nh, dh)
    rpb_c = rpb[:, :, dcol]
    cmask = jnp.asarray(cmask_np)[:, :, None, :]

    def one_row(r):
        rs = jnp.clip(r - wr // 2, 0, rows - wr)
        kr = lax.dynamic_slice_in_dim(kg, rs, wr, axis=1)[:, :, kcol]
        vr = lax.dynamic_slice_in_dim(vg, rs, wr, axis=1)[:, :, kcol]
        qr = lax.dynamic_index_in_dim(qg, r, axis=1, keepdims=False)
        s_loc = jnp.einsum('bnqhd,brnkhd->bhnqrk', qr, kr).astype(jnp.float32) * scale
        dr = rs + jnp.arange(wr) - r + (NA_WIN_R - 1)
        bias = jnp.transpose(rpb_c[:, dr], (0, 2, 3, 1, 4)).astype(jnp.float32)
        s_loc = jnp.where(cmask, s_loc + bias, -jnp.inf).reshape(b, nh, nqb, NA_QCB, wr * NA_KCB)
        s_ctx = jnp.einsum('bnqhd,bchd->bhnqc', qr, kc).astype(jnp.float32) * scale
        p = jax.nn.softmax(jnp.concatenate([s_loc, s_ctx], axis=-1), axis=-1)
        p_loc = p[..., :wr * NA_KCB].reshape(b, nh, nqb, NA_QCB, wr, NA_KCB)
        p_ctx = p[..., wr * NA_KCB:]
        o = (jnp.einsum('bhnqrk,brnkhd->bnqhd', p_loc, vr.astype(jnp.float32))
             + jnp.einsum('bhnqc,bchd->bnqhd', p_ctx, vc.astype(jnp.float32)))
        return o.reshape(b, GRID_W, nh, dh).astype(q.dtype)

    out = lax.map(one_row, jnp.arange(rows))
    return jnp.transpose(out, (1, 0, 2, 3, 4)).reshape(b, rows * GRID_W, nh, dh)


def _ctx_attn(q, k, v):
    s = jnp.einsum('bqhd,bkhd->bhqk', q, k).astype(jnp.float32) * (q.shape[-1] ** -0.5)
    p = jax.nn.softmax(s, axis=-1)
    return jnp.einsum('bhqk,bkhd->bqhd', p, v.astype(jnp.float32)).astype(q.dtype)


def _dwconv(x, w, bias):
    ch = x.shape[-1]
    y = lax.conv_general_dilated(x, w.astype(x.dtype)[:, None, :], window_strides=(1,),
                                 padding=[(SSM_CONV // 2, SSM_CONV // 2)],
                                 dimension_numbers=('NWC', 'WIO', 'NWC'), feature_group_count=ch)
    return y + bias.astype(x.dtype)


def _ssd_scan(x, dt, a, bm, cm, h0, with_y):
    tri = jnp.tril(jnp.ones((SSM_CHUNK, SSM_CHUNK), dtype=bool))[None, :, :, None, None]

    def body(h, inp):
        xc, dtc, bc, cc = inp
        acum = jnp.cumsum(dtc * a, axis=1)
        last = acum[:, -1]
        w_end = jnp.exp(last[:, None] - acum) * dtc
        h_new = jnp.exp(last)[..., None, None] * h + jnp.einsum('bjgh,bjgn,bjghp->bghpn', w_end, bc, xc)
        if not with_y:
            return h_new, None
        lmat = jnp.exp(jnp.where(tri, acum[:, :, None] - acum[:, None, :], -jnp.inf))
        cb = jnp.einsum('bign,bjgn->bijg', cc, bc)
        y = jnp.einsum('bijgh,bjghp->bighp', lmat * cb[..., None] * dtc[:, None], xc)
        y = y + jnp.einsum('bign,bghpn->bighp', cc, h) * jnp.exp(acum)[..., None]
        return h_new, y

    h, ys = lax.scan(body, h0, (_chunks(x, SSM_CHUNK), _chunks(dt, SSM_CHUNK), _chunks(bm, SSM_CHUNK), _chunks(cm, SSM_CHUNK)))
    return (_unchunk(ys) if with_y else None), h


def _ssd_bidir(x, dt, a, bm, cm, h0_f, h0_b, with_y):
    y_f, h_f = _ssd_scan(x, dt[:, :, 0], a[0], bm, cm, h0_f, with_y)
    y_b, h_b = _ssd_scan(_flip(x), _flip(dt[:, :, 1]), a[1], _flip(bm), _flip(cm), h0_b, with_y)
    y = y_f + _flip(y_b) if with_y else None
    return y, h_f, h_b


def _even_mixer(h, hc, w_in, w_out, decay_logit, rpb, rows, ctx_out):
    b, l, _ = h.shape
    cuts = [RET_W, 2 * RET_W, 3 * RET_W, 4 * RET_W, 4 * RET_W + NA_W, 4 * RET_W + 2 * NA_W]
    rq, rk, rv, rg, nq, nk, nv = jnp.split(h @ w_in, cuts, axis=-1)
    crq, crk, crv, crg, cnq, cnk, cnv = jnp.split(hc @ w_in, cuts, axis=-1)
    pos = jnp.arange(l)
    rpos = pos // GRID_W
    cpos = pos % GRID_W

    def ret_heads(t):
        return t.reshape(t.shape[0], t.shape[1], RET_HEADS, RET_DIM).astype(jnp.float32)

    def na_heads(t):
        return t.reshape(t.shape[0], t.shape[1], NA_HEADS, NA_DIM)

    def ret_finish(y, g):
        return (_head_norm(y) * jax.nn.silu(ret_heads(g))).reshape(y.shape[0], y.shape[1], RET_W).astype(h.dtype)

    kscale = RET_DIM ** -0.5
    log_g = jax.nn.log_sigmoid(decay_logit.astype(jnp.float32))
    s0 = jnp.zeros((b, RET_HEADS, RET_DIM, RET_DIM), jnp.float32)
    yc_ret, s_f, s_b = _retention_bidir(ret_heads(crq), ret_heads(crk) * kscale, ret_heads(crv), log_g, s0, s0, ctx_out)
    y_ret, _, _ = _retention_bidir(_rope_2d(ret_heads(rq), rpos, cpos), _rope_2d(ret_heads(rk), rpos, cpos) * kscale,
                                   ret_heads(rv), log_g, s_f, s_b, True)
    ck, cv = na_heads(cnk), na_heads(cnv)
    y_na = _na_latent(na_heads(nq), na_heads(nk), na_heads(nv), ck, cv, rpb, rows).reshape(b, l, NA_W)
    out = jnp.concatenate([ret_finish(y_ret, rg), y_na], axis=-1) @ w_out
    if not ctx_out:
        return out, None
    y_na_c = _ctx_attn(na_heads(cnq), ck, cv).reshape(b, hc.shape[1], NA_W)
    out_c = jnp.concatenate([ret_finish(yc_ret, crg), y_na_c], axis=-1) @ w_out
    return out, out_c


def _odd_mixer(h, hc, w_in, conv_w, conv_b, dt_bias, a_log, d_skip, norm_w, w_out, ctx_out):
    def prep(t):
        bsz, l = t.shape[:2]
        z, xbc, dt = jnp.split(t @ w_in, [SSM_INNER, SSM_INNER + SSM_CONV_CH], axis=-1)
        xbc = jax.nn.silu(_dwconv(xbc, conv_w, conv_b))
        xs, bm, cm = jnp.split(xbc, [SSM_INNER, SSM_INNER + SSM_GN], axis=-1)
        xs = xs.reshape(bsz, l, SSM_GROUPS, SSM_HPG, SSM_HEAD_DIM).astype(jnp.float32)
        bm = bm.reshape(bsz, l, SSM_GROUPS, SSM_STATE).astype(jnp.float32)
        cm = cm.reshape(bsz, l, SSM_GROUPS, SSM_STATE).astype(jnp.float32)
        dt = jax.nn.softplus(dt.astype(jnp.float32) + dt_bias.reshape(-1).astype(jnp.float32))
        return z, xs, bm, cm, dt.reshape(bsz, l, 2, SSM_GROUPS, SSM_HPG)

    def finish(y, xs, z):
        bsz, l = y.shape[:2]
        y = (y + d_skip.reshape(SSM_GROUPS, SSM_HPG, 1).astype(jnp.float32) * xs).reshape(bsz, l, SSM_INNER)
        y = _rmsnorm(y * jax.nn.silu(z.astype(jnp.float32)), norm_w)
        return y.astype(h.dtype) @ w_out

    a = -jnp.exp(a_log.astype(jnp.float32)).reshape(2, SSM_GROUPS, SSM_HPG)
    z, xs, bm, cm, dt = prep(h)
    cz, cxs, cbm, ccm, cdt = prep(hc)
    h0 = jnp.zeros((h.shape[0], SSM_GROUPS, SSM_HPG, SSM_HEAD_DIM, SSM_STATE), jnp.float32)
    yc, h_f, h_b = _ssd_bidir(cxs, cdt, a, cbm, ccm, h0, h0, ctx_out)
    y, _, _ = _ssd_bidir(xs, dt, a, bm, cm, h_f, h_b, True)
    out = finish(y, xs, z)
    if not ctx_out:
        return out, None
    return out, finish(yc, cxs, cz)


def setup_inputs(seed: int = 0) -> dict:
    key = jax.random.key(seed)
    ks = jax.random.split(key, 24)
    n_even = (DEPTH + 1) // 2
    n_odd = DEPTH // 2

    def nrm(k, shape, s):
        return jax.random.normal(k, shape, jnp.float32) * s

    gamma = 1.0 - 2.0 ** (-5.0 - np.arange(RET_HEADS, dtype=np.float32))
    logit0 = jnp.asarray(np.log(gamma / (1.0 - gamma)).astype(np.float32))
    dt0 = jnp.exp(jax.random.uniform(ks[17], (n_odd, 2, SSM_HEADS), jnp.float32, math.log(1e-3), math.log(1e-1)))
    return {
        'x': nrm(ks[0], (BATCH, SEQ, D_MODEL), 1.0),
        'c': nrm(ks[1], (BATCH, D_MODEL), 1.0),
        'ctx': nrm(ks[2], (BATCH, CTX_LEN, D_MODEL), 1.0),
        'c_ctx': nrm(ks[3], (D_MODEL,), 1.0),
        'w_mod': nrm(ks[4], (DEPTH, D_MODEL, 6 * D_MODEL), 0.5 * D_MODEL ** -0.5),
        'b_mod': nrm(ks[5], (DEPTH, 6 * D_MODEL), 0.01),
        'norm_mix': 1.0 + nrm(ks[6], (DEPTH, D_MODEL), 0.02),
        'norm_mlp': 1.0 + nrm(ks[7], (DEPTH, D_MODEL), 0.02),
        'w_mlp_in': nrm(ks[8], (DEPTH, D_MODEL, D_FF), D_MODEL ** -0.5),
        'w_mlp_out': nrm(ks[9], (DEPTH, D_FF, D_MODEL), D_FF ** -0.5),
        'w_in_even': nrm(ks[10], (n_even, D_MODEL, EVEN_IN), D_MODEL ** -0.5),
        'w_out_even': nrm(ks[11], (n_even, MIX_W, D_MODEL), MIX_W ** -0.5),
        'ret_decay_logit': logit0[None, None, :] + nrm(ks[12], (n_even, 2, RET_HEADS), 0.01),
        'na_rpb': nrm(ks[13], (n_even, NA_HEADS, 2 * NA_WIN_R - 1, 2 * NA_WIN_C - 1), 0.02),
        'w_in_odd': nrm(ks[14], (n_odd, D_MODEL, ODD_IN), D_MODEL ** -0.5),
        'conv_w': nrm(ks[15], (n_odd, SSM_CONV, SSM_CONV_CH), SSM_CONV ** -0.5),
        'conv_b': nrm(ks[16], (n_odd, SSM_CONV_CH), 0.01),
        'dt_bias': dt0 + jnp.log(-jnp.expm1(-dt0)),
        'a_log': jnp.log(jax.random.uniform(ks[18], (n_odd, 2, SSM_HEADS), jnp.float32, 1.0, 16.0)),
        'd_skip': 1.0 + nrm(ks[19], (n_odd, SSM_HEADS), 0.02),
        'ssm_norm': 1.0 + nrm(ks[20], (n_odd, SSM_INNER), 0.02),
        'w_out_odd': nrm(ks[21], (n_odd, SSM_INNER, D_MODEL), SSM_INNER ** -0.5),
        'norm_final': 1.0 + nrm(ks[22], (D_MODEL,), 0.02),
    }


def reference(x, c, ctx, c_ctx, w_mod, b_mod, norm_mix, norm_mlp, w_mlp_in, w_mlp_out,
              w_in_even, w_out_even, ret_decay_logit, na_rpb,
              w_in_odd, conv_w, conv_b, dt_bias, a_log, d_skip, ssm_norm, w_out_odd, norm_final):
    rows = x.shape[1] // GRID_W
    hctx = ctx
    for layer in range(DEPTH):
        last = layer == DEPTH - 1
        j = layer // 2
        mod = jax.nn.silu(c) @ w_mod[layer] + b_mod[layer]
        mod_c = jax.nn.silu(c_ctx) @ w_mod[layer] + b_mod[layer]
        sh1, sc1, g1, sh2, sc2, g2 = jnp.split(mod[:, None, :], 6, axis=-1)
        csh1, csc1, cg1, csh2, csc2, cg2 = jnp.split(mod_c, 6, axis=-1)
        h = _rmsnorm(x, norm_mix[layer]) * (1.0 + sc1) + sh1
        hc = _rmsnorm(hctx, norm_mix[layer]) * (1.0 + csc1) + csh1
        if layer % 2 == 0:
            y, yc = _even_mixer(h, hc, w_in_even[j], w_out_even[j], ret_decay_logit[j], na_rpb[j], rows, not last)
        else:
            y, yc = _odd_mixer(h, hc, w_in_odd[j], conv_w[j], conv_b[j], dt_bias[j], a_log[j], d_skip[j],
                               ssm_norm[j], w_out_odd[j], not last)
        x = x + g1 * y
        x = x + g2 * _mlp(_rmsnorm(x, norm_mlp[layer]) * (1.0 + sc2) + sh2, w_mlp_in[layer], w_mlp_out[layer])
        if not last:
            hctx = hctx + cg1 * yc
            hctx = hctx + cg2 * _mlp(_rmsnorm(hctx, norm_mlp[layer]) * (1.0 + csc2) + csh2,
                                     w_mlp_in[layer], w_mlp_out[layer])
    return _rmsnorm(x, norm_final)
```

```python
import functools

import numpy as np
import jax
import jax.numpy as jnp
from jax import lax
from jax.experimental import pallas as pl
from jax.experimental.pallas import tpu as pltpu

F32 = jnp.float32
BF16 = jnp.bfloat16

D_MODEL = 1024
D_FF = 4 * D_MODEL
DEPTH = 2
GRID_W = 64
EPS = 1e-6
ROPE_BASE = 10000.0

RET_HEADS = 4
RET_DIM = 128
RET_W = RET_HEADS * RET_DIM

NA_HEADS = 8
NA_DIM = 64
NA_W = NA_HEADS * NA_DIM
NA_WIN_R = 8
NA_WIN_C = 16

SSM_INNER = 2 * D_MODEL
SSM_HEAD_DIM = 64
SSM_HEADS = SSM_INNER // SSM_HEAD_DIM
SSM_GROUPS = 4
SSM_HPG = SSM_HEADS // SSM_GROUPS
SSM_STATE = 128
SSM_CONV = 7
SSM_GN = SSM_GROUPS * SSM_STATE
SSM_CONV_CH = SSM_INNER + 2 * SSM_GN
SSM_GW = SSM_HPG * SSM_HEAD_DIM

LANES = 128
TM = 256
CHUNK = 128
HALO = 16
NEG = -1e30
NA_KROWS = 10
VMEM_LIMIT = 56 * 1024 * 1024


def _params(n_axes):
    return pltpu.CompilerParams(dimension_semantics=("arbitrary",) * n_axes, vmem_limit_bytes=VMEM_LIMIT)


def _resident(shape):
    nd = len(shape)
    return pl.BlockSpec(shape, lambda *_: (0,) * nd, pipeline_mode=pl.Buffered(1))


def _nt(a, b):
    return lax.dot_general(a, b, (((1,), (1,)), ((), ())), preferred_element_type=F32)


def _tn(a, b):
    return lax.dot_general(a, b, (((0,), (0,)), ((), ())), preferred_element_type=F32)


def _mm(a, b):
    return jnp.dot(a, b, preferred_element_type=F32)


def _silu(v):
    return v * jax.nn.sigmoid(v)


def _norm_mod(x, g, sc, sh):
    ms = jnp.mean(x * x, axis=-1, keepdims=True)
    return (x * lax.rsqrt(ms + EPS) * g) * (1.0 + sc) + sh


def _mod_kernel(cc_ref, w_ref, b_ref, o_ref):
    s = _silu(cc_ref[...])
    o_ref[0] = _mm(s.astype(BF16), w_ref[0].astype(BF16)) + b_ref[0]


def _modulation(c, c_ctx, w_mod, b_mod):
    b = c.shape[0]
    rows = -(-(b + 1) // 8) * 8
    cc = jnp.zeros((rows, D_MODEL), F32).at[:b].set(c).at[b].set(c_ctx)
    tn = 1536
    out = pl.pallas_call(
        _mod_kernel,
        grid=(DEPTH, 6 * D_MODEL // tn),
        in_specs=[pl.BlockSpec((rows, D_MODEL), lambda l, j: (0, 0)),
                  pl.BlockSpec((1, D_MODEL, tn), lambda l, j: (l, 0, j)),
                  pl.BlockSpec((1, 1, tn), lambda l, j: (l, 0, j))],
        out_specs=pl.BlockSpec((1, rows, tn), lambda l, j: (l, 0, j)),
        out_shape=jax.ShapeDtypeStruct((DEPTH, rows, 6 * D_MODEL), F32),
        compiler_params=_params(2),
        name="modulation",
    )(cc, w_mod, b_mod.reshape(DEPTH, 1, 6 * D_MODEL))
    mod = out.reshape(DEPTH, rows, 6, D_MODEL)
    return jnp.stack([jnp.broadcast_to(mod[:, b:b + 1], (DEPTH, b, 6, D_MODEL)), mod[:, :b]], axis=2)


def _mod_spec():
    return pl.BlockSpec((1, 1, 6, D_MODEL), lambda b, t: (b, jnp.minimum(t, 1), 0, 0))


def _rope_tables(seq, ctx_len):
    pos = jnp.arange(seq)
    nf = RET_DIM // 4
    inv = ROPE_BASE ** (-jnp.arange(nf, dtype=F32) / nf)

    def cs(p):
        ang = p.astype(F32)[:, None] * inv[None, :]
        return jnp.cos(ang), jnp.sin(ang)

    cr, sr = cs(pos // GRID_W)
    cc, sc = cs(pos % GRID_W)
    zero = jnp.zeros_like(sr)
    cos = jnp.concatenate([cr, cr, cc, cc], axis=1)
    sin_a = jnp.concatenate([-sr, zero, -sc, zero], axis=1)
    sin_b = jnp.concatenate([zero, sr, zero, sc], axis=1)
    ident = jnp.ones((ctx_len, RET_DIM), F32)
    nul = jnp.zeros((ctx_len, RET_DIM), F32)
    return (jnp.concatenate([ident, cos], axis=0), jnp.concatenate([nul, sin_a], axis=0),
            jnp.concatenate([nul, sin_b], axis=0))


def _even_in_kernel(x_ref, mod_ref, g_ref, w_ref, cos_ref, sa_ref, sb_ref,
                    rq_ref, rk_ref, rv_ref, rg_ref, nq_ref, nk_ref, nv_ref):
    m = mod_ref[0, 0]
    h = _norm_mod(x_ref[0], g_ref[...], m[1:2], m[0:1]).astype(BF16)
    cos, sa, sb = cos_ref[...], sa_ref[...], sb_ref[...]

    def proj(j):
        return _mm(h, w_ref[:, j * RET_W:(j + 1) * RET_W])

    def rope_store(o_ref, r, scale):
        for hh in range(RET_HEADS):
            t = r[:, hh * RET_DIM:(hh + 1) * RET_DIM]
            o = t * cos + pltpu.roll(t, RET_DIM - 32, 1) * sa + pltpu.roll(t, 32, 1) * sb
            o_ref[0, :, hh * RET_DIM:(hh + 1) * RET_DIM] = (o * scale).astype(BF16)

    rope_store(rq_ref, proj(0), 1.0)
    rope_store(rk_ref, proj(1), RET_DIM ** -0.5)
    rv_ref[0] = proj(2).astype(BF16)
    rg_ref[0] = proj(3).astype(BF16)
    nq_ref[0] = (proj(4) * NA_DIM ** -0.5).astype(BF16)
    nk_ref[0] = proj(5).astype(BF16)
    nv_ref[0] = proj(6).astype(BF16)


def _even_in(xs, modsel, g, w, tables):
    b, t, _ = xs.shape
    tok = pl.BlockSpec((1, TM, D_MODEL), lambda bb, tt: (bb, tt, 0))
    tab = pl.BlockSpec((TM, RET_DIM), lambda bb, tt: (tt, 0))
    out = pl.BlockSpec((1, TM, RET_W), lambda bb, tt: (bb, tt, 0))
    return pl.pallas_call(
        _even_in_kernel,
        grid=(b, t // TM),
        in_specs=[tok, _mod_spec(), _resident((1, D_MODEL)), _resident(w.shape), tab, tab, tab],
        out_specs=[out] * 7,
        out_shape=[jax.ShapeDtypeStruct((b, t, RET_W), BF16)] * 7,
        compiler_params=_params(2),
        name="even_in",
    )(xs, modsel, g.reshape(1, D_MODEL), w, *tables)


def _ret_tables(decay_logit):
    log_g = jax.nn.log_sigmoid(decay_logit.astype(F32))
    idx = jnp.arange(CHUNK, dtype=F32)
    diff = idx[:, None] - idx[None, :]
    ones = jnp.ones((CHUNK, CHUNK), F32)

    def one(lg, d):
        dd = diff if d == 0 else -diff
        dm = jnp.where(dd >= 0, jnp.exp(jnp.maximum(dd, 0.0) * lg), 0.0)
        qe = (idx + 1.0) if d == 0 else (CHUNK - idx)
        ke = (CHUNK - 1.0 - idx) if d == 0 else idx
        return jnp.stack([dm, jnp.exp(qe * lg)[:, None] * ones, jnp.exp(ke * lg)[:, None] * ones,
                          jnp.exp(CHUNK * lg) * ones])

    return jnp.stack([jnp.stack([one(log_g[d, h], d) for h in range(RET_HEADS)]) for d in range(2)])


def _ret_kernel(qf_ref, kf_ref, vf_ref, qb_ref, kb_ref, vb_ref, dec_ref, yf_ref, yb_ref, s_ref):
    @pl.when(pl.program_id(2) == 0)
    def _():
        s_ref[...] = jnp.zeros_like(s_ref)

    for d, (q_ref, k_ref, v_ref, y_ref) in enumerate(((qf_ref, kf_ref, vf_ref, yf_ref),
                                                      (qb_ref, kb_ref, vb_ref, yb_ref))):
        q, k, v = q_ref[0], k_ref[0], v_ref[0]
        s = s_ref[d]
        att = _nt(q, k) * dec_ref[d, 0, 0]
        y_ref[0] = _mm(att.astype(BF16), v) + _mm(q, s.astype(BF16)) * dec_ref[d, 0, 1]
        kk = (k.astype(F32) * dec_ref[d, 0, 2]).astype(BF16)
        s_ref[d] = dec_ref[d, 0, 3] * s + _tn(kk, v)


def _scan_orders(n_steps, n_ctx):
    fwd = lambda i: i
    bwd = lambda i: jnp.where(i < n_ctx, n_ctx - 1 - i, n_steps + n_ctx - 1 - i)
    return fwd, bwd


def _retention(rq, rk, rv, dec, ctx_len):
    b, t, _ = rq.shape
    ns = t // CHUNK
    fwd, bwd = _scan_orders(ns, ctx_len // CHUNK)

    def spec(order):
        return pl.BlockSpec((1, CHUNK, RET_DIM), lambda bb, hh, i: (bb, order(i), hh))

    return pl.pallas_call(
        _ret_kernel,
        grid=(b, RET_HEADS, ns),
        in_specs=[spec(fwd)] * 3 + [spec(bwd)] * 3
                 + [pl.BlockSpec((2, 1, 4, CHUNK, CHUNK), lambda bb, hh, i: (0, hh, 0, 0, 0))],
        out_specs=[spec(fwd), spec(bwd)],
        out_shape=[jax.ShapeDtypeStruct((b, t, RET_W), F32)] * 2,
        scratch_shapes=[pltpu.VMEM((2, RET_DIM, RET_DIM), F32)],
        compiler_params=_params(3),
        name="retention",
    )(rq, rk, rv, rq, rk, rv, dec)


def _na_bias_tables(rpb):
    qc = np.arange(GRID_W)[:, None]
    kc = np.arange(GRID_W)[None, :]
    cstart = np.clip(qc - NA_WIN_C // 2, 0, GRID_W - NA_WIN_C)
    inwin = (kc >= cstart) & (kc < cstart + NA_WIN_C)
    dcol = np.clip(kc - qc, -(NA_WIN_C - 1), NA_WIN_C - 1) + (NA_WIN_C - 1)
    t = jnp.where(jnp.asarray(inwin), rpb.astype(F32)[:, :, dcol], NEG)
    pad = jnp.full((NA_HEADS, 2, GRID_W, GRID_W), NEG, F32)
    t = jnp.concatenate([pad, t, pad], axis=1)
    return jnp.concatenate([t[:, :-1], t[:, 1:]], axis=-1)


def _na_kernel(q_ref, k_ref, v_ref, tbl_ref, o_ref, *, rows, ctx_len):
    s = pl.program_id(1)
    n_ctx = ctx_len // CHUNK
    left = lax.broadcasted_iota(jnp.int32, (1, LANES), 1) < NA_DIM
    q = q_ref[0]
    n_loc = NA_KROWS * GRID_W

    def attend(kloc, vloc, bias_of_head):
        for hp in range(NA_HEADS // 2):
            sl = slice(hp * LANES, (hp + 1) * LANES)
            qp = q[:, sl]
            kc, vc = k_ref[0, 0:ctx_len, sl], v_ref[0, 0:ctx_len, sl]
            outs = []
            for e in range(2):
                qm = jnp.where(left if e == 0 else jnp.logical_not(left), qp, jnp.zeros_like(qp))
                s_ctx = _nt(qm, kc)
                mx = jnp.max(s_ctx, axis=-1, keepdims=True)
                if kloc is not None:
                    s_loc = _nt(qm, kloc[:, sl]) + bias_of_head(2 * hp + e)
                    mx = jnp.maximum(mx, jnp.max(s_loc, axis=-1, keepdims=True))
                    p_loc = jnp.exp(s_loc - mx)
                p_ctx = jnp.exp(s_ctx - mx)
                den = jnp.sum(p_ctx, axis=-1, keepdims=True)
                o = _mm(p_ctx.astype(BF16), vc)
                if kloc is not None:
                    den = den + jnp.sum(p_loc, axis=-1, keepdims=True)
                    o = o + _mm(p_loc.astype(BF16), vloc[:, sl])
                outs.append(o / den)
            o_ref[0, :, sl] = jnp.where(left, outs[0], outs[1]).astype(BF16)

    @pl.when(s < n_ctx)
    def _():
        attend(None, None, None)

    @pl.when(s >= n_ctx)
    def _():
        r0 = 2 * (s - n_ctx)
        kb = jnp.clip(r0 - NA_WIN_R // 2, 0, rows - NA_KROWS)
        off = pl.multiple_of(ctx_len + kb * GRID_W, GRID_W)
        kloc = k_ref[0, pl.ds(off, n_loc), :]
        vloc = v_ref[0, pl.ds(off, n_loc), :]
        idx, pen = [], []
        for qr in range(2):
            r = r0 + qr
            rs = jnp.clip(r - NA_WIN_R // 2, 0, rows - NA_WIN_R)
            for mm in range(NA_KROWS // 2):
                kp = kb + 2 * mm
                ok_l = jnp.logical_and(kp >= rs, kp < rs + NA_WIN_R)
                ok_r = jnp.logical_and(kp + 1 >= rs, kp + 1 < rs + NA_WIN_R)
                idx.append(kp - r + NA_WIN_R + 1)
                pen.append(jnp.where(left, jnp.where(ok_l, 0.0, NEG), jnp.where(ok_r, 0.0, NEG)))

        def bias_of_head(h):
            rows_b = []
            for qr in range(2):
                blocks = []
                for mm in range(NA_KROWS // 2):
                    n = qr * (NA_KROWS // 2) + mm
                    blocks.append(tbl_ref[h, idx[n]] + pen[n])
                rows_b.append(jnp.concatenate(blocks, axis=1))
            return jnp.concatenate(rows_b, axis=0)

        attend(kloc, vloc, bias_of_head)


def _na(nq, nk, nv, tbl, ctx_len):
    b, t, _ = nq.shape
    rows = (t - ctx_len) // GRID_W
    assert rows >= NA_KROWS and rows % 2 == 0 and CHUNK == 2 * GRID_W
    full = pl.BlockSpec((1, t, NA_W), lambda bb, s: (bb, 0, 0))
    blk = pl.BlockSpec((1, CHUNK, NA_W), lambda bb, s: (bb, s, 0))
    return pl.pallas_call(
        functools.partial(_na_kernel, rows=rows, ctx_len=ctx_len),
        grid=(b, t // CHUNK),
        in_specs=[blk, full, full, _resident(tbl.shape)],
        out_specs=blk,
        out_shape=jax.ShapeDtypeStruct((b, t, NA_W), BF16),
        compiler_params=_params(2),
        name="na",
    )(nq, nk, nv, tbl)


def _mlp_tail(x1, m, nmlp, w1_ref, w2_ref):
    h2 = _norm_mod(x1, nmlp, m[4:5], m[3:4]).astype(BF16)
    fc = D_MODEL
    acc = jnp.zeros_like(x1)
    for j in range(D_FF // fc):
        u = _mm(h2, w1_ref[:, j * fc:(j + 1) * fc])
        u = jnp.square(jnp.maximum(u, 0.0)).astype(BF16)
        acc = acc + _mm(u, w2_ref[j * fc:(j + 1) * fc, :])
    return x1 + m[5:6] * acc


def _even_out_kernel(x_ref, mod_ref, yf_ref, yb_ref, rg_ref, yna_ref, wo_ref, nmlp_ref, w1_ref, w2_ref, o_ref):
    m = mod_ref[0, 0]
    y = yf_ref[0] + yb_ref[0]
    g = rg_ref[0].astype(F32)
    mix = _mm(yna_ref[0], wo_ref[RET_W:RET_W + NA_W, :])
    for hh in range(RET_HEADS):
        sl = slice(hh * RET_DIM, (hh + 1) * RET_DIM)
        yh = y[:, sl]
        dlt = yh - jnp.mean(yh, axis=-1, keepdims=True)
        yn = dlt * lax.rsqrt(jnp.mean(dlt * dlt, axis=-1, keepdims=True) + EPS)
        mix = mix + _mm((yn * _silu(g[:, sl])).astype(BF16), wo_ref[sl, :])
    x1 = x_ref[0] + m[2:3] * mix
    o_ref[0] = _mlp_tail(x1, m, nmlp_ref[...], w1_ref, w2_ref)


def _even_out(xs, modsel, yf, yb, rg, yna, wo, nmlp, w1, w2):
    b, t, _ = xs.shape
    tok = lambda w: pl.BlockSpec((1, TM, w), lambda bb, tt: (bb, tt, 0))
    return pl.pallas_call(
        _even_out_kernel,
        grid=(b, t // TM),
        in_specs=[tok(D_MODEL), _mod_spec(), tok(RET_W), tok(RET_W), tok(RET_W), tok(NA_W),
                  _resident(wo.shape), _resident((1, D_MODEL)), _resident(w1.shape), _resident(w2.shape)],
        out_specs=tok(D_MODEL),
        out_shape=jax.ShapeDtypeStruct((b, t, D_MODEL), F32),
        compiler_params=_params(2),
        name="even_out_mlp",
    )(xs, modsel, yf, yb, rg, yna, wo, nmlp.reshape(1, D_MODEL), w1, w2)


def _odd_in_kernel(x_ref, mod_ref, g_ref, w_ref, wdt_ref, dtb_ref, z_ref, xbc_ref, dt_ref):
    m = mod_ref[0, 0]
    h = _norm_mod(x_ref[0], g_ref[...], m[1:2], m[0:1]).astype(BF16)
    cw = 512
    for j in range(SSM_INNER // cw):
        z_ref[0, :, j * cw:(j + 1) * cw] = _mm(h, w_ref[:, j * cw:(j + 1) * cw]).astype(BF16)
    for j in range(SSM_CONV_CH // cw):
        c0 = SSM_INNER + j * cw
        xbc_ref[0, :, j * cw:(j + 1) * cw] = _mm(h, w_ref[:, c0:c0 + cw]).astype(BF16)
    raw = _mm(h, wdt_ref[...]) + dtb_ref[...]
    dt_ref[0] = jnp.maximum(raw, 0.0) + jnp.log1p(jnp.exp(-jnp.abs(raw)))


def _odd_in(xs, modsel, g, w_main, w_dt, dt_bias):
    b, t, _ = xs.shape
    tok = lambda w: pl.BlockSpec((1, TM, w), lambda bb, tt: (bb, tt, 0))
    return pl.pallas_call(
        _odd_in_kernel,
        grid=(b, t // TM),
        in_specs=[tok(D_MODEL), _mod_spec(), _resident((1, D_MODEL)), _resident(w_main.shape),
                  _resident(w_dt.shape), _resident((1, LANES))],
        out_specs=[tok(SSM_INNER), tok(SSM_CONV_CH), tok(LANES)],
        out_shape=[jax.ShapeDtypeStruct((b, t, SSM_INNER), BF16),
                   jax.ShapeDtypeStruct((b, t, SSM_CONV_CH), BF16),
                   jax.ShapeDtypeStruct((b, t, LANES), F32)],
        compiler_params=_params(2),
        name="odd_in",
    )(xs, modsel, g.reshape(1, D_MODEL), w_main, w_dt, dt_bias)


def _conv_kernel(main_ref, prev_ref, next_ref, w_ref, b_ref, o_ref, ext_ref, *, n_tiles):
    t = pl.program_id(1)
    has_prev = (t >= 2).astype(F32)
    has_next = jnp.logical_and(t >= 1, t < n_tiles - 1).astype(F32)
    cw = 512
    for j in range(SSM_CONV_CH // cw):
        sl = slice(j * cw, (j + 1) * cw)
        ext_ref[0:HALO, :] = prev_ref[0, :, sl].astype(F32) * has_prev
        ext_ref[HALO:HALO + TM, :] = main_ref[0, :, sl].astype(F32)
        ext_ref[HALO + TM:, :] = next_ref[0, :, sl].astype(F32) * has_next
        acc = jnp.zeros((TM, cw), F32) + b_ref[:, sl]
        for k in range(SSM_CONV):
            acc = acc + w_ref[k:k + 1, sl] * ext_ref[pl.ds(HALO - SSM_CONV // 2 + k, TM), :]
        o_ref[0, :, sl] = _silu(acc).astype(BF16)


def _conv(xbc, w, bias, ctx_len):
    b, t, ch = xbc.shape
    assert ctx_len == TM and TM % HALO == 0
    nt = t // TM
    per = TM // HALO
    main = pl.BlockSpec((1, TM, ch), lambda bb, tt: (bb, tt, 0))
    prev = pl.BlockSpec((1, HALO, ch), lambda bb, tt: (bb, jnp.maximum(tt * per - 1, 0), 0))
    nxt = pl.BlockSpec((1, HALO, ch), lambda bb, tt: (bb, jnp.minimum((tt + 1) * per, nt * per - 1), 0))
    return pl.pallas_call(
        functools.partial(_conv_kernel, n_tiles=nt),
        grid=(b, nt),
        in_specs=[main, prev, nxt, _resident(w.shape), _resident((1, ch))],
        out_specs=main,
        out_shape=jax.ShapeDtypeStruct((b, t, ch), BF16),
        scratch_shapes=[pltpu.VMEM((TM + 2 * HALO, 512), F32)],
        compiler_params=_params(2),
        name="dwconv_silu",
    )(xbc, xbc, xbc, w, bias.reshape(1, ch))


def _dot_exact_lhs(a16, x):
    hi = x.astype(BF16)
    r1 = x - hi.astype(F32)
    mid = r1.astype(BF16)
    lo = (r1 - mid.astype(F32)).astype(BF16)
    return _mm(a16, hi) + _mm(a16, mid) + _mm(a16, lo)


def _ssd_kernel(xf_ref, xb_ref, dtf_ref, dtb_ref, a_ref, tri_ref, yf_ref, yb_ref, h_ref):
    @pl.when(pl.program_id(1) == 0)
    def _():
        h_ref[...] = jnp.zeros_like(h_ref)

    left = lax.broadcasted_iota(jnp.int32, (1, LANES), 1) < SSM_HEAD_DIM
    ii = lax.broadcasted_iota(jnp.int32, (CHUNK, CHUNK), 0)
    jj = lax.broadcasted_iota(jnp.int32, (CHUNK, CHUNK), 1)
    b_off = SSM_INNER
    c_off = SSM_INNER + SSM_GN

    for d, (x_ref, dt_ref, y_ref) in enumerate(((xf_ref, dtf_ref, yf_ref), (xb_ref, dtb_ref, yb_ref))):
        dtc = dt_ref[0]
        acum = _dot_exact_lhs(tri_ref[d], dtc * a_ref[...])
        acum_t = acum.T
        mask = (ii >= jj) if d == 0 else (jj >= ii)
        last = acum[CHUNK - 1:CHUNK, :] if d == 0 else acum[0:1, :]
        eac = jnp.exp(acum)
        wend = jnp.exp(last - acum)
        elast = jnp.exp(last)

        def pair(a, c0, n):
            return jnp.where(left, jnp.broadcast_to(a[:, c0:c0 + 1], (n, LANES)),
                             jnp.broadcast_to(a[:, c0 + 1:c0 + 2], (n, LANES)))

        for g in range(SSM_GROUPS):
            bg = x_ref[0, :, b_off + g * SSM_STATE:b_off + (g + 1) * SSM_STATE]
            cg = x_ref[0, :, c_off + g * SSM_STATE:c_off + (g + 1) * SSM_STATE]
            cb = _nt(cg, bg)
            h_t = h_ref[d, g]
            y_int = _mm(cg, h_t.astype(BF16))
            xws, els = [], []
            for hp in range(SSM_HPG // 2):
                c0 = d * SSM_HEADS + g * SSM_HPG + 2 * hp
                ch = slice(g * SSM_GW + hp * LANES, g * SSM_GW + (hp + 1) * LANES)
                xdt = x_ref[0, :, ch].astype(F32) * pair(dtc, c0, CHUNK)
                xdt16 = xdt.astype(BF16)
                yh = []
                for e in range(2):
                    c = c0 + e
                    lmat = jnp.exp(jnp.where(mask, acum[:, c:c + 1] - acum_t[c:c + 1, :], NEG))
                    yh.append(_mm((lmat * cb).astype(BF16), xdt16))
                y_ref[0, :, ch] = (jnp.where(left, yh[0], yh[1])
                                   + y_int[:, hp * LANES:(hp + 1) * LANES] * pair(eac, c0, CHUNK))
                xws.append((xdt * pair(wend, c0, CHUNK)).astype(BF16))
                els.append(pair(elast, c0, 1))
            h_ref[d, g] = jnp.concatenate(els, axis=1) * h_t + _tn(bg, jnp.concatenate(xws, axis=1))


def _ssd(xc, dt, a_row, ctx_len):
    b, t, ch = xc.shape
    ns = t // CHUNK
    fwd, bwd = _scan_orders(ns, ctx_len // CHUNK)
    idx = np.arange(CHUNK)
    tri = jnp.asarray(np.stack([idx[:, None] >= idx[None, :], idx[:, None] <= idx[None, :]]), BF16)

    def spec(order, w):
        return pl.BlockSpec((1, CHUNK, w), lambda bb, i: (bb, order(i), 0))

    return pl.pallas_call(
        _ssd_kernel,
        grid=(b, ns),
        in_specs=[spec(fwd, ch), spec(bwd, ch), spec(fwd, LANES), spec(bwd, LANES),
                  _resident((1, LANES)), _resident((2, CHUNK, CHUNK))],
        out_specs=[spec(fwd, SSM_INNER), spec(bwd, SSM_INNER)],
        out_shape=[jax.ShapeDtypeStruct((b, t, SSM_INNER), F32)] * 2,
        scratch_shapes=[pltpu.VMEM((2, SSM_GROUPS, SSM_STATE, SSM_GW), F32)],
        compiler_params=_params(2),
        name="ssd",
    )(xc, xc, dt, dt, a_row, tri)


def _odd_out_kernel(x_ref, mod_ref, yf_ref, yb_ref, xc_ref, z_ref, dsk_ref, nssm_ref, wo_ref,
                    nmlp_ref, w1_ref, w2_ref, nfin_ref, o_ref):
    m = mod_ref[0, 0]
    y = yf_ref[0] + yb_ref[0] + dsk_ref[...] * xc_ref[0].astype(F32)
    y = y * _silu(z_ref[0].astype(F32))
    y = y * lax.rsqrt(jnp.mean(y * y, axis=-1, keepdims=True) + EPS) * nssm_ref[...]
    x1 = x_ref[0] + m[2:3] * _mm(y.astype(BF16), wo_ref[...])
    x2 = _mlp_tail(x1, m, nmlp_ref[...], w1_ref, w2_ref)
    o_ref[0] = x2 * lax.rsqrt(jnp.mean(x2 * x2, axis=-1, keepdims=True) + EPS) * nfin_ref[...]


def _odd_out(xs, modsel, yf, yb, xc, z, dsk, nssm, wo, nmlp, w1, w2, nfin, ctx_len):
    b, t, _ = xs.shape
    skip = ctx_len // TM
    tok = lambda w: pl.BlockSpec((1, TM, w), lambda bb, tt: (bb, tt + skip, 0))
    mod = pl.BlockSpec((1, 1, 6, D_MODEL), lambda bb, tt: (bb, 1, 0, 0))
    vec = lambda w: _resident((1, w))
    return pl.pallas_call(
        _odd_out_kernel,
        grid=(b, t // TM - skip),
        in_specs=[tok(D_MODEL), mod, tok(SSM_INNER), tok(SSM_INNER), tok(SSM_INNER), tok(SSM_INNER),
                  vec(SSM_INNER), vec(SSM_INNER), _resident(wo.shape), vec(D_MODEL),
                  _resident(w1.shape), _resident(w2.shape), vec(D_MODEL)],
        out_specs=pl.BlockSpec((1, TM, D_MODEL), lambda bb, tt: (bb, tt, 0)),
        out_shape=jax.ShapeDtypeStruct((b, t - ctx_len, D_MODEL), F32),
        compiler_params=_params(2),
        name="odd_out_mlp",
    )(xs, modsel, yf, yb, xc, z, dsk.reshape(1, -1), nssm.reshape(1, -1), wo, nmlp.reshape(1, -1),
      w1, w2, nfin.reshape(1, -1))


def kernel(x, c, ctx, c_ctx, w_mod, b_mod, norm_mix, norm_mlp, w_mlp_in, w_mlp_out, w_in_even, w_out_even,
           ret_decay_logit, na_rpb, w_in_odd, conv_w, conv_b, dt_bias, a_log, d_skip, ssm_norm, w_out_odd,
           norm_final):
    assert w_mod.shape[0] == DEPTH == 2 and x.shape[2] == D_MODEL
    ctx_len = ctx.shape[1]
    seq = x.shape[1]
    assert ctx_len == TM and seq % (2 * GRID_W) == 0

    modsel = _modulation(c, c_ctx, w_mod, b_mod)
    xs = jnp.concatenate([ctx, x], axis=1)
    w1 = w_mlp_in.astype(BF16)
    w2 = w_mlp_out.astype(BF16)

    rq, rk, rv, rg, nq, nk, nv = _even_in(xs, modsel[0], norm_mix[0], w_in_even[0].astype(BF16),
                                          _rope_tables(seq, ctx_len))
    yf, yb = _retention(rq, rk, rv, _ret_tables(ret_decay_logit[0]), ctx_len)
    yna = _na(nq, nk, nv, _na_bias_tables(na_rpb[0]), ctx_len)
    xs = _even_out(xs, modsel[0], yf, yb, rg, yna, w_out_even[0].astype(BF16), norm_mlp[0], w1[0], w2[0])

    n_main = SSM_INNER + SSM_CONV_CH
    wi = w_in_odd[0]
    w_dt = jnp.pad(wi[:, n_main:], ((0, 0), (0, LANES - 2 * SSM_HEADS))).astype(BF16)
    pad_row = lambda v: jnp.pad(v.astype(F32).reshape(1, -1), ((0, 0), (0, LANES - 2 * SSM_HEADS)))
    z, xbc, dt = _odd_in(xs, modsel[1], norm_mix[1], wi[:, :n_main].astype(BF16), w_dt, pad_row(dt_bias[0]))
    xc = _conv(xbc, conv_w[0].astype(F32), conv_b[0].astype(F32), ctx_len)
    yf, yb = _ssd(xc, dt, pad_row(-jnp.exp(a_log[0].astype(F32))), ctx_len)
    dsk = jnp.repeat(d_skip[0].astype(F32), SSM_HEAD_DIM)
    return _odd_out(xs, modsel[1], yf, yb, xc, z, dsk, ssm_norm[0], w_out_odd[0].astype(BF16), norm_mlp[1],
                    w1[1], w2[1], norm_final, ctx_len)
```

```python
import functools

import numpy as np
import jax
import jax.numpy as jnp
from jax import lax
from jax.experimental import pallas as pl
from jax.experimental.pallas import tpu as pltpu

F32 = jnp.float32
BF16 = jnp.bfloat16

D_MODEL = 1024
D_FF = 4 * D_MODEL
DEPTH = 2
GRID_W = 64
EPS = 1e-6
ROPE_BASE = 10000.0

RET_HEADS = 4
RET_DIM = 128
RET_W = RET_HEADS * RET_DIM

NA_HEADS = 8
NA_DIM = 64
NA_W = NA_HEADS * NA_DIM
NA_WIN_R = 8
NA_WIN_C = 16

SSM_INNER = 2 * D_MODEL
SSM_HEAD_DIM = 64
SSM_HEADS = SSM_INNER // SSM_HEAD_DIM
SSM_GROUPS = 4
SSM_HPG = SSM_HEADS // SSM_GROUPS
SSM_STATE = 128
SSM_CONV = 7
SSM_GN = SSM_GROUPS * SSM_STATE
SSM_CONV_CH = SSM_INNER + 2 * SSM_GN
SSM_GW = SSM_HPG * SSM_HEAD_DIM

LANES = 128
SUBLANES = 8
TM = 256
CHUNK = 128
HALO = 32
LOG2E = 1.4426950408889634
NEG = -1e30
NA_KROWS = 10
VMEM_LIMIT = 56 * 1024 * 1024


def _params(n_axes):
    return pltpu.CompilerParams(dimension_semantics=("arbitrary",) * n_axes, vmem_limit_bytes=VMEM_LIMIT)


def _resident(shape):
    nd = len(shape)
    return pl.BlockSpec(shape, lambda *_: (0,) * nd, pipeline_mode=pl.Buffered(1))


def _nt(a, b):
    return lax.dot_general(a, b, (((1,), (1,)), ((), ())), preferred_element_type=F32)


def _tn(a, b):
    return lax.dot_general(a, b, (((0,), (0,)), ((), ())), preferred_element_type=F32)


def _mm(a, b):
    return jnp.dot(a, b, preferred_element_type=F32)


def _silu(v):
    return v * jax.nn.sigmoid(v)


def _norm_mod(x, g, sc, sh):
    ms = jnp.mean(x * x, axis=-1, keepdims=True)
    return (x * lax.rsqrt(ms + EPS) * g) * (1.0 + sc) + sh


def _mod_kernel(cc_ref, w_ref, b_ref, o_ref):
    s = _silu(cc_ref[...])
    o_ref[0] = _mm(s.astype(BF16), w_ref[0].astype(BF16)) + b_ref[0]


def _modulation(c, c_ctx, w_mod, b_mod):
    b = c.shape[0]
    rows = -(-(b + 1) // 8) * 8
    cc = jnp.zeros((rows, D_MODEL), F32).at[:b].set(c).at[b].set(c_ctx)
    tn = 1536
    out = pl.pallas_call(
        _mod_kernel,
        grid=(DEPTH, 6 * D_MODEL // tn),
        in_specs=[pl.BlockSpec((rows, D_MODEL), lambda l, j: (0, 0)),
                  pl.BlockSpec((1, D_MODEL, tn), lambda l, j: (l, 0, j)),
                  pl.BlockSpec((1, 1, tn), lambda l, j: (l, 0, j))],
        out_specs=pl.BlockSpec((1, rows, tn), lambda l, j: (l, 0, j)),
        out_shape=jax.ShapeDtypeStruct((DEPTH, rows, 6 * D_MODEL), F32),
        compiler_params=_params(2),
        name="modulation",
    )(cc, w_mod, b_mod.reshape(DEPTH, 1, 6 * D_MODEL))
    mod = out.reshape(DEPTH, rows, 6, D_MODEL)
    return jnp.stack([jnp.broadcast_to(mod[:, b:b + 1], (DEPTH, b, 6, D_MODEL)), mod[:, :b]], axis=2)


def _mod_spec():
    return pl.BlockSpec((1, 1, 6, D_MODEL), lambda b, t: (b, jnp.minimum(t, 1), 0, 0))


def _rope_tables(seq, ctx_len):
    pos = jnp.arange(seq)
    nf = RET_DIM // 4
    inv = ROPE_BASE ** (-jnp.arange(nf, dtype=F32) / nf)

    def cs(p):
        ang = p.astype(F32)[:, None] * inv[None, :]
        return jnp.cos(ang), jnp.sin(ang)

    cr, sr = cs(pos // GRID_W)
    cc, sc = cs(pos % GRID_W)
    zero = jnp.zeros_like(sr)
    cos = jnp.concatenate([cr, cr, cc, cc], axis=1)
    sin_a = jnp.concatenate([-sr, zero, -sc, zero], axis=1)
    sin_b = jnp.concatenate([zero, sr, zero, sc], axis=1)
    ident = jnp.ones((ctx_len, RET_DIM), F32)
    nul = jnp.zeros((ctx_len, RET_DIM), F32)
    return (jnp.concatenate([ident, cos], axis=0), jnp.concatenate([nul, sin_a], axis=0),
            jnp.concatenate([nul, sin_b], axis=0))


def _even_in_kernel(x_ref, mod_ref, g_ref, w_ref, cos_ref, sa_ref, sb_ref,
                    rq_ref, rk_ref, rv_ref, rg_ref, nq_ref, nk_ref, nv_ref):
    m = mod_ref[0, 0]
    h = _norm_mod(x_ref[0], g_ref[...], m[1:2], m[0:1]).astype(BF16)
    cos, sa, sb = cos_ref[...], sa_ref[...], sb_ref[...]

    def proj(j):
        return _mm(h, w_ref[:, j * RET_W:(j + 1) * RET_W])

    def rope_store(o_ref, r, scale):
        for hh in range(RET_HEADS):
            t = r[:, hh * RET_DIM:(hh + 1) * RET_DIM]
            o = t * cos + pltpu.roll(t, RET_DIM - 32, 1) * sa + pltpu.roll(t, 32, 1) * sb
            o_ref[0, :, hh * RET_DIM:(hh + 1) * RET_DIM] = (o * scale).astype(BF16)

    rope_store(rq_ref, proj(0), 1.0)
    rope_store(rk_ref, proj(1), RET_DIM ** -0.5)
    rv_ref[0] = proj(2).astype(BF16)
    rg_ref[0] = proj(3).astype(BF16)
    nq_ref[0] = (proj(4) * NA_DIM ** -0.5).astype(BF16)
    nk_ref[0] = proj(5).astype(BF16)
    nv_ref[0] = proj(6).astype(BF16)


def _even_in(xs, modsel, g, w, tables):
    b, t, _ = xs.shape
    tok = pl.BlockSpec((1, TM, D_MODEL), lambda bb, tt: (bb, tt, 0))
    tab = pl.BlockSpec((TM, RET_DIM), lambda bb, tt: (tt, 0))
    out = pl.BlockSpec((1, TM, RET_W), lambda bb, tt: (bb, tt, 0))
    return pl.pallas_call(
        _even_in_kernel,
        grid=(b, t // TM),
        in_specs=[tok, _mod_spec(), _resident((1, D_MODEL)), _resident(w.shape), tab, tab, tab],
        out_specs=[out] * 7,
        out_shape=[jax.ShapeDtypeStruct((b, t, RET_W), BF16)] * 7,
        compiler_params=_params(2),
        name="even_in",
    )(xs, modsel, g.reshape(1, D_MODEL), w, *tables)


def _ret_tables(decay_logit):
    log_g = jax.nn.log_sigmoid(decay_logit.astype(F32))
    idx = jnp.arange(CHUNK, dtype=F32)
    diff = idx[:, None] - idx[None, :]
    ones = jnp.ones((CHUNK, CHUNK), F32)

    def one(lg, d):
        dd = diff if d == 0 else -diff
        dm = jnp.where(dd >= 0, jnp.exp(jnp.maximum(dd, 0.0) * lg), 0.0)
        qe = (idx + 1.0) if d == 0 else (CHUNK - idx)
        ke = (CHUNK - 1.0 - idx) if d == 0 else idx
        return jnp.stack([dm, jnp.exp(qe * lg)[:, None] * ones, jnp.exp(ke * lg)[:, None] * ones,
                          jnp.exp(CHUNK * lg) * ones])

    return jnp.stack([jnp.stack([one(log_g[d, h], d) for h in range(RET_HEADS)]) for d in range(2)])


def _ret_kernel(qf_ref, kf_ref, vf_ref, qb_ref, kb_ref, vb_ref, dec_ref, yf_ref, yb_ref, s_ref):
    @pl.when(pl.program_id(1) == 0)
    def _():
        s_ref[...] = jnp.zeros_like(s_ref)

    for d, (q_ref, k_ref, v_ref, y_ref) in enumerate(((qf_ref, kf_ref, vf_ref, yf_ref),
                                                      (qb_ref, kb_ref, vb_ref, yb_ref))):
        for hh in range(RET_HEADS):
            sl = slice(hh * RET_DIM, (hh + 1) * RET_DIM)
            q, k, v = q_ref[0, :, sl], k_ref[0, :, sl], v_ref[0, :, sl]
            s = s_ref[d, hh]
            att = _nt(q, k) * dec_ref[d, hh, 0]
            y_ref[0, :, sl] = _mm(att.astype(BF16), v) + _mm(q, s.astype(BF16)) * dec_ref[d, hh, 1]
            kk = (k.astype(F32) * dec_ref[d, hh, 2]).astype(BF16)
            s_ref[d, hh] = dec_ref[d, hh, 3] * s + _tn(kk, v)


def _scan_orders(n_steps, n_ctx):
    fwd = lambda i: i
    bwd = lambda i: jnp.where(i < n_ctx, n_ctx - 1 - i, n_steps + n_ctx - 1 - i)
    return fwd, bwd


def _retention(rq, rk, rv, dec, ctx_len):
    b, t, _ = rq.shape
    ns = t // CHUNK
    fwd, bwd = _scan_orders(ns, ctx_len // CHUNK)

    def spec(order):
        return pl.BlockSpec((1, CHUNK, RET_W), lambda bb, i: (bb, order(i), 0))

    return pl.pallas_call(
        _ret_kernel,
        grid=(b, ns),
        in_specs=[spec(fwd)] * 3 + [spec(bwd)] * 3 + [_resident(dec.shape)],
        out_specs=[spec(fwd), spec(bwd)],
        out_shape=[jax.ShapeDtypeStruct((b, t, RET_W), F32)] * 2,
        scratch_shapes=[pltpu.VMEM((2, RET_HEADS, RET_DIM, RET_DIM), F32)],
        compiler_params=_params(2),
        name="retention",
    )(rq, rk, rv, rq, rk, rv, dec)


def _na_bias_tables(rpb):
    qc = np.arange(GRID_W)[:, None]
    kc = np.arange(GRID_W)[None, :]
    cstart = np.clip(qc - NA_WIN_C // 2, 0, GRID_W - NA_WIN_C)
    inwin = (kc >= cstart) & (kc < cstart + NA_WIN_C)
    dcol = np.clip(kc - qc, -(NA_WIN_C - 1), NA_WIN_C - 1) + (NA_WIN_C - 1)
    t = jnp.where(jnp.asarray(inwin), rpb.astype(F32)[:, :, dcol], NEG)
    pad = jnp.full((NA_HEADS, 2, GRID_W, GRID_W), NEG, F32)
    t = jnp.concatenate([pad, t, pad], axis=1)
    return jnp.concatenate([t[:, :-1], t[:, 1:]], axis=-1)


def _na_kernel(q_ref, k_ref, v_ref, tbl_ref, o_ref, *, rows, ctx_len):
    s = pl.program_id(1)
    n_ctx = ctx_len // CHUNK
    left = lax.broadcasted_iota(jnp.int32, (1, LANES), 1) < NA_DIM
    q = q_ref[0]
    n_loc = NA_KROWS * GRID_W

    def attend(kloc, vloc, bias_of_head):
        for hp in range(NA_HEADS // 2):
            sl = slice(hp * LANES, (hp + 1) * LANES)
            qp = q[:, sl]
            kc, vc = k_ref[0, 0:ctx_len, sl], v_ref[0, 0:ctx_len, sl]
            outs = []
            for e in range(2):
                qm = jnp.where(left if e == 0 else jnp.logical_not(left), qp, jnp.zeros_like(qp))
                s_ctx = _nt(qm, kc)
                mx = jnp.max(s_ctx, axis=-1, keepdims=True)
                if kloc is not None:
                    s_loc = _nt(qm, kloc[:, sl]) + bias_of_head(2 * hp + e)
                    mx = jnp.maximum(mx, jnp.max(s_loc, axis=-1, keepdims=True))
                    p_loc = jnp.exp(s_loc - mx)
                p_ctx = jnp.exp(s_ctx - mx)
                den = jnp.sum(p_ctx, axis=-1, keepdims=True)
                o = _mm(p_ctx.astype(BF16), vc)
                if kloc is not None:
                    den = den + jnp.sum(p_loc, axis=-1, keepdims=True)
                    o = o + _mm(p_loc.astype(BF16), vloc[:, sl])
                outs.append(o / den)
            o_ref[0, :, sl] = jnp.where(left, outs[0], outs[1]).astype(BF16)

    @pl.when(s < n_ctx)
    def _():
        attend(None, None, None)

    @pl.when(s >= n_ctx)
    def _():
        r0 = 2 * (s - n_ctx)
        kb = jnp.clip(r0 - NA_WIN_R // 2, 0, rows - NA_KROWS)
        off = pl.multiple_of(ctx_len + kb * GRID_W, GRID_W)
        kloc = k_ref[0, pl.ds(off, n_loc), :]
        vloc = v_ref[0, pl.ds(off, n_loc), :]
        idx, pen = [], []
        for qr in range(2):
            r = r0 + qr
            rs = jnp.clip(r - NA_WIN_R // 2, 0, rows - NA_WIN_R)
            for mm in range(NA_KROWS // 2):
                kp = kb + 2 * mm
                ok_l = jnp.logical_and(kp >= rs, kp < rs + NA_WIN_R)
                ok_r = jnp.logical_and(kp + 1 >= rs, kp + 1 < rs + NA_WIN_R)
                idx.append(kp - r + NA_WIN_R + 1)
                pen.append(jnp.where(left, jnp.where(ok_l, 0.0, NEG), jnp.where(ok_r, 0.0, NEG)))

        def bias_of_head(h):
            rows_b = []
            for qr in range(2):
                blocks = []
                for mm in range(NA_KROWS // 2):
                    n = qr * (NA_KROWS // 2) + mm
                    blocks.append(tbl_ref[h, idx[n]] + pen[n])
                rows_b.append(jnp.concatenate(blocks, axis=1))
            return jnp.concatenate(rows_b, axis=0)

        attend(kloc, vloc, bias_of_head)


def _na(nq, nk, nv, tbl, ctx_len):
    b, t, _ = nq.shape
    rows = (t - ctx_len) // GRID_W
    assert rows >= NA_KROWS and rows % 2 == 0 and CHUNK == 2 * GRID_W
    full = pl.BlockSpec((1, t, NA_W), lambda bb, s: (bb, 0, 0))
    blk = pl.BlockSpec((1, CHUNK, NA_W), lambda bb, s: (bb, s, 0))
    return pl.pallas_call(
        functools.partial(_na_kernel, rows=rows, ctx_len=ctx_len),
        grid=(b, t // CHUNK),
        in_specs=[blk, full, full, _resident(tbl.shape)],
        out_specs=blk,
        out_shape=jax.ShapeDtypeStruct((b, t, NA_W), BF16),
        compiler_params=_params(2),
        name="na",
    )(nq, nk, nv, tbl)


def _mlp_tail(x1, m, nmlp, w1_ref, w2_ref):
    h2 = _norm_mod(x1, nmlp, m[4:5], m[3:4]).astype(BF16)
    fc = D_MODEL
    acc = jnp.zeros_like(x1)
    for j in range(D_FF // fc):
        u = _mm(h2, w1_ref[:, j * fc:(j + 1) * fc])
        u = jnp.square(jnp.maximum(u, 0.0)).astype(BF16)
        acc = acc + _mm(u, w2_ref[j * fc:(j + 1) * fc, :])
    return x1 + m[5:6] * acc


def _even_out_kernel(x_ref, mod_ref, yf_ref, yb_ref, rg_ref, yna_ref, wo_ref, nmlp_ref, w1_ref, w2_ref, o_ref):
    m = mod_ref[0, 0]
    y = yf_ref[0] + yb_ref[0]
    g = rg_ref[0].astype(F32)
    mix = _mm(yna_ref[0], wo_ref[RET_W:RET_W + NA_W, :])
    for hh in range(RET_HEADS):
        sl = slice(hh * RET_DIM, (hh + 1) * RET_DIM)
        yh = y[:, sl]
        dlt = yh - jnp.mean(yh, axis=-1, keepdims=True)
        yn = dlt * lax.rsqrt(jnp.mean(dlt * dlt, axis=-1, keepdims=True) + EPS)
        mix = mix + _mm((yn * _silu(g[:, sl])).astype(BF16), wo_ref[sl, :])
    x1 = x_ref[0] + m[2:3] * mix
    o_ref[0] = _mlp_tail(x1, m, nmlp_ref[...], w1_ref, w2_ref)


def _even_out(xs, modsel, yf, yb, rg, yna, wo, nmlp, w1, w2):
    b, t, _ = xs.shape
    tok = lambda w: pl.BlockSpec((1, TM, w), lambda bb, tt: (bb, tt, 0))
    return pl.pallas_call(
        _even_out_kernel,
        grid=(b, t // TM),
        in_specs=[tok(D_MODEL), _mod_spec(), tok(RET_W), tok(RET_W), tok(RET_W), tok(NA_W),
                  _resident(wo.shape), _resident((1, D_MODEL)), _resident(w1.shape), _resident(w2.shape)],
        out_specs=tok(D_MODEL),
        out_shape=jax.ShapeDtypeStruct((b, t, D_MODEL), F32),
        compiler_params=_params(2),
        name="even_out_mlp",
    )(xs, modsel, yf, yb, rg, yna, wo, nmlp.reshape(1, D_MODEL), w1, w2)


def _conv_perm():
    span = TM // SUBLANES
    rho = np.arange(TM)
    p = np.zeros((TM, TM), np.float32)
    p[rho, (rho % SUBLANES) * span + rho // SUBLANES] = 1.0
    return jnp.asarray(p, BF16)


def _odd_in_kernel(x_ref, mod_ref, g_ref, w_ref, wdt_ref, dtb_ref, perm_ref, z_ref, xbc_ref, dt_ref):
    m = mod_ref[0, 0]
    h = _norm_mod(x_ref[0], g_ref[...], m[1:2], m[0:1]).astype(BF16)
    cw = 512
    for j in range(SSM_INNER // cw):
        z_ref[0, :, j * cw:(j + 1) * cw] = _mm(h, w_ref[:, j * cw:(j + 1) * cw]).astype(BF16)
    hp = _mm(perm_ref[...], h).astype(BF16)
    for j in range(SSM_CONV_CH // cw):
        c0 = SSM_INNER + j * cw
        xbc_ref[0, :, j * cw:(j + 1) * cw] = _mm(hp, w_ref[:, c0:c0 + cw]).astype(BF16)
    raw = _mm(h, wdt_ref[...]) + dtb_ref[...]
    dt_ref[0] = jnp.maximum(raw, 0.0) + jnp.log1p(jnp.exp(-jnp.abs(raw)))


def _odd_in(xs, modsel, g, w_main, w_dt, dt_bias):
    b, t, _ = xs.shape
    tok = lambda w: pl.BlockSpec((1, TM, w), lambda bb, tt: (bb, tt, 0))
    return pl.pallas_call(
        _odd_in_kernel,
        grid=(b, t // TM),
        in_specs=[tok(D_MODEL), _mod_spec(), _resident((1, D_MODEL)), _resident(w_main.shape),
                  _resident(w_dt.shape), _resident((1, LANES)), _resident((TM, TM))],
        out_specs=[tok(SSM_INNER), tok(SSM_CONV_CH), tok(LANES)],
        out_shape=[jax.ShapeDtypeStruct((b, t, SSM_INNER), BF16),
                   jax.ShapeDtypeStruct((b, t, SSM_CONV_CH), BF16),
                   jax.ShapeDtypeStruct((b, t, LANES), F32)],
        compiler_params=_params(2),
        name="odd_in",
    )(xs, modsel, g.reshape(1, D_MODEL), w_main, w_dt, dt_bias, _conv_perm())


def _conv_kernel(main_ref, prev_ref, next_ref, w_ref, b_ref, unperm_ref, o_ref, act_ref, *, n_tiles):
    t = pl.program_id(1)
    has_prev = (t >= 2).astype(F32)
    has_next = jnp.logical_and(t >= 1, t < n_tiles - 1).astype(F32)
    span = TM // SUBLANES
    half = SSM_CONV // 2
    sub = lax.broadcasted_iota(jnp.int32, (SUBLANES, LANES), 0)

    def lane_tile(j, carry):
        lanes = pl.ds(pl.multiple_of(j * LANES, LANES), LANES)
        xm = main_ref[0, :, lanes].astype(F32)
        pv = prev_ref[0, :, lanes].astype(F32) * has_prev
        nx = next_ref[0, :, lanes].astype(F32) * has_next
        reg = {m: xm[m * SUBLANES:(m + 1) * SUBLANES] for m in range(span)}
        for i in range(1, half + 1):
            row = HALO - 1 - (i - 1) * SUBLANES
            edge = jnp.broadcast_to(pv[row:row + 1], (SUBLANES, LANES))
            reg[-i] = jnp.where(sub == 0, edge, pltpu.roll(reg[span - i], 1, 0))
            row = (i - 1) * SUBLANES
            edge = jnp.broadcast_to(nx[row:row + 1], (SUBLANES, LANES))
            reg[span + i - 1] = jnp.where(sub == SUBLANES - 1, edge, pltpu.roll(reg[i - 1], SUBLANES - 1, 0))
        wk = [jnp.broadcast_to(w_ref[k:k + 1, lanes], (SUBLANES, LANES)) for k in range(SSM_CONV)]
        bias = jnp.broadcast_to(b_ref[:, lanes], (SUBLANES, LANES))
        acts = []
        for m in range(span):
            acc = bias
            for k in range(SSM_CONV):
                acc = acc + wk[k] * reg[m + k - half]
            acts.append(_silu(acc))
        act_ref[:, lanes] = jnp.concatenate(acts, axis=0).astype(BF16)
        return carry

    lax.fori_loop(0, SSM_CONV_CH // LANES, lane_tile, 0, unroll=2)
    o_ref[0] = _mm(unperm_ref[...], act_ref[...]).astype(BF16)


def _conv(xbc, w, bias, ctx_len):
    b, t, ch = xbc.shape
    assert ctx_len == TM and TM % HALO == 0 and HALO >= (SSM_CONV // 2) * SUBLANES
    nt = t // TM
    per = TM // HALO
    main = pl.BlockSpec((1, TM, ch), lambda bb, tt: (bb, tt, 0))
    prev = pl.BlockSpec((1, HALO, ch), lambda bb, tt: (bb, jnp.maximum(tt * per - 1, 0), 0))
    nxt = pl.BlockSpec((1, HALO, ch), lambda bb, tt: (bb, jnp.minimum((tt + 1) * per, nt * per - 1), 0))
    return pl.pallas_call(
        functools.partial(_conv_kernel, n_tiles=nt),
        grid=(b, nt),
        in_specs=[main, prev, nxt, _resident(w.shape), _resident((1, ch)), _resident((TM, TM))],
        out_specs=main,
        out_shape=jax.ShapeDtypeStruct((b, t, ch), BF16),
        scratch_shapes=[pltpu.VMEM((TM, ch), BF16)],
        compiler_params=_params(2),
        name="dwconv_silu",
    )(xbc, xbc, xbc, w, bias.reshape(1, ch), _conv_perm().T)


def _dot_exact_lhs(a16, x):
    hi = x.astype(BF16)
    r1 = x - hi.astype(F32)
    mid = r1.astype(BF16)
    lo = (r1 - mid.astype(F32)).astype(BF16)
    return _mm(a16, hi) + _mm(a16, mid) + _mm(a16, lo)


def _ssd_kernel(xf_ref, xb_ref, dtf_ref, dtb_ref, a_ref, tri_ref, yf_ref, yb_ref, h_ref, row_ref):
    @pl.when(pl.program_id(1) == 0)
    def _():
        h_ref[...] = jnp.zeros_like(h_ref)

    left = lax.broadcasted_iota(jnp.int32, (1, LANES), 1) < SSM_HEAD_DIM
    m_left = jnp.where(left, 1.0, 0.0).astype(BF16)
    m_right = jnp.where(left, 0.0, 1.0).astype(BF16)
    ii = lax.broadcasted_iota(jnp.int32, (CHUNK, CHUNK), 0)
    jj = lax.broadcasted_iota(jnp.int32, (CHUNK, CHUNK), 1)
    b_off = SSM_INNER
    c_off = SSM_INNER + SSM_GN

    for d, (x_ref, dt_ref, y_ref) in enumerate(((xf_ref, dtf_ref, yf_ref), (xb_ref, dtb_ref, yb_ref))):
        dtc = dt_ref[0]
        acum = _dot_exact_lhs(tri_ref[d], dtc * a_ref[...]) * LOG2E
        acum_t = acum.T
        dt_t = dtc.T
        mask = (ii >= jj) if d == 0 else (jj >= ii)
        last = acum[CHUNK - 1:CHUNK, :] if d == 0 else acum[0:1, :]
        last_t = acum_t[:, CHUNK - 1:CHUNK] if d == 0 else acum_t[:, 0:1]
        elast = jnp.exp2(last)
        row_ref[d, 0] = acum_t
        row_ref[d, 1] = dt_t
        row_ref[d, 2] = dt_t * jnp.exp2(last_t - acum_t)

        for g in range(SSM_GROUPS):
            bg = x_ref[0, :, b_off + g * SSM_STATE:b_off + (g + 1) * SSM_STATE]
            cg = x_ref[0, :, c_off + g * SSM_STATE:c_off + (g + 1) * SSM_STATE]
            cb = _nt(cg, bg)
            bg_t = bg.astype(F32).T
            h_t = h_ref[d, g]
            y_int = _mm(cg, h_t.astype(BF16))
            for hp in range(SSM_HPG // 2):
                c0 = d * SSM_HEADS + g * SSM_HPG + 2 * hp
                ch = slice(g * SSM_GW + hp * LANES, g * SSM_GW + (hp + 1) * LANES)
                hl = slice(hp * LANES, (hp + 1) * LANES)
                xp = x_ref[0, :, ch]
                xbd = jnp.concatenate([xp * m_left, xp * m_right], axis=0)
                intra, upd, eac = [], [], []
                for e in range(2):
                    c = c0 + e
                    acol = jnp.broadcast_to(acum[:, c:c + 1], (CHUNK, CHUNK))
                    lmat = jnp.exp2(jnp.where(mask, acol - row_ref[d, 0, c:c + 1, :], NEG))
                    intra.append((lmat * cb * row_ref[d, 1, c:c + 1, :]).astype(BF16))
                    upd.append((bg_t * row_ref[d, 2, c:c + 1, :]).astype(BF16))
                    eac.append(jnp.exp2(acol))
                lhs = jnp.concatenate([jnp.concatenate(intra, axis=1), jnp.concatenate(upd, axis=1)], axis=0)
                r = _mm(lhs, xbd)
                y_ref[0, :, ch] = r[:CHUNK] + y_int[:, hl] * jnp.where(left, eac[0], eac[1])
                el = jnp.where(left, jnp.broadcast_to(elast[:, c0:c0 + 1], (1, LANES)),
                               jnp.broadcast_to(elast[:, c0 + 1:c0 + 2], (1, LANES)))
                h_ref[d, g, :, hl] = el * h_t[:, hl] + r[CHUNK:]


def _ssd(xc, dt, a_row, ctx_len):
    b, t, ch = xc.shape
    ns = t // CHUNK
    fwd, bwd = _scan_orders(ns, ctx_len // CHUNK)
    idx = np.arange(CHUNK)
    tri = jnp.asarray(np.stack([idx[:, None] >= idx[None, :], idx[:, None] <= idx[None, :]]), BF16)

    def spec(order, w):
        return pl.BlockSpec((1, CHUNK, w), lambda bb, i: (bb, order(i), 0))

    return pl.pallas_call(
        _ssd_kernel,
        grid=(b, ns),
        in_specs=[spec(fwd, ch), spec(bwd, ch), spec(fwd, LANES), spec(bwd, LANES),
                  _resident((1, LANES)), _resident((2, CHUNK, CHUNK))],
        out_specs=[spec(fwd, SSM_INNER), spec(bwd, SSM_INNER)],
        out_shape=[jax.ShapeDtypeStruct((b, t, SSM_INNER), F32)] * 2,
        scratch_shapes=[pltpu.VMEM((2, SSM_GROUPS, SSM_STATE, SSM_GW), F32),
                        pltpu.VMEM((2, 3, CHUNK, CHUNK), F32)],
        compiler_params=_params(2),
        name="ssd",
    )(xc, xc, dt, dt, a_row, tri)


def _odd_out_kernel(x_ref, mod_ref, yf_ref, yb_ref, xc_ref, z_ref, dsk_ref, nssm_ref, wo_ref,
                    nmlp_ref, w1_ref, w2_ref, nfin_ref, o_ref):
    m = mod_ref[0, 0]
    y = yf_ref[0] + yb_ref[0] + dsk_ref[...] * xc_ref[0].astype(F32)
    y = y * _silu(z_ref[0].astype(F32))
    y = y * lax.rsqrt(jnp.mean(y * y, axis=-1, keepdims=True) + EPS) * nssm_ref[...]
    x1 = x_ref[0] + m[2:3] * _mm(y.astype(BF16), wo_ref[...])
    x2 = _mlp_tail(x1, m, nmlp_ref[...], w1_ref, w2_ref)
    o_ref[0] = x2 * lax.rsqrt(jnp.mean(x2 * x2, axis=-1, keepdims=True) + EPS) * nfin_ref[...]


def _odd_out(xs, modsel, yf, yb, xc, z, dsk, nssm, wo, nmlp, w1, w2, nfin, ctx_len):
    b, t, _ = xs.shape
    skip = ctx_len // TM
    tok = lambda w: pl.BlockSpec((1, TM, w), lambda bb, tt: (bb, tt + skip, 0))
    mod = pl.BlockSpec((1, 1, 6, D_MODEL), lambda bb, tt: (bb, 1, 0, 0))
    vec = lambda w: _resident((1, w))
    return pl.pallas_call(
        _odd_out_kernel,
        grid=(b, t // TM - skip),
        in_specs=[tok(D_MODEL), mod, tok(SSM_INNER), tok(SSM_INNER), tok(SSM_INNER), tok(SSM_INNER),
                  vec(SSM_INNER), vec(SSM_INNER), _resident(wo.shape), vec(D_MODEL),
                  _resident(w1.shape), _resident(w2.shape), vec(D_MODEL)],
        out_specs=pl.BlockSpec((1, TM, D_MODEL), lambda bb, tt: (bb, tt, 0)),
        out_shape=jax.ShapeDtypeStruct((b, t - ctx_len, D_MODEL), F32),
        compiler_params=_params(2),
        name="odd_out_mlp",
    )(xs, modsel, yf, yb, xc, z, dsk.reshape(1, -1), nssm.reshape(1, -1), wo, nmlp.reshape(1, -1),
      w1, w2, nfin.reshape(1, -1))


def kernel(x, c, ctx, c_ctx, w_mod, b_mod, norm_mix, norm_mlp, w_mlp_in, w_mlp_out, w_in_even, w_out_even,
           ret_decay_logit, na_rpb, w_in_odd, conv_w, conv_b, dt_bias, a_log, d_skip, ssm_norm, w_out_odd,
           norm_final):
    assert w_mod.shape[0] == DEPTH == 2 and x.shape[2] == D_MODEL
    ctx_len = ctx.shape[1]
    seq = x.shape[1]
    assert ctx_len == TM and seq % (2 * GRID_W) == 0

    modsel = _modulation(c, c_ctx, w_mod, b_mod)
    xs = jnp.concatenate([ctx, x], axis=1)
    w1 = w_mlp_in.astype(BF16)
    w2 = w_mlp_out.astype(BF16)

    rq, rk, rv, rg, nq, nk, nv = _even_in(xs, modsel[0], norm_mix[0], w_in_even[0].astype(BF16),
                                          _rope_tables(seq, ctx_len))
    yf, yb = _retention(rq, rk, rv, _ret_tables(ret_decay_logit[0]), ctx_len)
    yna = _na(nq, nk, nv, _na_bias_tables(na_rpb[0]), ctx_len)
    xs = _even_out(xs, modsel[0], yf, yb, rg, yna, w_out_even[0].astype(BF16), norm_mlp[0], w1[0], w2[0])

    n_main = SSM_INNER + SSM_CONV_CH
    wi = w_in_odd[0]
    w_dt = jnp.pad(wi[:, n_main:], ((0, 0), (0, LANES - 2 * SSM_HEADS))).astype(BF16)
    pad_row = lambda v: jnp.pad(v.astype(F32).reshape(1, -1), ((0, 0), (0, LANES - 2 * SSM_HEADS)))
    z, xbc, dt = _odd_in(xs, modsel[1], norm_mix[1], wi[:, :n_main].astype(BF16), w_dt, pad_row(dt_bias[0]))
    xc = _conv(xbc, conv_w[0].astype(F32), conv_b[0].astype(F32), ctx_len)
    yf, yb = _ssd(xc, dt, pad_row(-jnp.exp(a_log[0].astype(F32))), ctx_len)
    dsk = jnp.repeat(d_skip[0].astype(F32), SSM_HEAD_DIM)
    return _odd_out(xs, modsel[1], yf, yb, xc, z, dsk, ssm_norm[0], w_out_odd[0].astype(BF16), norm_mlp[1],
                    w1[1], w2[1], norm_final, ctx_len)
```

```python
import functools

import numpy as np
import jax
import jax.numpy as jnp
from jax import lax
from jax.experimental import pallas as pl
from jax.experimental.pallas import tpu as pltpu

F32 = jnp.float32
BF16 = jnp.bfloat16

D_MODEL = 1024
D_FF = 4 * D_MODEL
DEPTH = 2
GRID_W = 64
EPS = 1e-6
ROPE_BASE = 10000.0

RET_HEADS = 4
RET_DIM = 128
RET_W = RET_HEADS * RET_DIM

NA_HEADS = 8
NA_DIM = 64
NA_W = NA_HEADS * NA_DIM
NA_WIN_R = 8
NA_WIN_C = 16

SSM_INNER = 2 * D_MODEL
SSM_HEAD_DIM = 64
SSM_HEADS = SSM_INNER // SSM_HEAD_DIM
SSM_GROUPS = 4
SSM_HPG = SSM_HEADS // SSM_GROUPS
SSM_STATE = 128
SSM_CONV = 7
SSM_GN = SSM_GROUPS * SSM_STATE
SSM_CONV_CH = SSM_INNER + 2 * SSM_GN
SSM_GW = SSM_HPG * SSM_HEAD_DIM

LANES = 128
SUBLANES = 8
TM = 256
CHUNK = 128
HALO = 32
LOG2E = 1.4426950408889634
NEG = -1e30
NA_KROWS = 10
VMEM_LIMIT = 56 * 1024 * 1024


def _params(n_axes):
    return pltpu.CompilerParams(dimension_semantics=("arbitrary",) * n_axes, vmem_limit_bytes=VMEM_LIMIT)


def _resident(shape):
    nd = len(shape)
    return pl.BlockSpec(shape, lambda *_: (0,) * nd, pipeline_mode=pl.Buffered(1))


def _nt(a, b):
    return lax.dot_general(a, b, (((1,), (1,)), ((), ())), preferred_element_type=F32)


def _tn(a, b):
    return lax.dot_general(a, b, (((0,), (0,)), ((), ())), preferred_element_type=F32)


def _mm(a, b):
    return jnp.dot(a, b, preferred_element_type=F32)


def _silu(v):
    return v * jax.nn.sigmoid(v)


def _norm_mod(x, g, sc, sh):
    ms = jnp.mean(x * x, axis=-1, keepdims=True)
    return (x * lax.rsqrt(ms + EPS) * g) * (1.0 + sc) + sh


def _mod_kernel(cc_ref, w_ref, b_ref, o_ref):
    s = _silu(cc_ref[...])
    o_ref[0] = _mm(s.astype(BF16), w_ref[0].astype(BF16)) + b_ref[0]


def _modulation(c, c_ctx, w_mod, b_mod):
    b = c.shape[0]
    rows = -(-(b + 1) // 8) * 8
    cc = jnp.zeros((rows, D_MODEL), F32).at[:b].set(c).at[b].set(c_ctx)
    tn = 1536
    out = pl.pallas_call(
        _mod_kernel,
        grid=(DEPTH, 6 * D_MODEL // tn),
        in_specs=[pl.BlockSpec((rows, D_MODEL), lambda l, j: (0, 0)),
                  pl.BlockSpec((1, D_MODEL, tn), lambda l, j: (l, 0, j)),
                  pl.BlockSpec((1, 1, tn), lambda l, j: (l, 0, j))],
        out_specs=pl.BlockSpec((1, rows, tn), lambda l, j: (l, 0, j)),
        out_shape=jax.ShapeDtypeStruct((DEPTH, rows, 6 * D_MODEL), F32),
        compiler_params=_params(2),
        name="modulation",
    )(cc, w_mod, b_mod.reshape(DEPTH, 1, 6 * D_MODEL))
    mod = out.reshape(DEPTH, rows, 6, D_MODEL)
    return jnp.stack([jnp.broadcast_to(mod[:, b:b + 1], (DEPTH, b, 6, D_MODEL)), mod[:, :b]], axis=2)


def _mod_spec():
    return pl.BlockSpec((1, 1, 6, D_MODEL), lambda b, t: (b, jnp.minimum(t, 1), 0, 0))


def _rope_tables(seq, ctx_len):
    pos = jnp.arange(seq)
    nf = RET_DIM // 4
    inv = ROPE_BASE ** (-jnp.arange(nf, dtype=F32) / nf)

    def cs(p):
        ang = p.astype(F32)[:, None] * inv[None, :]
        return jnp.cos(ang), jnp.sin(ang)

    cr, sr = cs(pos // GRID_W)
    cc, sc = cs(pos % GRID_W)
    zero = jnp.zeros_like(sr)
    cos = jnp.concatenate([cr, cr, cc, cc], axis=1)
    sin_a = jnp.concatenate([-sr, zero, -sc, zero], axis=1)
    sin_b = jnp.concatenate([zero, sr, zero, sc], axis=1)
    ident = jnp.ones((ctx_len, RET_DIM), F32)
    nul = jnp.zeros((ctx_len, RET_DIM), F32)
    return (jnp.concatenate([ident, cos], axis=0), jnp.concatenate([nul, sin_a], axis=0),
            jnp.concatenate([nul, sin_b], axis=0))


def _stream_tile(ctx_ref, x_ref):
    return jnp.where(pl.program_id(1) == 0, ctx_ref[0], x_ref[0])


def _stream_specs():
    return [pl.BlockSpec((1, TM, D_MODEL), lambda bb, tt: (bb, 0, 0)),
            pl.BlockSpec((1, TM, D_MODEL), lambda bb, tt: (bb, jnp.maximum(tt - 1, 0), 0))]


def _even_in_kernel(ctx_ref, x_ref, mod_ref, g_ref, w_ref, wvt_ref, cos_ref, sa_ref, sb_ref,
                    rq_ref, rk_ref, rv_ref, rg_ref, nq_ref, nk_ref, nvt_ref):
    m = mod_ref[0, 0]
    h = _norm_mod(_stream_tile(ctx_ref, x_ref), g_ref[...], m[1:2], m[0:1]).astype(BF16)
    cos, sa, sb = cos_ref[...], sa_ref[...], sb_ref[...]

    def proj(j):
        return _mm(h, w_ref[:, j * RET_W:(j + 1) * RET_W])

    def rope_store(o_ref, r, scale):
        for hh in range(RET_HEADS):
            t = r[:, hh * RET_DIM:(hh + 1) * RET_DIM]
            o = t * cos + pltpu.roll(t, RET_DIM - 32, 1) * sa + pltpu.roll(t, 32, 1) * sb
            o_ref[0, :, hh * RET_DIM:(hh + 1) * RET_DIM] = (o * scale).astype(BF16)

    rope_store(rq_ref, proj(0), 1.0)
    rope_store(rk_ref, proj(1), RET_DIM ** -0.5)
    rv_ref[0] = proj(2).astype(BF16)
    rg_ref[0] = proj(3).astype(BF16)
    nq_ref[0] = (proj(4) * (NA_DIM ** -0.5 * LOG2E)).astype(BF16)
    nk_ref[0] = proj(5).astype(BF16)
    nvt_ref[0] = _nt(wvt_ref[...], h).astype(BF16)


def _even_in(ctx, x, modsel, g, w, w_vt, tables):
    b = x.shape[0]
    t = ctx.shape[1] + x.shape[1]
    tab = pl.BlockSpec((TM, RET_DIM), lambda bb, tt: (tt, 0))
    out = pl.BlockSpec((1, TM, RET_W), lambda bb, tt: (bb, tt, 0))
    return pl.pallas_call(
        _even_in_kernel,
        grid=(b, t // TM),
        in_specs=_stream_specs() + [_mod_spec(), _resident((1, D_MODEL)), _resident(w.shape),
                                    _resident(w_vt.shape), tab, tab, tab],
        out_specs=[out] * 6 + [pl.BlockSpec((1, NA_W, TM), lambda bb, tt: (bb, 0, tt))],
        out_shape=[jax.ShapeDtypeStruct((b, t, RET_W), BF16)] * 6 + [jax.ShapeDtypeStruct((b, NA_W, t), BF16)],
        compiler_params=_params(2),
        name="even_in",
    )(ctx, x, modsel, g.reshape(1, D_MODEL), w, w_vt, *tables)


def _ret_tables(decay_logit):
    log_g = jax.nn.log_sigmoid(decay_logit.astype(F32))
    idx = jnp.arange(CHUNK, dtype=F32)
    diff = idx[:, None] - idx[None, :]
    ones = jnp.ones((CHUNK, CHUNK), F32)

    def one(lg, d):
        dd = diff if d == 0 else -diff
        dm = jnp.where(dd >= 0, jnp.exp(jnp.maximum(dd, 0.0) * lg), 0.0)
        qe = (idx + 1.0) if d == 0 else (CHUNK - idx)
        ke = (CHUNK - 1.0 - idx) if d == 0 else idx
        return jnp.stack([dm, jnp.exp(qe * lg)[:, None] * ones, jnp.exp(ke * lg)[:, None] * ones,
                          jnp.exp(CHUNK * lg) * ones])

    return jnp.stack([jnp.stack([one(log_g[d, h], d) for h in range(RET_HEADS)]) for d in range(2)])


def _ret_kernel(qf_ref, kf_ref, vf_ref, qb_ref, kb_ref, vb_ref, dec_ref, yf_ref, yb_ref, s_ref):
    @pl.when(pl.program_id(1) == 0)
    def _():
        s_ref[...] = jnp.zeros_like(s_ref)

    refs = ((qf_ref, kf_ref, vf_ref, yf_ref), (qb_ref, kb_ref, vb_ref, yb_ref))
    chains = [(d, hh, slice(hh * RET_DIM, (hh + 1) * RET_DIM)) for d in range(2) for hh in range(RET_HEADS)]
    stage1 = []
    for d, hh, sl in chains:
        q_ref, k_ref, _, _ = refs[d]
        q = q_ref[0, :, sl]
        stage1.append((_nt(q, k_ref[0, :, sl]), _mm(q, s_ref[d, hh].astype(BF16))))
    for (d, hh, sl), (att, inter) in zip(chains, stage1):
        _, _, v_ref, y_ref = refs[d]
        y_ref[0, :, sl] = (_mm((att * dec_ref[d, hh, 0]).astype(BF16), v_ref[0, :, sl])
                           + inter * dec_ref[d, hh, 1])
    for d, hh, sl in chains:
        _, k_ref, v_ref, _ = refs[d]
        kk = (k_ref[0, :, sl].astype(F32) * dec_ref[d, hh, 2]).astype(BF16)
        s_ref[d, hh] = dec_ref[d, hh, 3] * s_ref[d, hh] + _tn(kk, v_ref[0, :, sl])


def _scan_orders(n_steps, n_ctx):
    fwd = lambda i: i
    bwd = lambda i: jnp.where(i < n_ctx, n_ctx - 1 - i, n_steps + n_ctx - 1 - i)
    return fwd, bwd


def _retention(rq, rk, rv, dec, ctx_len):
    b, t, _ = rq.shape
    ns = t // CHUNK
    fwd, bwd = _scan_orders(ns, ctx_len // CHUNK)

    def spec(order):
        return pl.BlockSpec((1, CHUNK, RET_W), lambda bb, i: (bb, order(i), 0))

    return pl.pallas_call(
        _ret_kernel,
        grid=(b, ns),
        in_specs=[spec(fwd)] * 3 + [spec(bwd)] * 3 + [_resident(dec.shape)],
        out_specs=[spec(fwd), spec(bwd)],
        out_shape=[jax.ShapeDtypeStruct((b, t, RET_W), F32)] * 2,
        scratch_shapes=[pltpu.VMEM((2, RET_HEADS, RET_DIM, RET_DIM), F32)],
        compiler_params=_params(2),
        name="retention",
    )(rq, rk, rv, rq, rk, rv, dec)


def _na_bias_tables(rpb):
    qc = np.arange(GRID_W)[:, None]
    kc = np.arange(GRID_W)[None, :]
    cstart = np.clip(qc - NA_WIN_C // 2, 0, GRID_W - NA_WIN_C)
    inwin = (kc >= cstart) & (kc < cstart + NA_WIN_C)
    dcol = np.clip(kc - qc, -(NA_WIN_C - 1), NA_WIN_C - 1) + (NA_WIN_C - 1)
    onehot = np.zeros((GRID_W, GRID_W, 2 * NA_WIN_C - 1), np.float32)
    onehot[qc, kc, dcol] = 1.0
    t = jnp.einsum("hrd,qkd->hrkq", rpb.astype(F32), jnp.asarray(onehot), precision=lax.Precision.HIGHEST)
    t = jnp.where(jnp.asarray(inwin.T), t * LOG2E, NEG)
    pad = jnp.full((NA_HEADS, 2, GRID_W, GRID_W), NEG, F32)
    tp = jnp.concatenate([pad, t, pad], axis=1)
    n = 2 * NA_WIN_R + 1
    top = jnp.concatenate([tp[:, 1:1 + n], tp[:, 0:n]], axis=-1)
    bot = jnp.concatenate([tp[:, 2:2 + n], tp[:, 1:1 + n]], axis=-1)
    return jnp.concatenate([top, bot], axis=-2)


def _na_kernel(q_ref, k_ref, vt_ref, tbl_ref, o_ref, *, rows, ctx_len):
    s = pl.program_id(1)
    n_ctx = ctx_len // CHUNK
    left = lax.broadcasted_iota(jnp.int32, (1, LANES), 1) < NA_DIM
    top_half = lax.broadcasted_iota(jnp.int32, (CHUNK, LANES), 0) < GRID_W
    left_half = lax.broadcasted_iota(jnp.int32, (CHUNK, LANES), 1) < GRID_W
    q = q_ref[0]
    n_loc = NA_KROWS * GRID_W
    n_blk = NA_KROWS // 2

    def attend(local):
        pairs = [slice(hp * LANES, (hp + 1) * LANES) for hp in range(NA_HEADS // 2)]
        scores = []
        for hp, sl in enumerate(pairs):
            qp = q[:, sl]
            zero = jnp.zeros_like(qp)
            qboth = jnp.concatenate([jnp.where(left, qp, zero), jnp.where(left, zero, qp)], axis=0)
            s_ctx = _nt(k_ref[0, 0:ctx_len, sl], qboth)
            s_loc = None
            if local is not None:
                off, idx, pen = local
                s_loc = _nt(k_ref[0, pl.ds(off, n_loc), sl], qboth)
                s_loc = s_loc + jnp.concatenate(
                    [jnp.concatenate([tbl_ref[2 * hp, idx[m]] + pen[m], tbl_ref[2 * hp + 1, idx[m]] + pen[m]],
                                     axis=1) for m in range(n_blk)], axis=0)
            scores.append((s_ctx, s_loc))
        probs = []
        for s_ctx, s_loc in scores:
            mx = jnp.max(s_ctx, axis=0, keepdims=True)
            if s_loc is not None:
                mx = jnp.maximum(mx, jnp.max(s_loc, axis=0, keepdims=True))
            p_ctx = jnp.exp2(s_ctx - mx)
            den = jnp.sum(p_ctx, axis=0, keepdims=True)
            p_loc = None
            if s_loc is not None:
                p_loc = jnp.exp2(s_loc - mx)
                den = den + jnp.sum(p_loc, axis=0, keepdims=True)
                p_loc = p_loc.astype(BF16)
            probs.append((p_ctx.astype(BF16), p_loc, den))
        for sl, (p_ctx, p_loc, den) in zip(pairs, probs):
            ot = _mm(vt_ref[0, sl, 0:ctx_len], p_ctx)
            if p_loc is not None:
                ot = ot + _mm(vt_ref[0, sl, pl.ds(local[0], n_loc)], p_loc)
            ot = ot / den
            pair_t = jnp.concatenate([ot[0:NA_DIM, 0:CHUNK], ot[NA_DIM:, CHUNK:]], axis=0)
            o_ref[0, :, sl] = pair_t.T.astype(BF16)

    @pl.when(s < n_ctx)
    def _():
        attend(None)

    @pl.when(s >= n_ctx)
    def _():
        r0 = 2 * (s - n_ctx)
        kb = jnp.clip(r0 - NA_WIN_R // 2, 0, rows - NA_KROWS)
        off = pl.multiple_of(ctx_len + kb * GRID_W, LANES)

        def row_pen(kr, r):
            rs = jnp.clip(r - NA_WIN_R // 2, 0, rows - NA_WIN_R)
            return jnp.where(jnp.logical_and(kr >= rs, kr < rs + NA_WIN_R), 0.0, NEG)

        idx, pen = [], []
        for m in range(n_blk):
            kp = kb + 2 * m
            idx.append(kp - r0 + NA_WIN_R)
            pen.append(jnp.where(top_half,
                                 jnp.where(left_half, row_pen(kp, r0), row_pen(kp, r0 + 1)),
                                 jnp.where(left_half, row_pen(kp + 1, r0), row_pen(kp + 1, r0 + 1))))
        attend((off, idx, pen))


def _na(nq, nk, nvt, tbl, ctx_len):
    b, t, _ = nq.shape
    rows = (t - ctx_len) // GRID_W
    assert rows >= NA_KROWS and rows % 2 == 0 and CHUNK == 2 * GRID_W == LANES
    blk = pl.BlockSpec((1, CHUNK, NA_W), lambda bb, s: (bb, s, 0))
    return pl.pallas_call(
        functools.partial(_na_kernel, rows=rows, ctx_len=ctx_len),
        grid=(b, t // CHUNK),
        in_specs=[blk, pl.BlockSpec((1, t, NA_W), lambda bb, s: (bb, 0, 0)),
                  pl.BlockSpec((1, NA_W, t), lambda bb, s: (bb, 0, 0)), _resident(tbl.shape)],
        out_specs=blk,
        out_shape=jax.ShapeDtypeStruct((b, t, NA_W), BF16),
        compiler_params=_params(2),
        name="na",
    )(nq, nk, nvt, tbl)


def _mlp_tail(x1, m, nmlp, w1_ref, w2_ref):
    h2 = _norm_mod(x1, nmlp, m[4:5], m[3:4]).astype(BF16)
    fc = D_MODEL
    acc = jnp.zeros_like(x1)
    for j in range(D_FF // fc):
        u = _mm(h2, w1_ref[:, j * fc:(j + 1) * fc])
        u = jnp.square(jnp.maximum(u, 0.0)).astype(BF16)
        acc = acc + _mm(u, w2_ref[j * fc:(j + 1) * fc, :])
    return x1 + m[5:6] * acc


def _even_out_kernel(ctx_ref, x_ref, mod_ref, yf_ref, yb_ref, rg_ref, yna_ref, wo_ref, nmlp_ref, w1_ref, w2_ref,
                     o_ref):
    m = mod_ref[0, 0]
    y = yf_ref[0] + yb_ref[0]
    g = rg_ref[0].astype(F32)
    mix = _mm(yna_ref[0], wo_ref[RET_W:RET_W + NA_W, :])
    for hh in range(RET_HEADS):
        sl = slice(hh * RET_DIM, (hh + 1) * RET_DIM)
        yh = y[:, sl]
        dlt = yh - jnp.mean(yh, axis=-1, keepdims=True)
        yn = dlt * lax.rsqrt(jnp.mean(dlt * dlt, axis=-1, keepdims=True) + EPS)
        mix = mix + _mm((yn * _silu(g[:, sl])).astype(BF16), wo_ref[sl, :])
    x1 = _stream_tile(ctx_ref, x_ref) + m[2:3] * mix
    o_ref[0] = _mlp_tail(x1, m, nmlp_ref[...], w1_ref, w2_ref)


def _even_out(ctx, x, modsel, yf, yb, rg, yna, wo, nmlp, w1, w2):
    b = x.shape[0]
    t = ctx.shape[1] + x.shape[1]
    tok = lambda w: pl.BlockSpec((1, TM, w), lambda bb, tt: (bb, tt, 0))
    return pl.pallas_call(
        _even_out_kernel,
        grid=(b, t // TM),
        in_specs=_stream_specs() + [_mod_spec(), tok(RET_W), tok(RET_W), tok(RET_W), tok(NA_W),
                                    _resident(wo.shape), _resident((1, D_MODEL)), _resident(w1.shape),
                                    _resident(w2.shape)],
        out_specs=tok(D_MODEL),
        out_shape=jax.ShapeDtypeStruct((b, t, D_MODEL), F32),
        compiler_params=_params(2),
        name="even_out_mlp",
    )(ctx, x, modsel, yf, yb, rg, yna, wo, nmlp.reshape(1, D_MODEL), w1, w2)


def _conv_perm():
    span = TM // SUBLANES
    rho = np.arange(TM)
    p = np.zeros((TM, TM), np.float32)
    p[rho, (rho % SUBLANES) * span + rho // SUBLANES] = 1.0
    return jnp.asarray(p, BF16)


def _odd_in_kernel(x_ref, mod_ref, g_ref, w_ref, wdt_ref, dtb_ref, perm_ref, z_ref, xbc_ref, dt_ref):
    m = mod_ref[0, 0]
    h = _norm_mod(x_ref[0], g_ref[...], m[1:2], m[0:1]).astype(BF16)
    cw = 512
    for j in range(SSM_INNER // cw):
        z_ref[0, :, j * cw:(j + 1) * cw] = _mm(h, w_ref[:, j * cw:(j + 1) * cw]).astype(BF16)
    hp = _mm(perm_ref[...], h).astype(BF16)
    for j in range(SSM_CONV_CH // cw):
        c0 = SSM_INNER + j * cw
        xbc_ref[0, :, j * cw:(j + 1) * cw] = _mm(hp, w_ref[:, c0:c0 + cw]).astype(BF16)
    raw = _mm(h, wdt_ref[...]) + dtb_ref[...]
    dt_ref[0] = jnp.maximum(raw, 0.0) + jnp.log1p(jnp.exp(-jnp.abs(raw)))


def _odd_in(xs, modsel, g, w_main, w_dt, dt_bias):
    b, t, _ = xs.shape
    tok = lambda w: pl.BlockSpec((1, TM, w), lambda bb, tt: (bb, tt, 0))
    return pl.pallas_call(
        _odd_in_kernel,
        grid=(b, t // TM),
        in_specs=[tok(D_MODEL), _mod_spec(), _resident((1, D_MODEL)), _resident(w_main.shape),
                  _resident(w_dt.shape), _resident((1, LANES)), _resident((TM, TM))],
        out_specs=[tok(SSM_INNER), tok(SSM_CONV_CH), tok(LANES)],
        out_shape=[jax.ShapeDtypeStruct((b, t, SSM_INNER), BF16),
                   jax.ShapeDtypeStruct((b, t, SSM_CONV_CH), BF16),
                   jax.ShapeDtypeStruct((b, t, LANES), F32)],
        compiler_params=_params(2),
        name="odd_in",
    )(xs, modsel, g.reshape(1, D_MODEL), w_main, w_dt, dt_bias, _conv_perm())


def _conv_kernel(main_ref, prev_ref, next_ref, w_ref, b_ref, unperm_ref, o_ref, act_ref, *, n_tiles):
    t = pl.program_id(1)
    has_prev = (t >= 2).astype(F32)
    has_next = jnp.logical_and(t >= 1, t < n_tiles - 1).astype(F32)
    span = TM // SUBLANES
    half = SSM_CONV // 2
    sub = lax.broadcasted_iota(jnp.int32, (SUBLANES, LANES), 0)

    def lane_tile(j, carry):
        lanes = pl.ds(pl.multiple_of(j * LANES, LANES), LANES)
        xm = main_ref[0, :, lanes].astype(F32)
        pv = prev_ref[0, :, lanes].astype(F32) * has_prev
        nx = next_ref[0, :, lanes].astype(F32) * has_next
        reg = {m: xm[m * SUBLANES:(m + 1) * SUBLANES] for m in range(span)}
        for i in range(1, half + 1):
            row = HALO - 1 - (i - 1) * SUBLANES
            edge = jnp.broadcast_to(pv[row:row + 1], (SUBLANES, LANES))
            reg[-i] = jnp.where(sub == 0, edge, pltpu.roll(reg[span - i], 1, 0))
            row = (i - 1) * SUBLANES
            edge = jnp.broadcast_to(nx[row:row + 1], (SUBLANES, LANES))
            reg[span + i - 1] = jnp.where(sub == SUBLANES - 1, edge, pltpu.roll(reg[i - 1], SUBLANES - 1, 0))
        wk = [jnp.broadcast_to(w_ref[k:k + 1, lanes], (SUBLANES, LANES)) for k in range(SSM_CONV)]
        bias = jnp.broadcast_to(b_ref[:, lanes], (SUBLANES, LANES))
        acts = []
        for m in range(span):
            acc = bias
            for k in range(SSM_CONV):
                acc = acc + wk[k] * reg[m + k - half]
            acts.append(_silu(acc))
        act_ref[:, lanes] = jnp.concatenate(acts, axis=0).astype(BF16)
        return carry

    lax.fori_loop(0, SSM_CONV_CH // LANES, lane_tile, 0, unroll=2)
    o_ref[0] = _mm(unperm_ref[...], act_ref[...]).astype(BF16)


def _conv(xbc, w, bias, ctx_len):
    b, t, ch = xbc.shape
    assert ctx_len == TM and TM % HALO == 0 and HALO >= (SSM_CONV // 2) * SUBLANES
    nt = t // TM
    per = TM // HALO
    main = pl.BlockSpec((1, TM, ch), lambda bb, tt: (bb, tt, 0))
    prev = pl.BlockSpec((1, HALO, ch), lambda bb, tt: (bb, jnp.maximum(tt * per - 1, 0), 0))
    nxt = pl.BlockSpec((1, HALO, ch), lambda bb, tt: (bb, jnp.minimum((tt + 1) * per, nt * per - 1), 0))
    return pl.pallas_call(
        functools.partial(_conv_kernel, n_tiles=nt),
        grid=(b, nt),
        in_specs=[main, prev, nxt, _resident(w.shape), _resident((1, ch)), _resident((TM, TM))],
        out_specs=main,
        out_shape=jax.ShapeDtypeStruct((b, t, ch), BF16),
        scratch_shapes=[pltpu.VMEM((TM, ch), BF16)],
        compiler_params=_params(2),
        name="dwconv_silu",
    )(xbc, xbc, xbc, w, bias.reshape(1, ch), _conv_perm().T)


def _dot_exact_lhs(a16, x):
    hi = x.astype(BF16)
    r1 = x - hi.astype(F32)
    mid = r1.astype(BF16)
    lo = (r1 - mid.astype(F32)).astype(BF16)
    return _mm(a16, hi) + _mm(a16, mid) + _mm(a16, lo)


def _ssd_kernel(xf_ref, xb_ref, dtf_ref, dtb_ref, a_ref, tri_ref, yf_ref, yb_ref, h_ref, row_ref):
    @pl.when(pl.program_id(1) == 0)
    def _():
        h_ref[...] = jnp.zeros_like(h_ref)

    left = lax.broadcasted_iota(jnp.int32, (1, LANES), 1) < SSM_HEAD_DIM
    keep_l = jnp.where(left, jnp.uint32(0xFFFFFFFF), jnp.uint32(0))
    keep_r = jnp.where(left, jnp.uint32(0), jnp.uint32(0xFFFFFFFF))
    ii = lax.broadcasted_iota(jnp.int32, (CHUNK, CHUNK), 0)
    jj = lax.broadcasted_iota(jnp.int32, (CHUNK, CHUNK), 1)
    b_off = SSM_INNER
    c_off = SSM_INNER + SSM_GN

    for d, (x_ref, dt_ref, y_ref) in enumerate(((xf_ref, dtf_ref, yf_ref), (xb_ref, dtb_ref, yb_ref))):
        dtc = dt_ref[0]
        acum = _dot_exact_lhs(tri_ref[d], dtc * a_ref[...]) * LOG2E
        acum_t = acum.T
        dt_t = dtc.T
        mask = (ii >= jj) if d == 0 else (jj >= ii)
        last = acum[CHUNK - 1:CHUNK, :] if d == 0 else acum[0:1, :]
        last_t = acum_t[:, CHUNK - 1:CHUNK] if d == 0 else acum_t[:, 0:1]
        elast = jnp.exp2(last)
        row_ref[d, 0] = acum_t
        row_ref[d, 1] = dt_t
        row_ref[d, 2] = dt_t * jnp.exp2(last_t - acum_t)

        for g in range(SSM_GROUPS):
            bg = x_ref[0, :, b_off + g * SSM_STATE:b_off + (g + 1) * SSM_STATE]
            cg = x_ref[0, :, c_off + g * SSM_STATE:c_off + (g + 1) * SSM_STATE]
            cb = _nt(cg, bg)
            bg_t = bg.astype(F32).T
            h_t = h_ref[d, g]
            y_int = _mm(cg, h_t.astype(BF16))
            for hp in range(SSM_HPG // 2):
                c0 = d * SSM_HEADS + g * SSM_HPG + 2 * hp
                ch = slice(g * SSM_GW + hp * LANES, g * SSM_GW + (hp + 1) * LANES)
                hl = slice(hp * LANES, (hp + 1) * LANES)
                xp = x_ref[0, :, ch]
                xu = pltpu.bitcast(xp, jnp.uint32)
                xbd = jnp.concatenate([pltpu.bitcast(xu & keep_l, BF16), pltpu.bitcast(xu & keep_r, BF16)], axis=0)
                intra, upd, eac = [], [], []
                for e in range(2):
                    c = c0 + e
                    acol = jnp.broadcast_to(acum[:, c:c + 1], (CHUNK, CHUNK))
                    lmat = jnp.exp2(jnp.where(mask, acol - row_ref[d, 0, c:c + 1, :], NEG))
                    intra.append((lmat * cb * row_ref[d, 1, c:c + 1, :]).astype(BF16))
                    upd.append((bg_t * row_ref[d, 2, c:c + 1, :]).astype(BF16))
                    eac.append(jnp.exp2(acol))
                lhs = jnp.concatenate([jnp.concatenate(intra, axis=1), jnp.concatenate(upd, axis=1)], axis=0)
                r = _mm(lhs, xbd)
                y_ref[0, :, ch] = r[:CHUNK] + y_int[:, hl] * jnp.where(left, eac[0], eac[1])
                el = jnp.where(left, jnp.broadcast_to(elast[:, c0:c0 + 1], (1, LANES)),
                               jnp.broadcast_to(elast[:, c0 + 1:c0 + 2], (1, LANES)))
                h_ref[d, g, :, hl] = el * h_t[:, hl] + r[CHUNK:]


def _ssd(xc, dt, a_row, ctx_len):
    b, t, ch = xc.shape
    ns = t // CHUNK
    fwd, bwd = _scan_orders(ns, ctx_len // CHUNK)
    idx = np.arange(CHUNK)
    tri = jnp.asarray(np.stack([idx[:, None] >= idx[None, :], idx[:, None] <= idx[None, :]]), BF16)

    def spec(order, w):
        return pl.BlockSpec((1, CHUNK, w), lambda bb, i: (bb, order(i), 0))

    return pl.pallas_call(
        _ssd_kernel,
        grid=(b, ns),
        in_specs=[spec(fwd, ch), spec(bwd, ch), spec(fwd, LANES), spec(bwd, LANES),
                  _resident((1, LANES)), _resident((2, CHUNK, CHUNK))],
        out_specs=[spec(fwd, SSM_INNER), spec(bwd, SSM_INNER)],
        out_shape=[jax.ShapeDtypeStruct((b, t, SSM_INNER), F32)] * 2,
        scratch_shapes=[pltpu.VMEM((2, SSM_GROUPS, SSM_STATE, SSM_GW), F32),
                        pltpu.VMEM((2, 3, CHUNK, CHUNK), F32)],
        compiler_params=_params(2),
        name="ssd",
    )(xc, xc, dt, dt, a_row, tri)


def _odd_out_kernel(x_ref, mod_ref, yf_ref, yb_ref, xc_ref, z_ref, dsk_ref, nssm_ref, wo_ref,
                    nmlp_ref, w1_ref, w2_ref, nfin_ref, o_ref):
    m = mod_ref[0, 0]
    y = yf_ref[0] + yb_ref[0] + dsk_ref[...] * xc_ref[0].astype(F32)
    y = y * _silu(z_ref[0].astype(F32))
    y = y * lax.rsqrt(jnp.mean(y * y, axis=-1, keepdims=True) + EPS) * nssm_ref[...]
    x1 = x_ref[0] + m[2:3] * _mm(y.astype(BF16), wo_ref[...])
    x2 = _mlp_tail(x1, m, nmlp_ref[...], w1_ref, w2_ref)
    o_ref[0] = x2 * lax.rsqrt(jnp.mean(x2 * x2, axis=-1, keepdims=True) + EPS) * nfin_ref[...]


def _odd_out(xs, modsel, yf, yb, xc, z, dsk, nssm, wo, nmlp, w1, w2, nfin, ctx_len):
    b, t, _ = xs.shape
    skip = ctx_len // TM
    tok = lambda w: pl.BlockSpec((1, TM, w), lambda bb, tt: (bb, tt + skip, 0))
    mod = pl.BlockSpec((1, 1, 6, D_MODEL), lambda bb, tt: (bb, 1, 0, 0))
    vec = lambda w: _resident((1, w))
    return pl.pallas_call(
        _odd_out_kernel,
        grid=(b, t // TM - skip),
        in_specs=[tok(D_MODEL), mod, tok(SSM_INNER), tok(SSM_INNER), tok(SSM_INNER), tok(SSM_INNER),
                  vec(SSM_INNER), vec(SSM_INNER), _resident(wo.shape), vec(D_MODEL),
                  _resident(w1.shape), _resident(w2.shape), vec(D_MODEL)],
        out_specs=pl.BlockSpec((1, TM, D_MODEL), lambda bb, tt: (bb, tt, 0)),
        out_shape=jax.ShapeDtypeStruct((b, t - ctx_len, D_MODEL), F32),
        compiler_params=_params(2),
        name="odd_out_mlp",
    )(xs, modsel, yf, yb, xc, z, dsk.reshape(1, -1), nssm.reshape(1, -1), wo, nmlp.reshape(1, -1),
      w1, w2, nfin.reshape(1, -1))


def kernel(x, c, ctx, c_ctx, w_mod, b_mod, norm_mix, norm_mlp, w_mlp_in, w_mlp_out, w_in_even, w_out_even,
           ret_decay_logit, na_rpb, w_in_odd, conv_w, conv_b, dt_bias, a_log, d_skip, ssm_norm, w_out_odd,
           norm_final):
    assert w_mod.shape[0] == DEPTH == 2 and x.shape[2] == D_MODEL
    ctx_len = ctx.shape[1]
    seq = x.shape[1]
    assert ctx_len == TM and seq % (2 * GRID_W) == 0

    modsel = _modulation(c, c_ctx, w_mod, b_mod)
    w1 = w_mlp_in.astype(BF16)
    w2 = w_mlp_out.astype(BF16)

    n_tok = 4 * RET_W + 2 * NA_W
    w_even = w_in_even[0].astype(BF16)
    rq, rk, rv, rg, nq, nk, nvt = _even_in(ctx, x, modsel[0], norm_mix[0], w_even[:, :n_tok], w_even[:, n_tok:].T,
                                           _rope_tables(seq, ctx_len))
    yf, yb = _retention(rq, rk, rv, _ret_tables(ret_decay_logit[0]), ctx_len)
    yna = _na(nq, nk, nvt, _na_bias_tables(na_rpb[0]), ctx_len)
    xs = _even_out(ctx, x, modsel[0], yf, yb, rg, yna, w_out_even[0].astype(BF16), norm_mlp[0], w1[0], w2[0])

    n_main = SSM_INNER + SSM_CONV_CH
    wi = w_in_odd[0]
    w_dt = jnp.pad(wi[:, n_main:], ((0, 0), (0, LANES - 2 * SSM_HEADS))).astype(BF16)
    pad_row = lambda v: jnp.pad(v.astype(F32).reshape(1, -1), ((0, 0), (0, LANES - 2 * SSM_HEADS)))
    z, xbc, dt = _odd_in(xs, modsel[1], norm_mix[1], wi[:, :n_main].astype(BF16), w_dt, pad_row(dt_bias[0]))
    xc = _conv(xbc, conv_w[0].astype(F32), conv_b[0].astype(F32), ctx_len)
    yf, yb = _ssd(xc, dt, pad_row(-jnp.exp(a_log[0].astype(F32))), ctx_len)
    dsk = jnp.repeat(d_skip[0].astype(F32), SSM_HEAD_DIM)
    return _odd_out(xs, modsel[1], yf, yb, xc, z, dsk, ssm_norm[0], w_out_odd[0].astype(BF16), norm_mlp[1],
                    w1[1], w2[1], norm_final, ctx_len)
```

```python
import functools

import numpy as np
import jax
import jax.numpy as jnp
from jax import lax
from jax.experimental import pallas as pl
from jax.experimental.pallas import tpu as pltpu

F32 = jnp.float32
BF16 = jnp.bfloat16

D_MODEL = 1024
D_FF = 4 * D_MODEL
DEPTH = 2
GRID_W = 64
EPS = 1e-6
ROPE_BASE = 10000.0

RET_HEADS = 4
RET_DIM = 128
RET_W = RET_HEADS * RET_DIM

NA_HEADS = 8
NA_DIM = 64
NA_W = NA_HEADS * NA_DIM
NA_WIN_R = 8
NA_WIN_C = 16

SSM_INNER = 2 * D_MODEL
SSM_HEAD_DIM = 64
SSM_HEADS = SSM_INNER // SSM_HEAD_DIM
SSM_GROUPS = 4
SSM_HPG = SSM_HEADS // SSM_GROUPS
SSM_STATE = 128
SSM_CONV = 7
SSM_GN = SSM_GROUPS * SSM_STATE
SSM_CONV_CH = SSM_INNER + 2 * SSM_GN
SSM_GW = SSM_HPG * SSM_HEAD_DIM

LANES = 128
SUBLANES = 8
TM = 256
CHUNK = 128
FIN_ROWS = 32
HALO = 32
LOG2E = 1.4426950408889634
NEG = -1e30
NA_KROWS = 10
VMEM_LIMIT = 56 * 1024 * 1024


def _params(n_axes):
    return pltpu.CompilerParams(dimension_semantics=("arbitrary",) * n_axes, vmem_limit_bytes=VMEM_LIMIT)


def _resident(shape):
    nd = len(shape)
    return pl.BlockSpec(shape, lambda *_: (0,) * nd, pipeline_mode=pl.Buffered(1))


def _resident_layer(shape, layer):
    nd = len(shape)
    return pl.BlockSpec((None,) + tuple(shape[1:]), lambda *_: (layer,) + (0,) * (nd - 1),
                        pipeline_mode=pl.Buffered(1))


def _nt(a, b):
    return lax.dot_general(a, b, (((1,), (1,)), ((), ())), preferred_element_type=F32)


def _tn(a, b):
    return lax.dot_general(a, b, (((0,), (0,)), ((), ())), preferred_element_type=F32)


def _mm(a, b):
    return jnp.dot(a, b, preferred_element_type=F32)


def _silu(v):
    return v * jax.nn.sigmoid(v)


def _norm_mod(x, g, sc, sh):
    ms = jnp.mean(x * x, axis=-1, keepdims=True)
    return (x * lax.rsqrt(ms + EPS) * g) * (1.0 + sc) + sh


def _mod_kernel(cc_ref, w_ref, b_ref, o_ref):
    s = _silu(cc_ref[...])
    o_ref[0] = _mm(s.astype(BF16), w_ref[0].astype(BF16)) + b_ref[0]


def _modulation(c, c_ctx, w_mod, b_mod):
    b = c.shape[0]
    rows = -(-(b + 1) // 8) * 8
    cc = jnp.zeros((rows, D_MODEL), F32).at[:b].set(c).at[b].set(c_ctx)
    tn = 1536
    out = pl.pallas_call(
        _mod_kernel,
        grid=(DEPTH, 6 * D_MODEL // tn),
        in_specs=[pl.BlockSpec((rows, D_MODEL), lambda l, j: (0, 0)),
                  pl.BlockSpec((1, D_MODEL, tn), lambda l, j: (l, 0, j)),
                  pl.BlockSpec((1, 1, tn), lambda l, j: (l, 0, j))],
        out_specs=pl.BlockSpec((1, rows, tn), lambda l, j: (l, 0, j)),
        out_shape=jax.ShapeDtypeStruct((DEPTH, rows, 6 * D_MODEL), F32),
        compiler_params=_params(2),
        name="modulation",
    )(cc, w_mod, b_mod.reshape(DEPTH, 1, 6 * D_MODEL))
    mod = out.reshape(DEPTH, rows, 6, D_MODEL)
    return jnp.stack([jnp.broadcast_to(mod[:, b:b + 1], (DEPTH, b, 6, D_MODEL)), mod[:, :b]], axis=2)


def _mod_spec():
    return pl.BlockSpec((1, 1, 6, D_MODEL), lambda b, t: (b, jnp.minimum(t, 1), 0, 0))


def _rope_tables(seq, ctx_len):
    pos = jnp.arange(seq)
    nf = RET_DIM // 4
    inv = ROPE_BASE ** (-jnp.arange(nf, dtype=F32) / nf)

    def cs(p):
        ang = p.astype(F32)[:, None] * inv[None, :]
        return jnp.cos(ang), jnp.sin(ang)

    cr, sr = cs(pos // GRID_W)
    cc, sc = cs(pos % GRID_W)
    zero = jnp.zeros_like(sr)
    cos = jnp.concatenate([cr, cr, cc, cc], axis=1)
    sin_a = jnp.concatenate([-sr, zero, -sc, zero], axis=1)
    sin_b = jnp.concatenate([zero, sr, zero, sc], axis=1)
    ident = jnp.ones((ctx_len, RET_DIM), F32)
    nul = jnp.zeros((ctx_len, RET_DIM), F32)
    return (jnp.concatenate([ident, cos], axis=0), jnp.concatenate([nul, sin_a], axis=0),
            jnp.concatenate([nul, sin_b], axis=0))


def _stream_tile(ctx_ref, x_ref):
    return jnp.where(pl.program_id(1) == 0, ctx_ref[0], x_ref[0])


def _stream_specs():
    return [pl.BlockSpec((1, TM, D_MODEL), lambda bb, tt: (bb, 0, 0)),
            pl.BlockSpec((1, TM, D_MODEL), lambda bb, tt: (bb, jnp.maximum(tt - 1, 0), 0))]


def _even_in_kernel(ctx_ref, x_ref, mod_ref, g_ref, w_ref, wvt_ref, cos_ref, sa_ref, sb_ref,
                    rq_ref, rk_ref, rv_ref, rg_ref, nq_ref, nk_ref, nvt_ref):
    m = mod_ref[0, 0]
    h = _norm_mod(_stream_tile(ctx_ref, x_ref), g_ref[...], m[1:2], m[0:1]).astype(BF16)
    cos, sa, sb = cos_ref[...], sa_ref[...], sb_ref[...]

    def proj(j):
        return _mm(h, w_ref[:, j * RET_W:(j + 1) * RET_W])

    def rope_store(o_ref, r, scale):
        for hh in range(RET_HEADS):
            t = r[:, hh * RET_DIM:(hh + 1) * RET_DIM]
            o = t * cos + pltpu.roll(t, RET_DIM - 32, 1) * sa + pltpu.roll(t, 32, 1) * sb
            o_ref[0, :, hh * RET_DIM:(hh + 1) * RET_DIM] = (o * scale).astype(BF16)

    rope_store(rq_ref, proj(0), 1.0)
    rope_store(rk_ref, proj(1), RET_DIM ** -0.5)
    rv_ref[0] = proj(2).astype(BF16)
    rg_ref[0] = _silu(proj(3)).astype(BF16)
    nq_ref[0] = (proj(4) * (NA_DIM ** -0.5 * LOG2E)).astype(BF16)
    nk_ref[0] = proj(5).astype(BF16)
    nvt_ref[0] = _nt(wvt_ref[...], h).astype(BF16)


def _even_in(ctx, x, modsel, g, w, w_vt, tables):
    b = x.shape[0]
    t = ctx.shape[1] + x.shape[1]
    tab = pl.BlockSpec((TM, RET_DIM), lambda bb, tt: (tt, 0))
    out = pl.BlockSpec((1, TM, RET_W), lambda bb, tt: (bb, tt, 0))
    return pl.pallas_call(
        _even_in_kernel,
        grid=(b, t // TM),
        in_specs=_stream_specs() + [_mod_spec(), _resident((1, D_MODEL)), _resident(w.shape),
                                    _resident(w_vt.shape), tab, tab, tab],
        out_specs=[out] * 6 + [pl.BlockSpec((1, NA_W, TM), lambda bb, tt: (bb, 0, tt))],
        out_shape=[jax.ShapeDtypeStruct((b, t, RET_W), BF16)] * 6 + [jax.ShapeDtypeStruct((b, NA_W, t), BF16)],
        compiler_params=_params(2),
        name="even_in",
    )(ctx, x, modsel, g.reshape(1, D_MODEL), w, w_vt, *tables)


def _ret_tables(decay_logit):
    log_g = jax.nn.log_sigmoid(decay_logit.astype(F32))
    idx = jnp.arange(CHUNK, dtype=F32)
    diff = idx[:, None] - idx[None, :]
    ones = jnp.ones((CHUNK, CHUNK), F32)

    def one(lg, d):
        dd = diff if d == 0 else -diff
        dm = jnp.where(dd >= 0, jnp.exp(jnp.maximum(dd, 0.0) * lg), 0.0)
        qe = (idx + 1.0) if d == 0 else (CHUNK - idx)
        ke = (CHUNK - 1.0 - idx) if d == 0 else idx
        return jnp.stack([dm, jnp.exp(qe * lg)[:, None] * ones, jnp.exp(ke * lg)[:, None] * ones,
                          jnp.exp(CHUNK * lg) * ones])

    return jnp.stack([jnp.stack([one(log_g[d, h], d) for h in range(RET_HEADS)]) for d in range(2)])


def _ret_kernel(qf_ref, kf_ref, vf_ref, qb_ref, kb_ref, vb_ref, dec_ref, yf_ref, yb_ref, s_ref):
    @pl.when(pl.program_id(1) == 0)
    def _():
        s_ref[...] = jnp.zeros_like(s_ref)

    refs = ((qf_ref, kf_ref, vf_ref, yf_ref), (qb_ref, kb_ref, vb_ref, yb_ref))
    chains = [(d, hh, slice(hh * RET_DIM, (hh + 1) * RET_DIM)) for d in range(2) for hh in range(RET_HEADS)]
    stage1 = []
    for d, hh, sl in chains:
        q_ref, k_ref, _, _ = refs[d]
        q = q_ref[0, :, sl]
        stage1.append((_nt(q, k_ref[0, :, sl]), _mm(q, s_ref[d, hh].astype(BF16))))
    for (d, hh, sl), (att, inter) in zip(chains, stage1):
        _, _, v_ref, y_ref = refs[d]
        y_ref[0, :, sl] = (_mm((att * dec_ref[d, hh, 0]).astype(BF16), v_ref[0, :, sl])
                           + inter * dec_ref[d, hh, 1])
    for d, hh, sl in chains:
        _, k_ref, v_ref, _ = refs[d]
        kk = (k_ref[0, :, sl].astype(F32) * dec_ref[d, hh, 2]).astype(BF16)
        s_ref[d, hh] = dec_ref[d, hh, 3] * s_ref[d, hh] + _tn(kk, v_ref[0, :, sl])


def _scan_orders(n_steps, n_ctx):
    fwd = lambda i: i
    bwd = lambda i: jnp.where(i < n_ctx, n_ctx - 1 - i, n_steps + n_ctx - 1 - i)
    return fwd, bwd


def _retention(rq, rk, rv, dec, ctx_len):
    b, t, _ = rq.shape
    ns = t // CHUNK
    fwd, bwd = _scan_orders(ns, ctx_len // CHUNK)

    def spec(order):
        return pl.BlockSpec((1, CHUNK, RET_W), lambda bb, i: (bb, order(i), 0))

    return pl.pallas_call(
        _ret_kernel,
        grid=(b, ns),
        in_specs=[spec(fwd)] * 3 + [spec(bwd)] * 3 + [_resident(dec.shape)],
        out_specs=[spec(fwd), spec(bwd)],
        out_shape=[jax.ShapeDtypeStruct((b, t, RET_W), F32)] * 2,
        scratch_shapes=[pltpu.VMEM((2, RET_HEADS, RET_DIM, RET_DIM), F32)],
        compiler_params=_params(2),
        name="retention",
    )(rq, rk, rv, rq, rk, rv, dec)


def _na_bias_tables(rpb):
    qc = np.arange(GRID_W)[:, None]
    kc = np.arange(GRID_W)[None, :]
    cstart = np.clip(qc - NA_WIN_C // 2, 0, GRID_W - NA_WIN_C)
    inwin = (kc >= cstart) & (kc < cstart + NA_WIN_C)
    dcol = np.clip(kc - qc, -(NA_WIN_C - 1), NA_WIN_C - 1) + (NA_WIN_C - 1)
    onehot = np.zeros((GRID_W, GRID_W, 2 * NA_WIN_C - 1), np.float32)
    onehot[qc, kc, dcol] = 1.0
    t = jnp.einsum("hrd,qkd->hrkq", rpb.astype(F32), jnp.asarray(onehot), precision=lax.Precision.HIGHEST)
    t = jnp.where(jnp.asarray(inwin.T), t * LOG2E, NEG)
    pad = jnp.full((NA_HEADS, 2, GRID_W, GRID_W), NEG, F32)
    tp = jnp.concatenate([pad, t, pad], axis=1)
    n = 2 * NA_WIN_R + 1
    top = jnp.concatenate([tp[:, 1:1 + n], tp[:, 0:n]], axis=-1)
    bot = jnp.concatenate([tp[:, 2:2 + n], tp[:, 1:1 + n]], axis=-1)
    return jnp.concatenate([top, bot], axis=-2)


def _na_kernel(q_ref, k_ref, vt_ref, tbl_ref, o_ref, *, rows, ctx_len):
    s = pl.program_id(1)
    n_ctx = ctx_len // CHUNK
    left = lax.broadcasted_iota(jnp.int32, (1, LANES), 1) < NA_DIM
    top_half = lax.broadcasted_iota(jnp.int32, (CHUNK, LANES), 0) < GRID_W
    left_half = lax.broadcasted_iota(jnp.int32, (CHUNK, LANES), 1) < GRID_W
    q = q_ref[0]
    n_loc = NA_KROWS * GRID_W
    n_blk = NA_KROWS // 2

    def attend(local):
        pairs = [slice(hp * LANES, (hp + 1) * LANES) for hp in range(NA_HEADS // 2)]
        scores = []
        for hp, sl in enumerate(pairs):
            qp = q[:, sl]
            zero = jnp.zeros_like(qp)
            qboth = jnp.concatenate([jnp.where(left, qp, zero), jnp.where(left, zero, qp)], axis=0)
            s_ctx = _nt(k_ref[0, 0:ctx_len, sl], qboth)
            s_loc = None
            if local is not None:
                off, idx, pen = local
                s_loc = _nt(k_ref[0, pl.ds(off, n_loc), sl], qboth)
                s_loc = s_loc + jnp.concatenate(
                    [jnp.concatenate([tbl_ref[2 * hp, idx[m]] + pen[m], tbl_ref[2 * hp + 1, idx[m]] + pen[m]],
                                     axis=1) for m in range(n_blk)], axis=0)
            scores.append((s_ctx, s_loc))
        probs = []
        for s_ctx, s_loc in scores:
            mx = jnp.max(s_ctx, axis=0, keepdims=True)
            if s_loc is not None:
                mx = jnp.maximum(mx, jnp.max(s_loc, axis=0, keepdims=True))
            p_ctx = jnp.exp2(s_ctx - mx)
            den = jnp.sum(p_ctx, axis=0, keepdims=True)
            p_loc = None
            if s_loc is not None:
                p_loc = jnp.exp2(s_loc - mx)
                den = den + jnp.sum(p_loc, axis=0, keepdims=True)
                p_loc = p_loc.astype(BF16)
            probs.append((p_ctx.astype(BF16), p_loc, den))
        for sl, (p_ctx, p_loc, den) in zip(pairs, probs):
            ot = _mm(vt_ref[0, sl, 0:ctx_len], p_ctx)
            if p_loc is not None:
                ot = ot + _mm(vt_ref[0, sl, pl.ds(local[0], n_loc)], p_loc)
            ot = ot / den
            pair_t = jnp.concatenate([ot[0:NA_DIM, 0:CHUNK], ot[NA_DIM:, CHUNK:]], axis=0)
            o_ref[0, :, sl] = pair_t.T.astype(BF16)

    @pl.when(s < n_ctx)
    def _():
        attend(None)

    @pl.when(s >= n_ctx)
    def _():
        r0 = 2 * (s - n_ctx)
        kb = jnp.clip(r0 - NA_WIN_R // 2, 0, rows - NA_KROWS)
        off = pl.multiple_of(ctx_len + kb * GRID_W, LANES)

        def row_pen(kr, r):
            rs = jnp.clip(r - NA_WIN_R // 2, 0, rows - NA_WIN_R)
            return jnp.where(jnp.logical_and(kr >= rs, kr < rs + NA_WIN_R), 0.0, NEG)

        idx, pen = [], []
        for m in range(n_blk):
            kp = kb + 2 * m
            idx.append(kp - r0 + NA_WIN_R)
            pen.append(jnp.where(top_half,
                                 jnp.where(left_half, row_pen(kp, r0), row_pen(kp, r0 + 1)),
                                 jnp.where(left_half, row_pen(kp + 1, r0), row_pen(kp + 1, r0 + 1))))
        attend((off, idx, pen))


def _na(nq, nk, nvt, tbl, ctx_len):
    b, t, _ = nq.shape
    rows = (t - ctx_len) // GRID_W
    assert rows >= NA_KROWS and rows % 2 == 0 and CHUNK == 2 * GRID_W == LANES
    blk = pl.BlockSpec((1, CHUNK, NA_W), lambda bb, s: (bb, s, 0))
    return pl.pallas_call(
        functools.partial(_na_kernel, rows=rows, ctx_len=ctx_len),
        grid=(b, t // CHUNK),
        in_specs=[blk, pl.BlockSpec((1, t, NA_W), lambda bb, s: (bb, 0, 0)),
                  pl.BlockSpec((1, NA_W, t), lambda bb, s: (bb, 0, 0)), _resident(tbl.shape)],
        out_specs=blk,
        out_shape=jax.ShapeDtypeStruct((b, t, NA_W), BF16),
        compiler_params=_params(2),
        name="na",
    )(nq, nk, nvt, tbl)


def _zero_token(v):
    bits = lax.bitcast_convert_type(v[0:SUBLANES, 0:LANES], jnp.int32)
    return lax.shift_right_logical(lax.shift_right_logical(bits, 16), 16)


def _mlp_tail(x1, m, nmlp, w1_ref, w2_ref, h2_ref, fillers=()):
    fillers = list(fillers)

    def fill():
        if fillers:
            token = fillers.pop(0)()
            head = pltpu.bitcast(h2_ref[0:2 * SUBLANES, 0:LANES], jnp.int32)
            h2_ref[0:2 * SUBLANES, 0:LANES] = pltpu.bitcast(head + token, BF16)

    h2_ref[...] = _norm_mod(x1, nmlp, m[4:5], m[3:4]).astype(BF16)
    fc = D_MODEL
    n_stage = D_FF // fc
    acc = jnp.zeros_like(x1)
    for j in range(n_stage):
        while len(fillers) > 2 * (n_stage - 1 - j):
            fill()
        u = _mm(h2_ref[...], w1_ref[:, j * fc:(j + 1) * fc])
        fill()
        u = jnp.square(jnp.maximum(u, 0.0)).astype(BF16)
        acc = acc + _mm(u, w2_ref[j * fc:(j + 1) * fc, :])
        fill()
    return x1 + m[5:6] * acc


def _two_phase(step_fn, slots_ref):
    i = pl.program_id(0)

    @pl.when(i == 0)
    def _():
        slots_ref[...] = jnp.zeros_like(slots_ref)

    for parity in range(2):
        @pl.when(i % 2 == parity)
        def _():
            step_fn(slots_ref.at[1 - parity], slots_ref.at[parity])


def _two_phase_maps(n_tiles, n_total):
    def mm(i):
        j = jnp.maximum(i - 1, 0)
        return j // n_tiles, j % n_tiles

    def fin(i):
        k = jnp.minimum(i, n_total - 1)
        return k // n_tiles, k % n_tiles

    return mm, fin


def _even_out_kernel(ctx_ref, x_ref, mod_ref, yna_ref, yf_ref, yb_ref, rg_ref, wo_ref, nmlp_ref, w1_ref, w2_ref,
                     o_ref, slots_ref, h2_ref, *, n_tiles):
    def step(cur_ref, nxt_ref):
        m = mod_ref[0, 0]
        is_ctx = (jnp.maximum(pl.program_id(0) - 1, 0) % n_tiles) == 0
        def finish_head(hh):
            sl = slice(hh * RET_DIM, (hh + 1) * RET_DIM)
            yh = yf_ref[0, :, sl] + yb_ref[0, :, sl]
            dlt = yh - jnp.mean(yh, axis=-1, keepdims=True)
            yn = dlt * lax.rsqrt(jnp.mean(dlt * dlt, axis=-1, keepdims=True) + EPS)
            nxt_ref[:, sl] = (yn * rg_ref[0, :, sl].astype(F32)).astype(BF16)
            return _zero_token(yn)

        mix = _mm(yna_ref[0], wo_ref[RET_W:RET_W + NA_W, :]) + _mm(cur_ref[...], wo_ref[0:RET_W, :])
        x1 = jnp.where(is_ctx, ctx_ref[0], x_ref[0]) + m[2:3] * mix
        o_ref[0] = _mlp_tail(x1, m, nmlp_ref[...], w1_ref, w2_ref, h2_ref,
                             [functools.partial(finish_head, hh) for hh in range(RET_HEADS)])

    _two_phase(step, slots_ref)


def _even_out(ctx, x, modsel, yf, yb, rg, yna, wo, nmlp, w1, w2, layer):
    b = x.shape[0]
    t = ctx.shape[1] + x.shape[1]
    nt = t // TM
    mm, fin = _two_phase_maps(nt, b * nt)

    def mm_tok(w):
        return pl.BlockSpec((1, TM, w), lambda i: (mm(i)[0], mm(i)[1], 0))

    def fin_tok(w):
        return pl.BlockSpec((1, TM, w), lambda i: (fin(i)[0], fin(i)[1], 0))

    return pl.pallas_call(
        functools.partial(_even_out_kernel, n_tiles=nt),
        grid=(b * nt + 1,),
        in_specs=[pl.BlockSpec((1, TM, D_MODEL), lambda i: (mm(i)[0], 0, 0)),
                  pl.BlockSpec((1, TM, D_MODEL), lambda i: (mm(i)[0], jnp.maximum(mm(i)[1] - 1, 0), 0)),
                  pl.BlockSpec((1, 1, 6, D_MODEL), lambda i: (mm(i)[0], jnp.minimum(mm(i)[1], 1), 0, 0)),
                  mm_tok(NA_W), fin_tok(RET_W), fin_tok(RET_W), fin_tok(RET_W),
                  _resident(wo.shape), _resident((1, D_MODEL)), _resident_layer(w1.shape, layer),
                  _resident_layer(w2.shape, layer)],
        out_specs=mm_tok(D_MODEL),
        out_shape=jax.ShapeDtypeStruct((b, t, D_MODEL), F32),
        scratch_shapes=[pltpu.VMEM((2, TM, RET_W), BF16), pltpu.VMEM((TM, D_MODEL), BF16)],
        compiler_params=_params(1),
        name="even_out_mlp",
    )(ctx, x, modsel, yna, yf, yb, rg, wo, nmlp.reshape(1, D_MODEL), w1, w2)


def _conv_perm():
    span = TM // SUBLANES
    rho = np.arange(TM)
    p = np.zeros((TM, TM), np.float32)
    p[rho, (rho % SUBLANES) * span + rho // SUBLANES] = 1.0
    return jnp.asarray(p, BF16)


def _odd_in_kernel(x_ref, mod_ref, g_ref, w_ref, wdt_ref, dtb_ref, perm_ref, z_ref, xbc_ref, dt_ref):
    m = mod_ref[0, 0]
    h = _norm_mod(x_ref[0], g_ref[...], m[1:2], m[0:1]).astype(BF16)
    cw = 512
    for j in range(SSM_INNER // cw):
        z_ref[0, :, j * cw:(j + 1) * cw] = _silu(_mm(h, w_ref[:, j * cw:(j + 1) * cw])).astype(BF16)
    hp = _mm(perm_ref[...], h).astype(BF16)
    for j in range(SSM_CONV_CH // cw):
        c0 = SSM_INNER + j * cw
        xbc_ref[0, :, j * cw:(j + 1) * cw] = _mm(hp, w_ref[:, c0:c0 + cw]).astype(BF16)
    raw = _mm(h, wdt_ref[...]) + dtb_ref[...]
    dt_ref[0] = jnp.maximum(raw, 0.0) + jnp.log1p(jnp.exp(-jnp.abs(raw)))


def _odd_in(xs, modsel, g, w_main, w_dt, dt_bias):
    b, t, _ = xs.shape
    tok = lambda w: pl.BlockSpec((1, TM, w), lambda bb, tt: (bb, tt, 0))
    return pl.pallas_call(
        _odd_in_kernel,
        grid=(b, t // TM),
        in_specs=[tok(D_MODEL), _mod_spec(), _resident((1, D_MODEL)), _resident(w_main.shape),
                  _resident(w_dt.shape), _resident((1, LANES)), _resident((TM, TM))],
        out_specs=[tok(SSM_INNER), tok(SSM_CONV_CH), tok(LANES)],
        out_shape=[jax.ShapeDtypeStruct((b, t, SSM_INNER), BF16),
                   jax.ShapeDtypeStruct((b, t, SSM_CONV_CH), BF16),
                   jax.ShapeDtypeStruct((b, t, LANES), F32)],
        compiler_params=_params(2),
        name="odd_in",
    )(xs, modsel, g.reshape(1, D_MODEL), w_main, w_dt, dt_bias, _conv_perm())


def _conv_kernel(main_ref, prev_ref, next_ref, w_ref, b_ref, unperm_ref, o_ref, act_ref, *, n_tiles):
    t = pl.program_id(1)
    has_prev = (t >= 2).astype(F32)
    has_next = jnp.logical_and(t >= 1, t < n_tiles - 1).astype(F32)
    span = TM // SUBLANES
    half = SSM_CONV // 2
    sub = lax.broadcasted_iota(jnp.int32, (SUBLANES, LANES), 0)

    def lane_tile(j, carry):
        lanes = pl.ds(pl.multiple_of(j * LANES, LANES), LANES)
        xm = main_ref[0, :, lanes].astype(F32)
        pv = prev_ref[0, :, lanes].astype(F32) * has_prev
        nx = next_ref[0, :, lanes].astype(F32) * has_next
        reg = {m: xm[m * SUBLANES:(m + 1) * SUBLANES] for m in range(span)}
        for i in range(1, half + 1):
            row = HALO - 1 - (i - 1) * SUBLANES
            edge = jnp.broadcast_to(pv[row:row + 1], (SUBLANES, LANES))
            reg[-i] = jnp.where(sub == 0, edge, pltpu.roll(reg[span - i], 1, 0))
            row = (i - 1) * SUBLANES
            edge = jnp.broadcast_to(nx[row:row + 1], (SUBLANES, LANES))
            reg[span + i - 1] = jnp.where(sub == SUBLANES - 1, edge, pltpu.roll(reg[i - 1], SUBLANES - 1, 0))
        wk = [jnp.broadcast_to(w_ref[k:k + 1, lanes], (SUBLANES, LANES)) for k in range(SSM_CONV)]
        bias = jnp.broadcast_to(b_ref[:, lanes], (SUBLANES, LANES))
        acts = []
        for m in range(span):
            acc = bias
            for k in range(SSM_CONV):
                acc = acc + wk[k] * reg[m + k - half]
            acts.append(_silu(acc))
        act_ref[:, lanes] = jnp.concatenate(acts, axis=0).astype(BF16)
        return carry

    lax.fori_loop(0, SSM_CONV_CH // LANES, lane_tile, 0, unroll=2)
    o_ref[0] = _mm(unperm_ref[...], act_ref[...]).astype(BF16)


def _conv(xbc, w, bias, ctx_len):
    b, t, ch = xbc.shape
    assert ctx_len == TM and TM % HALO == 0 and HALO >= (SSM_CONV // 2) * SUBLANES
    nt = t // TM
    per = TM // HALO
    main = pl.BlockSpec((1, TM, ch), lambda bb, tt: (bb, tt, 0))
    prev = pl.BlockSpec((1, HALO, ch), lambda bb, tt: (bb, jnp.maximum(tt * per - 1, 0), 0))
    nxt = pl.BlockSpec((1, HALO, ch), lambda bb, tt: (bb, jnp.minimum((tt + 1) * per, nt * per - 1), 0))
    return pl.pallas_call(
        functools.partial(_conv_kernel, n_tiles=nt),
        grid=(b, nt),
        in_specs=[main, prev, nxt, _resident(w.shape), _resident((1, ch)), _resident((TM, TM))],
        out_specs=main,
        out_shape=jax.ShapeDtypeStruct((b, t, ch), BF16),
        scratch_shapes=[pltpu.VMEM((TM, ch), BF16)],
        compiler_params=_params(2),
        name="dwconv_silu",
    )(xbc, xbc, xbc, w, bias.reshape(1, ch), _conv_perm().T)


def _dot_exact_lhs(a16, x):
    hi = x.astype(BF16)
    r1 = x - hi.astype(F32)
    mid = r1.astype(BF16)
    lo = (r1 - mid.astype(F32)).astype(BF16)
    return _mm(a16, hi) + _mm(a16, mid) + _mm(a16, lo)


def _ssd_kernel(xf_ref, xb_ref, dtf_ref, dtb_ref, a_ref, tri_ref, yf_ref, yb_ref, h_ref, row_ref):
    @pl.when(pl.program_id(1) == 0)
    def _():
        h_ref[...] = jnp.zeros_like(h_ref)

    left = lax.broadcasted_iota(jnp.int32, (1, LANES), 1) < SSM_HEAD_DIM
    keep_l = jnp.where(left, jnp.uint32(0xFFFFFFFF), jnp.uint32(0))
    keep_r = jnp.where(left, jnp.uint32(0), jnp.uint32(0xFFFFFFFF))
    ii = lax.broadcasted_iota(jnp.int32, (CHUNK, CHUNK), 0)
    jj = lax.broadcasted_iota(jnp.int32, (CHUNK, CHUNK), 1)
    b_off = SSM_INNER
    c_off = SSM_INNER + SSM_GN

    directions = ((xf_ref, dtf_ref, yf_ref), (xb_ref, dtb_ref, yb_ref))
    decays = []
    for d, (_, dt_ref, _) in enumerate(directions):
        dtc = dt_ref[0]
        acum = _dot_exact_lhs(tri_ref[d], dtc * a_ref[...]) * LOG2E
        acum_t = acum.T
        dt_t = dtc.T
        last = acum[CHUNK - 1:CHUNK, :] if d == 0 else acum[0:1, :]
        last_t = acum_t[:, CHUNK - 1:CHUNK] if d == 0 else acum_t[:, 0:1]
        row_ref[d, 0] = acum_t - jnp.log2(dt_t)
        row_ref[d, 1] = dt_t * jnp.exp2(last_t - acum_t)
        decays.append((acum, jnp.exp2(last)))

    for d, (x_ref, _, y_ref) in enumerate(directions):
        acum, elast = decays[d]
        mask = (ii >= jj) if d == 0 else (jj >= ii)

        for g in range(SSM_GROUPS):
            bg = x_ref[0, :, b_off + g * SSM_STATE:b_off + (g + 1) * SSM_STATE]
            cg = x_ref[0, :, c_off + g * SSM_STATE:c_off + (g + 1) * SSM_STATE]
            cb = _nt(cg, bg)
            bg_t = bg.astype(F32).T
            h_t = h_ref[d, g]
            y_int = _mm(cg, h_t.astype(BF16))
            for hp in range(SSM_HPG // 2):
                c0 = d * SSM_HEADS + g * SSM_HPG + 2 * hp
                ch = slice(g * SSM_GW + hp * LANES, g * SSM_GW + (hp + 1) * LANES)
                hl = slice(hp * LANES, (hp + 1) * LANES)
                xp = x_ref[0, :, ch]
                xu = pltpu.bitcast(xp, jnp.uint32)
                xbd = jnp.concatenate([pltpu.bitcast(xu & keep_l, BF16), pltpu.bitcast(xu & keep_r, BF16)], axis=0)
                intra, upd, eac = [], [], []
                for e in range(2):
                    c = c0 + e
                    acol = jnp.broadcast_to(acum[:, c:c + 1], (CHUNK, CHUNK))
                    lmat = jnp.exp2(jnp.where(mask, acol - row_ref[d, 0, c:c + 1, :], NEG))
                    intra.append((lmat * cb).astype(BF16))
                    upd.append((bg_t * row_ref[d, 1, c:c + 1, :]).astype(BF16))
                    eac.append(jnp.exp2(acol))
                lhs = jnp.concatenate([jnp.concatenate(intra, axis=1), jnp.concatenate(upd, axis=1)], axis=0)
                r = _mm(lhs, xbd)
                y_ref[0, :, ch] = r[:CHUNK] + y_int[:, hl] * jnp.where(left, eac[0], eac[1])
                el = jnp.where(left, jnp.broadcast_to(elast[:, c0:c0 + 1], (1, LANES)),
                               jnp.broadcast_to(elast[:, c0 + 1:c0 + 2], (1, LANES)))
                h_ref[d, g, :, hl] = el * h_t[:, hl] + r[CHUNK:]


def _ssd(xc, dt, a_row, ctx_len):
    b, t, ch = xc.shape
    ns = t // CHUNK
    fwd, bwd = _scan_orders(ns, ctx_len // CHUNK)
    idx = np.arange(CHUNK)
    tri = jnp.asarray(np.stack([idx[:, None] >= idx[None, :], idx[:, None] <= idx[None, :]]), BF16)

    def spec(order, w):
        return pl.BlockSpec((1, CHUNK, w), lambda bb, i: (bb, order(i), 0))

    return pl.pallas_call(
        _ssd_kernel,
        grid=(b, ns),
        in_specs=[spec(fwd, ch), spec(bwd, ch), spec(fwd, LANES), spec(bwd, LANES),
                  _resident((1, LANES)), _resident((2, CHUNK, CHUNK))],
        out_specs=[spec(fwd, SSM_INNER), spec(bwd, SSM_INNER)],
        out_shape=[jax.ShapeDtypeStruct((b, t, SSM_INNER), F32)] * 2,
        scratch_shapes=[pltpu.VMEM((2, SSM_GROUPS, SSM_STATE, SSM_GW), F32),
                        pltpu.VMEM((2, 2, CHUNK, CHUNK), F32)],
        compiler_params=_params(2),
        name="ssd",
    )(xc, xc, dt, dt, a_row, tri)


def _odd_out_kernel(x_ref, mod_ref, yf_ref, yb_ref, xc_ref, z_ref, dsk_ref, nssm_ref, wo_ref,
                    nmlp_ref, w1_ref, w2_ref, nfin_ref, o_ref, slots_ref, h2_ref):
    def step(cur_ref, nxt_ref):
        m = mod_ref[0, 0]
        def finish_rows(r):
            rows = slice(r * FIN_ROWS, (r + 1) * FIN_ROWS)
            y = yf_ref[0, rows, :] + yb_ref[0, rows, :] + dsk_ref[...] * xc_ref[0, rows, :].astype(F32)
            y = y * z_ref[0, rows, :].astype(F32)
            y = y * lax.rsqrt(jnp.mean(y * y, axis=-1, keepdims=True) + EPS) * nssm_ref[...]
            nxt_ref[rows, :] = y.astype(BF16)
            return _zero_token(y)

        x1 = x_ref[0] + m[2:3] * _mm(cur_ref[...], wo_ref[...])
        x2 = _mlp_tail(x1, m, nmlp_ref[...], w1_ref, w2_ref, h2_ref,
                       [functools.partial(finish_rows, r) for r in range(TM // FIN_ROWS)])
        o_ref[0] = x2 * lax.rsqrt(jnp.mean(x2 * x2, axis=-1, keepdims=True) + EPS) * nfin_ref[...]

    _two_phase(step, slots_ref)


def _odd_out(xs, modsel, yf, yb, xc, z, dsk, nssm, wo, nmlp, w1, w2, layer, nfin, ctx_len):
    b, t, _ = xs.shape
    skip = ctx_len // TM
    nt = t // TM - skip
    mm, fin = _two_phase_maps(nt, b * nt)
    fin_tok = lambda w: pl.BlockSpec((1, TM, w), lambda i: (fin(i)[0], fin(i)[1] + skip, 0))
    vec = lambda w: _resident((1, w))
    return pl.pallas_call(
        _odd_out_kernel,
        grid=(b * nt + 1,),
        in_specs=[pl.BlockSpec((1, TM, D_MODEL), lambda i: (mm(i)[0], mm(i)[1] + skip, 0)),
                  pl.BlockSpec((1, 1, 6, D_MODEL), lambda i: (mm(i)[0], 1, 0, 0)),
                  fin_tok(SSM_INNER), fin_tok(SSM_INNER), fin_tok(SSM_INNER), fin_tok(SSM_INNER),
                  vec(SSM_INNER), vec(SSM_INNER), _resident(wo.shape), vec(D_MODEL),
                  _resident_layer(w1.shape, layer), _resident_layer(w2.shape, layer), vec(D_MODEL)],
        out_specs=pl.BlockSpec((1, TM, D_MODEL), lambda i: (mm(i)[0], mm(i)[1], 0)),
        out_shape=jax.ShapeDtypeStruct((b, t - ctx_len, D_MODEL), F32),
        scratch_shapes=[pltpu.VMEM((2, TM, SSM_INNER), BF16), pltpu.VMEM((TM, D_MODEL), BF16)],
        compiler_params=_params(1),
        name="odd_out_mlp",
    )(xs, modsel, yf, yb, xc, z, dsk.reshape(1, -1), nssm.reshape(1, -1), wo, nmlp.reshape(1, -1),
      w1, w2, nfin.reshape(1, -1))


def kernel(x, c, ctx, c_ctx, w_mod, b_mod, norm_mix, norm_mlp, w_mlp_in, w_mlp_out, w_in_even, w_out_even,
           ret_decay_logit, na_rpb, w_in_odd, conv_w, conv_b, dt_bias, a_log, d_skip, ssm_norm, w_out_odd,
           norm_final):
    assert w_mod.shape[0] == DEPTH == 2 and x.shape[2] == D_MODEL
    ctx_len = ctx.shape[1]
    seq = x.shape[1]
    assert ctx_len == TM and seq % (2 * GRID_W) == 0

    modsel = _modulation(c, c_ctx, w_mod, b_mod)
    w1 = w_mlp_in.astype(BF16)
    w2 = w_mlp_out.astype(BF16)

    n_tok = 4 * RET_W + 2 * NA_W
    w_even = w_in_even[0].astype(BF16)
    rq, rk, rv, rg, nq, nk, nvt = _even_in(ctx, x, modsel[0], norm_mix[0], w_even[:, :n_tok], w_even[:, n_tok:].T,
                                           _rope_tables(seq, ctx_len))
    yf, yb = _retention(rq, rk, rv, _ret_tables(ret_decay_logit[0]), ctx_len)
    yna = _na(nq, nk, nvt, _na_bias_tables(na_rpb[0]), ctx_len)
    xs = _even_out(ctx, x, modsel[0], yf, yb, rg, yna, w_out_even[0].astype(BF16), norm_mlp[0], w1, w2, 0)

    n_main = SSM_INNER + SSM_CONV_CH
    wi = w_in_odd[0]
    w_dt = jnp.pad(wi[:, n_main:], ((0, 0), (0, LANES - 2 * SSM_HEADS))).astype(BF16)
    pad_row = lambda v: jnp.pad(v.astype(F32).reshape(1, -1), ((0, 0), (0, LANES - 2 * SSM_HEADS)))
    z, xbc, dt = _odd_in(xs, modsel[1], norm_mix[1], wi[:, :n_main].astype(BF16), w_dt, pad_row(dt_bias[0]))
    xc = _conv(xbc, conv_w[0].astype(F32), conv_b[0].astype(F32), ctx_len)
    yf, yb = _ssd(xc, dt, pad_row(-jnp.exp(a_log[0].astype(F32))), ctx_len)
    dsk = jnp.repeat(d_skip[0].astype(F32), SSM_HEAD_DIM)
    return _odd_out(xs, modsel[1], yf, yb, xc, z, dsk, ssm_norm[0], w_out_odd[0].astype(BF16), norm_mlp[1],
                    w1, w2, 1, norm_final, ctx_len)
```

```python
import functools

import numpy as np
import jax
import jax.numpy as jnp
from jax import lax
from jax.experimental import pallas as pl
from jax.experimental.pallas import tpu as pltpu

F32 = jnp.float32
BF16 = jnp.bfloat16

D_MODEL = 1024
D_FF = 4 * D_MODEL
DEPTH = 2
GRID_W = 64
EPS = 1e-6
ROPE_BASE = 10000.0

RET_HEADS = 4
RET_DIM = 128
RET_W = RET_HEADS * RET_DIM

NA_HEADS = 8
NA_DIM = 64
NA_W = NA_HEADS * NA_DIM
NA_WIN_R = 8
NA_WIN_C = 16

SSM_INNER = 2 * D_MODEL
SSM_HEAD_DIM = 64
SSM_HEADS = SSM_INNER // SSM_HEAD_DIM
SSM_GROUPS = 4
SSM_HPG = SSM_HEADS // SSM_GROUPS
SSM_STATE = 128
SSM_CONV = 7
SSM_GN = SSM_GROUPS * SSM_STATE
SSM_CONV_CH = SSM_INNER + 2 * SSM_GN
SSM_GW = SSM_HPG * SSM_HEAD_DIM

LANES = 128
SUBLANES = 8
TM = 256
CHUNK = 128
SCAN_BLOCK = 2 * CHUNK
FIN_ROWS = 32
HALO = 32
LOG2E = 1.4426950408889634
NEG = -1e30
NA_KROWS = 10
VMEM_LIMIT = 56 * 1024 * 1024


def _params(n_axes):
    return pltpu.CompilerParams(dimension_semantics=("arbitrary",) * n_axes, vmem_limit_bytes=VMEM_LIMIT)


def _resident(shape):
    nd = len(shape)
    return pl.BlockSpec(shape, lambda *_: (0,) * nd, pipeline_mode=pl.Buffered(1))


def _resident_layer(shape, layer):
    nd = len(shape)
    return pl.BlockSpec((None,) + tuple(shape[1:]), lambda *_: (layer,) + (0,) * (nd - 1),
                        pipeline_mode=pl.Buffered(1))


def _nt(a, b):
    return lax.dot_general(a, b, (((1,), (1,)), ((), ())), preferred_element_type=F32)


def _tn(a, b):
    return lax.dot_general(a, b, (((0,), (0,)), ((), ())), preferred_element_type=F32)


def _mm(a, b):
    return jnp.dot(a, b, preferred_element_type=F32)


def _silu(v):
    return v * jax.nn.sigmoid(v)


def _norm_mod(x, g, sc, sh):
    ms = jnp.mean(x * x, axis=-1, keepdims=True)
    return (x * lax.rsqrt(ms + EPS) * g) * (1.0 + sc) + sh


def _mod_kernel(cc_ref, w_ref, b_ref, o_ref):
    s = _silu(cc_ref[...])
    o_ref[0] = _mm(s.astype(BF16), w_ref[0].astype(BF16)) + b_ref[0]


def _modulation(c, c_ctx, w_mod, b_mod):
    b = c.shape[0]
    rows = -(-(b + 1) // 8) * 8
    cc = jnp.zeros((rows, D_MODEL), F32).at[:b].set(c).at[b].set(c_ctx)
    tn = 1536
    out = pl.pallas_call(
        _mod_kernel,
        grid=(DEPTH, 6 * D_MODEL // tn),
        in_specs=[pl.BlockSpec((rows, D_MODEL), lambda l, j: (0, 0)),
                  pl.BlockSpec((1, D_MODEL, tn), lambda l, j: (l, 0, j)),
                  pl.BlockSpec((1, 1, tn), lambda l, j: (l, 0, j))],
        out_specs=pl.BlockSpec((1, rows, tn), lambda l, j: (l, 0, j)),
        out_shape=jax.ShapeDtypeStruct((DEPTH, rows, 6 * D_MODEL), F32),
        compiler_params=_params(2),
        name="modulation",
    )(cc, w_mod, b_mod.reshape(DEPTH, 1, 6 * D_MODEL))
    mod = out.reshape(DEPTH, rows, 6, D_MODEL)
    return jnp.stack([jnp.broadcast_to(mod[:, b:b + 1], (DEPTH, b, 6, D_MODEL)), mod[:, :b]], axis=2)


def _mod_spec():
    return pl.BlockSpec((1, 1, 6, D_MODEL), lambda b, t: (b, jnp.minimum(t, 1), 0, 0))


def _rope_tables(seq, ctx_len):
    pos = np.arange(seq)
    nf = RET_DIM // 4
    inv = (ROPE_BASE ** (-np.arange(nf, dtype=np.float32) / nf)).astype(np.float32)

    def cs(p):
        ang = p.astype(np.float32)[:, None] * inv[None, :]
        return np.cos(ang), np.sin(ang)

    cr, sr = cs(pos // GRID_W)
    cc, sc = cs(pos % GRID_W)
    zero = np.zeros_like(sr)
    cos = np.concatenate([cr, cr, cc, cc], axis=1)
    sin_a = np.concatenate([-sr, zero, -sc, zero], axis=1)
    sin_b = np.concatenate([zero, sr, zero, sc], axis=1)
    ident = np.ones((ctx_len, RET_DIM), np.float32)
    nul = np.zeros((ctx_len, RET_DIM), np.float32)
    return tuple(jnp.asarray(np.concatenate(parts, axis=0), F32)
                 for parts in ([ident, cos], [nul, sin_a], [nul, sin_b]))


def _stream_tile(ctx_ref, x_ref):
    return jnp.where(pl.program_id(1) == 0, ctx_ref[0], x_ref[0])


def _stream_specs():
    return [pl.BlockSpec((1, TM, D_MODEL), lambda bb, tt: (bb, 0, 0)),
            pl.BlockSpec((1, TM, D_MODEL), lambda bb, tt: (bb, jnp.maximum(tt - 1, 0), 0))]


def _even_in_kernel(ctx_ref, x_ref, mod_ref, g_ref, w_ref, wvt_ref, cos_ref, sa_ref, sb_ref,
                    rq_ref, rk_ref, rv_ref, rg_ref, nq_ref, nk_ref, nvt_ref):
    m = mod_ref[0, 0]
    h = _norm_mod(_stream_tile(ctx_ref, x_ref), g_ref[...], m[1:2], m[0:1]).astype(BF16)
    cos, sa, sb = cos_ref[...], sa_ref[...], sb_ref[...]

    def proj(j):
        return _mm(h, w_ref[:, j * RET_W:(j + 1) * RET_W])

    def rope_store(o_ref, r, scale):
        for hh in range(RET_HEADS):
            t = r[:, hh * RET_DIM:(hh + 1) * RET_DIM]
            o = t * cos + pltpu.roll(t, RET_DIM - 32, 1) * sa + pltpu.roll(t, 32, 1) * sb
            o_ref[0, :, hh * RET_DIM:(hh + 1) * RET_DIM] = (o * scale).astype(BF16)

    rope_store(rq_ref, proj(0), 1.0)
    rope_store(rk_ref, proj(1), RET_DIM ** -0.5)
    rv_ref[0] = proj(2).astype(BF16)
    rg_ref[0] = _silu(proj(3)).astype(BF16)
    nq_ref[0] = (proj(4) * (NA_DIM ** -0.5 * LOG2E)).astype(BF16)
    nk_ref[0] = proj(5).astype(BF16)
    nvt_ref[0] = _nt(wvt_ref[...], h).astype(BF16)


def _even_in(ctx, x, modsel, g, w, w_vt, tables):
    b = x.shape[0]
    t = ctx.shape[1] + x.shape[1]
    tab = pl.BlockSpec((TM, RET_DIM), lambda bb, tt: (tt, 0))
    out = pl.BlockSpec((1, TM, RET_W), lambda bb, tt: (bb, tt, 0))
    return pl.pallas_call(
        _even_in_kernel,
        grid=(b, t // TM),
        in_specs=_stream_specs() + [_mod_spec(), _resident((1, D_MODEL)), _resident(w.shape),
                                    _resident(w_vt.shape), tab, tab, tab],
        out_specs=[out] * 6 + [pl.BlockSpec((1, NA_W, TM), lambda bb, tt: (bb, 0, tt))],
        out_shape=[jax.ShapeDtypeStruct((b, t, RET_W), BF16)] * 6 + [jax.ShapeDtypeStruct((b, NA_W, t), BF16)],
        compiler_params=_params(2),
        name="even_in",
    )(ctx, x, modsel, g.reshape(1, D_MODEL), w, w_vt, *tables)


def _ret_tables(decay_logit):
    log_g = jax.nn.log_sigmoid(decay_logit.astype(F32))
    idx = jnp.arange(CHUNK, dtype=F32)
    diff = idx[:, None] - idx[None, :]
    ones = jnp.ones((CHUNK, CHUNK), F32)

    def one(lg, d):
        dd = diff if d == 0 else -diff
        dm = jnp.where(dd >= 0, jnp.exp(jnp.maximum(dd, 0.0) * lg), 0.0)
        qe = (idx + 1.0) if d == 0 else (CHUNK - idx)
        ke = (CHUNK - 1.0 - idx) if d == 0 else idx
        return jnp.stack([dm, jnp.exp(qe * lg)[:, None] * ones, jnp.exp(ke * lg)[:, None] * ones,
                          jnp.exp(CHUNK * lg) * ones])

    return jnp.stack([jnp.stack([one(log_g[d, h], d) for h in range(RET_HEADS)]) for d in range(2)])


def _ret_kernel(qf_ref, kf_ref, vf_ref, qb_ref, kb_ref, vb_ref, dec_ref, yf_ref, yb_ref, s_ref):
    @pl.when(pl.program_id(1) == 0)
    def _():
        s_ref[...] = jnp.zeros_like(s_ref)

    refs = ((qf_ref, kf_ref, vf_ref, yf_ref), (qb_ref, kb_ref, vb_ref, yb_ref))
    chains = [(d, hh, slice(hh * RET_DIM, (hh + 1) * RET_DIM)) for d in range(2) for hh in range(RET_HEADS)]
    n_sub = qf_ref.shape[1] // CHUNK
    for sub in range(n_sub):
        rows = [slice(c * CHUNK, (c + 1) * CHUNK) for c in (sub, n_sub - 1 - sub)]
        stage1 = []
        for d, hh, sl in chains:
            q_ref, k_ref, _, _ = refs[d]
            q = q_ref[0, rows[d], sl]
            stage1.append((_nt(q, k_ref[0, rows[d], sl]), _mm(q, s_ref[d, hh].astype(BF16))))
        for (d, hh, sl), (att, inter) in zip(chains, stage1):
            _, _, v_ref, y_ref = refs[d]
            y_ref[0, rows[d], sl] = (_mm((att * dec_ref[d, hh, 0]).astype(BF16), v_ref[0, rows[d], sl])
                                     + inter * dec_ref[d, hh, 1])
        for d, hh, sl in chains:
            _, k_ref, v_ref, _ = refs[d]
            kk = (k_ref[0, rows[d], sl].astype(F32) * dec_ref[d, hh, 2]).astype(BF16)
            s_ref[d, hh] = dec_ref[d, hh, 3] * s_ref[d, hh] + _tn(kk, v_ref[0, rows[d], sl])


def _scan_orders(n_steps, n_ctx):
    fwd = lambda i: i
    bwd = lambda i: jnp.where(i < n_ctx, n_ctx - 1 - i, n_steps + n_ctx - 1 - i)
    return fwd, bwd


def _retention(rq, rk, rv, dec, ctx_len):
    b, t, _ = rq.shape
    ns = t // SCAN_BLOCK
    fwd, bwd = _scan_orders(ns, ctx_len // SCAN_BLOCK)

    def spec(order):
        return pl.BlockSpec((1, SCAN_BLOCK, RET_W), lambda bb, i: (bb, order(i), 0))

    return pl.pallas_call(
        _ret_kernel,
        grid=(b, ns),
        in_specs=[spec(fwd)] * 3 + [spec(bwd)] * 3 + [_resident(dec.shape)],
        out_specs=[spec(fwd), spec(bwd)],
        out_shape=[jax.ShapeDtypeStruct((b, t, RET_W), F32)] * 2,
        scratch_shapes=[pltpu.VMEM((2, RET_HEADS, RET_DIM, RET_DIM), F32)],
        compiler_params=_params(2),
        name="retention",
    )(rq, rk, rv, rq, rk, rv, dec)


def _na_bias_tables(rpb):
    qc = np.arange(GRID_W)[:, None]
    kc = np.arange(GRID_W)[None, :]
    cstart = np.clip(qc - NA_WIN_C // 2, 0, GRID_W - NA_WIN_C)
    inwin = (kc >= cstart) & (kc < cstart + NA_WIN_C)
    dcol = np.clip(kc - qc, -(NA_WIN_C - 1), NA_WIN_C - 1) + (NA_WIN_C - 1)
    onehot = np.zeros((GRID_W, GRID_W, 2 * NA_WIN_C - 1), np.float32)
    onehot[qc, kc, dcol] = 1.0
    t = jnp.einsum("hrd,qkd->hrkq", rpb.astype(F32), jnp.asarray(onehot), precision=lax.Precision.HIGHEST)
    t = jnp.where(jnp.asarray(inwin.T), t * LOG2E, NEG)
    pad = jnp.full((NA_HEADS, 2, GRID_W, GRID_W), NEG, F32)
    tp = jnp.concatenate([pad, t, pad], axis=1)
    n = 2 * NA_WIN_R + 1
    top = jnp.concatenate([tp[:, 1:1 + n], tp[:, 0:n]], axis=-1)
    bot = jnp.concatenate([tp[:, 2:2 + n], tp[:, 1:1 + n]], axis=-1)
    return jnp.concatenate([top, bot], axis=-2)


def _na_kernel(q_ref, k_ref, vt_ref, tbl_ref, o_ref, *, rows, ctx_len):
    s = pl.program_id(1)
    n_ctx = ctx_len // SCAN_BLOCK
    n_unit = SCAN_BLOCK // CHUNK
    left = lax.broadcasted_iota(jnp.int32, (1, LANES), 1) < NA_DIM
    top_half = lax.broadcasted_iota(jnp.int32, (CHUNK, LANES), 0) < GRID_W
    left_half = lax.broadcasted_iota(jnp.int32, (CHUNK, LANES), 1) < GRID_W
    n_loc = NA_KROWS * GRID_W
    n_blk = NA_KROWS // 2

    def attend(local_of_unit):
        items = [(u, hp, slice(u * CHUNK, (u + 1) * CHUNK), slice(hp * LANES, (hp + 1) * LANES))
                 for u in range(n_unit) for hp in range(NA_HEADS // 2)]
        scores = []
        for u, hp, qrows, sl in items:
            qp = q_ref[0, qrows, sl]
            zero = jnp.zeros_like(qp)
            qboth = jnp.concatenate([jnp.where(left, qp, zero), jnp.where(left, zero, qp)], axis=0)
            s_ctx = _nt(k_ref[0, 0:ctx_len, sl], qboth)
            s_loc = None
            if local_of_unit[u] is not None:
                off, idx, pen = local_of_unit[u]
                s_loc = _nt(k_ref[0, pl.ds(off, n_loc), sl], qboth)
                s_loc = s_loc + jnp.concatenate(
                    [jnp.concatenate([tbl_ref[2 * hp, idx[m]] + pen[m], tbl_ref[2 * hp + 1, idx[m]] + pen[m]],
                                     axis=1) for m in range(n_blk)], axis=0)
            scores.append((s_ctx, s_loc))
        probs = []
        for s_ctx, s_loc in scores:
            mx = jnp.max(s_ctx, axis=0, keepdims=True)
            if s_loc is not None:
                mx = jnp.maximum(mx, jnp.max(s_loc, axis=0, keepdims=True))
            p_ctx = jnp.exp2(s_ctx - mx)
            den = jnp.sum(p_ctx, axis=0, keepdims=True)
            p_loc = None
            if s_loc is not None:
                p_loc = jnp.exp2(s_loc - mx)
                den = den + jnp.sum(p_loc, axis=0, keepdims=True)
                p_loc = p_loc.astype(BF16)
            probs.append((p_ctx.astype(BF16), p_loc, den))
        for (u, hp, qrows, sl), (p_ctx, p_loc, den) in zip(items, probs):
            ot = _mm(vt_ref[0, sl, 0:ctx_len], p_ctx)
            if p_loc is not None:
                ot = ot + _mm(vt_ref[0, sl, pl.ds(local_of_unit[u][0], n_loc)], p_loc)
            ot = ot / den
            pair_t = jnp.concatenate([ot[0:NA_DIM, 0:CHUNK], ot[NA_DIM:, CHUNK:]], axis=0)
            o_ref[0, qrows, sl] = pair_t.T.astype(BF16)

    @pl.when(s < n_ctx)
    def _():
        attend([None] * n_unit)

    @pl.when(s >= n_ctx)
    def _():
        def row_pen(kr, r):
            rs = jnp.clip(r - NA_WIN_R // 2, 0, rows - NA_WIN_R)
            return jnp.where(jnp.logical_and(kr >= rs, kr < rs + NA_WIN_R), 0.0, NEG)

        local_of_unit = []
        for u in range(n_unit):
            r0 = 2 * ((s - n_ctx) * n_unit + u)
            kb = jnp.clip(r0 - NA_WIN_R // 2, 0, rows - NA_KROWS)
            off = pl.multiple_of(ctx_len + kb * GRID_W, LANES)
            idx, pen = [], []
            for m in range(n_blk):
                kp = kb + 2 * m
                idx.append(kp - r0 + NA_WIN_R)
                pen.append(jnp.where(top_half,
                                     jnp.where(left_half, row_pen(kp, r0), row_pen(kp, r0 + 1)),
                                     jnp.where(left_half, row_pen(kp + 1, r0), row_pen(kp + 1, r0 + 1))))
            local_of_unit.append((off, idx, pen))
        attend(local_of_unit)


def _na(nq, nk, nvt, tbl, ctx_len):
    b, t, _ = nq.shape
    rows = (t - ctx_len) // GRID_W
    assert rows >= NA_KROWS and CHUNK == 2 * GRID_W == LANES
    assert (t - ctx_len) % SCAN_BLOCK == 0 and ctx_len % SCAN_BLOCK == 0
    blk = pl.BlockSpec((1, SCAN_BLOCK, NA_W), lambda bb, s: (bb, s, 0))
    return pl.pallas_call(
        functools.partial(_na_kernel, rows=rows, ctx_len=ctx_len),
        grid=(b, t // SCAN_BLOCK),
        in_specs=[blk, pl.BlockSpec((1, t, NA_W), lambda bb, s: (bb, 0, 0)),
                  pl.BlockSpec((1, NA_W, t), lambda bb, s: (bb, 0, 0)), _resident(tbl.shape)],
        out_specs=blk,
        out_shape=jax.ShapeDtypeStruct((b, t, NA_W), BF16),
        compiler_params=_params(2),
        name="na",
    )(nq, nk, nvt, tbl)


def _zero_token(v):
    bits = lax.bitcast_convert_type(v[0:SUBLANES, 0:LANES], jnp.int32)
    return lax.shift_right_logical(lax.shift_right_logical(bits, 16), 16)


def _mlp_tail(x1, m, nmlp, w1_ref, w2_ref, h2_ref, fillers=()):
    fillers = list(fillers)

    def fill():
        if fillers:
            token = fillers.pop(0)()
            head = pltpu.bitcast(h2_ref[0:2 * SUBLANES, 0:LANES], jnp.int32)
            h2_ref[0:2 * SUBLANES, 0:LANES] = pltpu.bitcast(head + token, BF16)

    h2_ref[...] = _norm_mod(x1, nmlp, m[4:5], m[3:4]).astype(BF16)
    fc = D_MODEL
    n_stage = D_FF // fc
    acc = jnp.zeros_like(x1)
    for j in range(n_stage):
        while len(fillers) > 2 * (n_stage - 1 - j):
            fill()
        u = _mm(h2_ref[...], w1_ref[:, j * fc:(j + 1) * fc])
        fill()
        u = jnp.square(jnp.maximum(u, 0.0)).astype(BF16)
        acc = acc + _mm(u, w2_ref[j * fc:(j + 1) * fc, :])
        fill()
    return x1 + m[5:6] * acc


def _two_phase(step_fn, slots_ref):
    i = pl.program_id(0)

    @pl.when(i == 0)
    def _():
        slots_ref[...] = jnp.zeros_like(slots_ref)

    for parity in range(2):
        @pl.when(i % 2 == parity)
        def _():
            step_fn(slots_ref.at[1 - parity], slots_ref.at[parity])


def _two_phase_maps(n_tiles, n_total):
    def mm(i):
        j = jnp.maximum(i - 1, 0)
        return j // n_tiles, j % n_tiles

    def fin(i):
        k = jnp.minimum(i, n_total - 1)
        return k // n_tiles, k % n_tiles

    return mm, fin


def _even_out_kernel(ctx_ref, x_ref, mod_ref, yna_ref, yf_ref, yb_ref, rg_ref, wo_ref, nmlp_ref, w1_ref, w2_ref,
                     o_ref, slots_ref, h2_ref, *, n_tiles):
    def step(cur_ref, nxt_ref):
        m = mod_ref[0, 0]
        is_ctx = (jnp.maximum(pl.program_id(0) - 1, 0) % n_tiles) == 0
        def finish_head(hh):
            sl = slice(hh * RET_DIM, (hh + 1) * RET_DIM)
            yh = yf_ref[0, :, sl] + yb_ref[0, :, sl]
            dlt = yh - jnp.mean(yh, axis=-1, keepdims=True)
            yn = dlt * lax.rsqrt(jnp.mean(dlt * dlt, axis=-1, keepdims=True) + EPS)
            nxt_ref[:, sl] = (yn * rg_ref[0, :, sl].astype(F32)).astype(BF16)
            return _zero_token(yn)

        mix = _mm(yna_ref[0], wo_ref[RET_W:RET_W + NA_W, :]) + _mm(cur_ref[...], wo_ref[0:RET_W, :])
        x1 = jnp.where(is_ctx, ctx_ref[0], x_ref[0]) + m[2:3] * mix
        o_ref[0] = _mlp_tail(x1, m, nmlp_ref[...], w1_ref, w2_ref, h2_ref,
                             [functools.partial(finish_head, hh) for hh in range(RET_HEADS)])

    _two_phase(step, slots_ref)


def _even_out(ctx, x, modsel, yf, yb, rg, yna, wo, nmlp, w1, w2, layer):
    b = x.shape[0]
    t = ctx.shape[1] + x.shape[1]
    nt = t // TM
    mm, fin = _two_phase_maps(nt, b * nt)

    def mm_tok(w):
        return pl.BlockSpec((1, TM, w), lambda i: (mm(i)[0], mm(i)[1], 0))

    def fin_tok(w):
        return pl.BlockSpec((1, TM, w), lambda i: (fin(i)[0], fin(i)[1], 0))

    return pl.pallas_call(
        functools.partial(_even_out_kernel, n_tiles=nt),
        grid=(b * nt + 1,),
        in_specs=[pl.BlockSpec((1, TM, D_MODEL), lambda i: (mm(i)[0], 0, 0)),
                  pl.BlockSpec((1, TM, D_MODEL), lambda i: (mm(i)[0], jnp.maximum(mm(i)[1] - 1, 0), 0)),
                  pl.BlockSpec((1, 1, 6, D_MODEL), lambda i: (mm(i)[0], jnp.minimum(mm(i)[1], 1), 0, 0)),
                  mm_tok(NA_W), fin_tok(RET_W), fin_tok(RET_W), fin_tok(RET_W),
                  _resident(wo.shape), _resident((1, D_MODEL)), _resident_layer(w1.shape, layer),
                  _resident_layer(w2.shape, layer)],
        out_specs=mm_tok(D_MODEL),
        out_shape=jax.ShapeDtypeStruct((b, t, D_MODEL), F32),
        scratch_shapes=[pltpu.VMEM((2, TM, RET_W), BF16), pltpu.VMEM((TM, D_MODEL), BF16)],
        compiler_params=_params(1),
        name="even_out_mlp",
    )(ctx, x, modsel, yna, yf, yb, rg, wo, nmlp.reshape(1, D_MODEL), w1, w2)


def _conv_perm():
    span = TM // SUBLANES
    rho = np.arange(TM)
    p = np.zeros((TM, TM), np.float32)
    p[rho, (rho % SUBLANES) * span + rho // SUBLANES] = 1.0
    return jnp.asarray(p, BF16)


def _odd_in_kernel(x_ref, mod_ref, g_ref, w_ref, wdt_ref, dtb_ref, perm_ref, z_ref, xbc_ref, dt_ref):
    m = mod_ref[0, 0]
    h = _norm_mod(x_ref[0], g_ref[...], m[1:2], m[0:1]).astype(BF16)
    cw = 512
    for j in range(SSM_INNER // cw):
        z_ref[0, :, j * cw:(j + 1) * cw] = _silu(_mm(h, w_ref[:, j * cw:(j + 1) * cw])).astype(BF16)
    hp = _mm(perm_ref[...], h).astype(BF16)
    for j in range(SSM_CONV_CH // cw):
        c0 = SSM_INNER + j * cw
        xbc_ref[0, :, j * cw:(j + 1) * cw] = _mm(hp, w_ref[:, c0:c0 + cw]).astype(BF16)
    raw = _mm(h, wdt_ref[...]) + dtb_ref[...]
    dt_ref[0] = jnp.maximum(raw, 0.0) + jnp.log1p(jnp.exp(-jnp.abs(raw)))


def _odd_in(xs, modsel, g, w_main, w_dt, dt_bias):
    b, t, _ = xs.shape
    tok = lambda w: pl.BlockSpec((1, TM, w), lambda bb, tt: (bb, tt, 0))
    return pl.pallas_call(
        _odd_in_kernel,
        grid=(b, t // TM),
        in_specs=[tok(D_MODEL), _mod_spec(), _resident((1, D_MODEL)), _resident(w_main.shape),
                  _resident(w_dt.shape), _resident((1, LANES)), _resident((TM, TM))],
        out_specs=[tok(SSM_INNER), tok(SSM_CONV_CH), tok(LANES)],
        out_shape=[jax.ShapeDtypeStruct((b, t, SSM_INNER), BF16),
                   jax.ShapeDtypeStruct((b, t, SSM_CONV_CH), BF16),
                   jax.ShapeDtypeStruct((b, t, LANES), F32)],
        compiler_params=_params(2),
        name="odd_in",
    )(xs, modsel, g.reshape(1, D_MODEL), w_main, w_dt, dt_bias, _conv_perm())


def _conv_kernel(main_ref, prev_ref, next_ref, w_ref, b_ref, unperm_ref, o_ref, act_ref, *, n_tiles):
    t = pl.program_id(1)
    has_prev = (t >= 2).astype(F32)
    has_next = jnp.logical_and(t >= 1, t < n_tiles - 1).astype(F32)
    span = TM // SUBLANES
    half = SSM_CONV // 2
    sub = lax.broadcasted_iota(jnp.int32, (SUBLANES, LANES), 0)

    def lane_tile(j, carry):
        lanes = pl.ds(pl.multiple_of(j * LANES, LANES), LANES)
        xm = main_ref[0, :, lanes].astype(F32)
        pv = prev_ref[0, :, lanes].astype(F32) * has_prev
        nx = next_ref[0, :, lanes].astype(F32) * has_next
        reg = {m: xm[m * SUBLANES:(m + 1) * SUBLANES] for m in range(span)}
        for i in range(1, half + 1):
            row = HALO - 1 - (i - 1) * SUBLANES
            edge = jnp.broadcast_to(pv[row:row + 1], (SUBLANES, LANES))
            reg[-i] = jnp.where(sub == 0, edge, pltpu.roll(reg[span - i], 1, 0))
            row = (i - 1) * SUBLANES
            edge = jnp.broadcast_to(nx[row:row + 1], (SUBLANES, LANES))
            reg[span + i - 1] = jnp.where(sub == SUBLANES - 1, edge, pltpu.roll(reg[i - 1], SUBLANES - 1, 0))
        wk = [jnp.broadcast_to(w_ref[k:k + 1, lanes], (SUBLANES, LANES)) for k in range(SSM_CONV)]
        bias = jnp.broadcast_to(b_ref[:, lanes], (SUBLANES, LANES))
        acts = []
        for m in range(span):
            acc = bias
            for k in range(SSM_CONV):
                acc = acc + wk[k] * reg[m + k - half]
            acts.append(_silu(acc))
        act_ref[:, lanes] = jnp.concatenate(acts, axis=0).astype(BF16)
        return carry

    lax.fori_loop(0, SSM_CONV_CH // LANES, lane_tile, 0, unroll=2)
    o_ref[0] = _mm(unperm_ref[...], act_ref[...]).astype(BF16)


def _conv(xbc, w, bias, ctx_len):
    b, t, ch = xbc.shape
    assert ctx_len == TM and TM % HALO == 0 and HALO >= (SSM_CONV // 2) * SUBLANES
    nt = t // TM
    per = TM // HALO
    main = pl.BlockSpec((1, TM, ch), lambda bb, tt: (bb, tt, 0))
    prev = pl.BlockSpec((1, HALO, ch), lambda bb, tt: (bb, jnp.maximum(tt * per - 1, 0), 0))
    nxt = pl.BlockSpec((1, HALO, ch), lambda bb, tt: (bb, jnp.minimum((tt + 1) * per, nt * per - 1), 0))
    return pl.pallas_call(
        functools.partial(_conv_kernel, n_tiles=nt),
        grid=(b, nt),
        in_specs=[main, prev, nxt, _resident(w.shape), _resident((1, ch)), _resident((TM, TM))],
        out_specs=main,
        out_shape=jax.ShapeDtypeStruct((b, t, ch), BF16),
        scratch_shapes=[pltpu.VMEM((TM, ch), BF16)],
        compiler_params=_params(2),
        name="dwconv_silu",
    )(xbc, xbc, xbc, w, bias.reshape(1, ch), _conv_perm().T)


def _dot_exact_lhs(a16, x):
    hi = x.astype(BF16)
    r1 = x - hi.astype(F32)
    mid = r1.astype(BF16)
    lo = (r1 - mid.astype(F32)).astype(BF16)
    return _mm(a16, hi) + _mm(a16, mid) + _mm(a16, lo)


def _ssd_kernel(xf_ref, xb_ref, dtf_ref, dtb_ref, a_ref, tri_ref, yf_ref, yb_ref, h_ref, row_ref):
    @pl.when(pl.program_id(1) == 0)
    def _():
        h_ref[...] = jnp.zeros_like(h_ref)

    left = lax.broadcasted_iota(jnp.int32, (1, LANES), 1) < SSM_HEAD_DIM
    keep_l = jnp.where(left, jnp.uint32(0xFFFFFFFF), jnp.uint32(0))
    keep_r = jnp.where(left, jnp.uint32(0), jnp.uint32(0xFFFFFFFF))
    ii = lax.broadcasted_iota(jnp.int32, (CHUNK, CHUNK), 0)
    jj = lax.broadcasted_iota(jnp.int32, (CHUNK, CHUNK), 1)
    b_off = SSM_INNER
    c_off = SSM_INNER + SSM_GN

    directions = ((xf_ref, dtf_ref, yf_ref), (xb_ref, dtb_ref, yb_ref))
    decays = []
    for d, (_, dt_ref, _) in enumerate(directions):
        dtc = dt_ref[0]
        acum = _dot_exact_lhs(tri_ref[d], dtc * a_ref[...]) * LOG2E
        acum_t = acum.T
        dt_t = dtc.T
        last = acum[CHUNK - 1:CHUNK, :] if d == 0 else acum[0:1, :]
        last_t = acum_t[:, CHUNK - 1:CHUNK] if d == 0 else acum_t[:, 0:1]
        row_ref[d, 0] = acum_t - jnp.log2(dt_t)
        row_ref[d, 1] = dt_t * jnp.exp2(last_t - acum_t)
        decays.append((acum, jnp.exp2(last)))

    for d, (x_ref, _, y_ref) in enumerate(directions):
        acum, elast = decays[d]
        mask = (ii >= jj) if d == 0 else (jj >= ii)

        for g in range(SSM_GROUPS):
            bg = x_ref[0, :, b_off + g * SSM_STATE:b_off + (g + 1) * SSM_STATE]
            cg = x_ref[0, :, c_off + g * SSM_STATE:c_off + (g + 1) * SSM_STATE]
            cb = _nt(cg, bg)
            bg_t = bg.astype(F32).T
            h_t = h_ref[d, g]
            y_int = _mm(cg, h_t.astype(BF16))
            for hp in range(SSM_HPG // 2):
                c0 = d * SSM_HEADS + g * SSM_HPG + 2 * hp
                ch = slice(g * SSM_GW + hp * LANES, g * SSM_GW + (hp + 1) * LANES)
                hl = slice(hp * LANES, (hp + 1) * LANES)
                xp = x_ref[0, :, ch]
                xu = pltpu.bitcast(xp, jnp.uint32)
                xbd = jnp.concatenate([pltpu.bitcast(xu & keep_l, BF16), pltpu.bitcast(xu & keep_r, BF16)], axis=0)
                intra, upd, eac = [], [], []
                for e in range(2):
                    c = c0 + e
                    acol = jnp.broadcast_to(acum[:, c:c + 1], (CHUNK, CHUNK))
                    lmat = jnp.exp2(jnp.where(mask, acol - row_ref[d, 0, c:c + 1, :], NEG))
                    intra.append((lmat * cb).astype(BF16))
                    upd.append((bg_t * row_ref[d, 1, c:c + 1, :]).astype(BF16))
                    eac.append(jnp.exp2(acol))
                lhs = jnp.concatenate([jnp.concatenate(intra, axis=1), jnp.concatenate(upd, axis=1)], axis=0)
                r = _mm(lhs, xbd)
                y_ref[0, :, ch] = r[:CHUNK] + y_int[:, hl] * jnp.where(left, eac[0], eac[1])
                el = jnp.where(left, jnp.broadcast_to(elast[:, c0:c0 + 1], (1, LANES)),
                               jnp.broadcast_to(elast[:, c0 + 1:c0 + 2], (1, LANES)))
                h_ref[d, g, :, hl] = el * h_t[:, hl] + r[CHUNK:]


def _ssd(xc, dt, a_row, ctx_len):
    b, t, ch = xc.shape
    ns = t // CHUNK
    fwd, bwd = _scan_orders(ns, ctx_len // CHUNK)
    idx = np.arange(CHUNK)
    tri = jnp.asarray(np.stack([idx[:, None] >= idx[None, :], idx[:, None] <= idx[None, :]]), BF16)

    def spec(order, w):
        return pl.BlockSpec((1, CHUNK, w), lambda bb, i: (bb, order(i), 0))

    return pl.pallas_call(
        _ssd_kernel,
        grid=(b, ns),
        in_specs=[spec(fwd, ch), spec(bwd, ch), spec(fwd, LANES), spec(bwd, LANES),
                  _resident((1, LANES)), _resident((2, CHUNK, CHUNK))],
        out_specs=[spec(fwd, SSM_INNER), spec(bwd, SSM_INNER)],
        out_shape=[jax.ShapeDtypeStruct((b, t, SSM_INNER), F32)] * 2,
        scratch_shapes=[pltpu.VMEM((2, SSM_GROUPS, SSM_STATE, SSM_GW), F32),
                        pltpu.VMEM((2, 2, CHUNK, CHUNK), F32)],
        compiler_params=_params(2),
        name="ssd",
    )(xc, xc, dt, dt, a_row, tri)


def _odd_out_kernel(x_ref, mod_ref, yf_ref, yb_ref, xc_ref, z_ref, dsk_ref, nssm_ref, wo_ref,
                    nmlp_ref, w1_ref, w2_ref, nfin_ref, o_ref, slots_ref, h2_ref):
    def step(cur_ref, nxt_ref):
        m = mod_ref[0, 0]
        def finish_rows(r):
            rows = slice(r * FIN_ROWS, (r + 1) * FIN_ROWS)
            y = yf_ref[0, rows, :] + yb_ref[0, rows, :] + dsk_ref[...] * xc_ref[0, rows, :].astype(F32)
            y = y * z_ref[0, rows, :].astype(F32)
            y = y * lax.rsqrt(jnp.mean(y * y, axis=-1, keepdims=True) + EPS) * nssm_ref[...]
            nxt_ref[rows, :] = y.astype(BF16)
            return _zero_token(y)

        x1 = x_ref[0] + m[2:3] * _mm(cur_ref[...], wo_ref[...])
        x2 = _mlp_tail(x1, m, nmlp_ref[...], w1_ref, w2_ref, h2_ref,
                       [functools.partial(finish_rows, r) for r in range(TM // FIN_ROWS)])
        o_ref[0] = x2 * lax.rsqrt(jnp.mean(x2 * x2, axis=-1, keepdims=True) + EPS) * nfin_ref[...]

    _two_phase(step, slots_ref)


def _odd_out(xs, modsel, yf, yb, xc, z, dsk, nssm, wo, nmlp, w1, w2, layer, nfin, ctx_len):
    b, t, _ = xs.shape
    skip = ctx_len // TM
    nt = t // TM - skip
    mm, fin = _two_phase_maps(nt, b * nt)
    fin_tok = lambda w: pl.BlockSpec((1, TM, w), lambda i: (fin(i)[0], fin(i)[1] + skip, 0))
    vec = lambda w: _resident((1, w))
    return pl.pallas_call(
        _odd_out_kernel,
        grid=(b * nt + 1,),
        in_specs=[pl.BlockSpec((1, TM, D_MODEL), lambda i: (mm(i)[0], mm(i)[1] + skip, 0)),
                  pl.BlockSpec((1, 1, 6, D_MODEL), lambda i: (mm(i)[0], 1, 0, 0)),
                  fin_tok(SSM_INNER), fin_tok(SSM_INNER), fin_tok(SSM_INNER), fin_tok(SSM_INNER),
                  vec(SSM_INNER), vec(SSM_INNER), _resident(wo.shape), vec(D_MODEL),
                  _resident_layer(w1.shape, layer), _resident_layer(w2.shape, layer), vec(D_MODEL)],
        out_specs=pl.BlockSpec((1, TM, D_MODEL), lambda i: (mm(i)[0], mm(i)[1], 0)),
        out_shape=jax.ShapeDtypeStruct((b, t - ctx_len, D_MODEL), F32),
        scratch_shapes=[pltpu.VMEM((2, TM, SSM_INNER), BF16), pltpu.VMEM((TM, D_MODEL), BF16)],
        compiler_params=_params(1),
        name="odd_out_mlp",
    )(xs, modsel, yf, yb, xc, z, dsk.reshape(1, -1), nssm.reshape(1, -1), wo, nmlp.reshape(1, -1),
      w1, w2, nfin.reshape(1, -1))


def kernel(x, c, ctx, c_ctx, w_mod, b_mod, norm_mix, norm_mlp, w_mlp_in, w_mlp_out, w_in_even, w_out_even,
           ret_decay_logit, na_rpb, w_in_odd, conv_w, conv_b, dt_bias, a_log, d_skip, ssm_norm, w_out_odd,
           norm_final):
    assert w_mod.shape[0] == DEPTH == 2 and x.shape[2] == D_MODEL
    ctx_len = ctx.shape[1]
    seq = x.shape[1]
    assert ctx_len == TM and seq % (2 * GRID_W) == 0

    modsel = _modulation(c, c_ctx, w_mod, b_mod)
    w1 = w_mlp_in.astype(BF16)
    w2 = w_mlp_out.astype(BF16)

    n_tok = 4 * RET_W + 2 * NA_W
    w_even = w_in_even[0].astype(BF16)
    rq, rk, rv, rg, nq, nk, nvt = _even_in(ctx, x, modsel[0], norm_mix[0], w_even, w_even[:, n_tok:].T,
                                           _rope_tables(seq, ctx_len))
    yf, yb = _retention(rq, rk, rv, _ret_tables(ret_decay_logit[0]), ctx_len)
    yna = _na(nq, nk, nvt, _na_bias_tables(na_rpb[0]), ctx_len)
    xs = _even_out(ctx, x, modsel[0], yf, yb, rg, yna, w_out_even[0].astype(BF16), norm_mlp[0], w1, w2, 0)

    n_main = SSM_INNER + SSM_CONV_CH
    wi = w_in_odd[0]
    w_dt = jnp.pad(wi[:, n_main:], ((0, 0), (0, LANES - 2 * SSM_HEADS))).astype(BF16)
    pad_row = lambda v: jnp.pad(v.astype(F32).reshape(1, -1), ((0, 0), (0, LANES - 2 * SSM_HEADS)))
    z, xbc, dt = _odd_in(xs, modsel[1], norm_mix[1], wi[:, :n_main].astype(BF16), w_dt, pad_row(dt_bias[0]))
    xc = _conv(xbc, conv_w[0].astype(F32), conv_b[0].astype(F32), ctx_len)
    yf, yb = _ssd(xc, dt, pad_row(-jnp.exp(a_log[0].astype(F32))), ctx_len)
    dsk = jnp.repeat(d_skip[0].astype(F32), SSM_HEAD_DIM)
    return _odd_out(xs, modsel[1], yf, yb, xc, z, dsk, ssm_norm[0], w_out_odd[0].astype(BF16), norm_mlp[1],
                    w1, w2, 1, norm_final, ctx_len)
```

```python
import functools

import numpy as np
import jax
import jax.numpy as jnp
from jax import lax
from jax.experimental import pallas as pl
from jax.experimental.pallas import tpu as pltpu

F32 = jnp.float32
BF16 = jnp.bfloat16

D_MODEL = 1024
D_FF = 4 * D_MODEL
DEPTH = 2
GRID_W = 64
EPS = 1e-6
ROPE_BASE = 10000.0

RET_HEADS = 4
RET_DIM = 128
RET_W = RET_HEADS * RET_DIM

NA_HEADS = 8
NA_DIM = 64
NA_W = NA_HEADS * NA_DIM
NA_WIN_R = 8
NA_WIN_C = 16

SSM_INNER = 2 * D_MODEL
SSM_HEAD_DIM = 64
SSM_HEADS = SSM_INNER // SSM_HEAD_DIM
SSM_GROUPS = 4
SSM_HPG = SSM_HEADS // SSM_GROUPS
SSM_STATE = 128
SSM_CONV = 7
SSM_GN = SSM_GROUPS * SSM_STATE
SSM_CONV_CH = SSM_INNER + 2 * SSM_GN
SSM_GW = SSM_HPG * SSM_HEAD_DIM

LANES = 128
SUBLANES = 8
TM = 256
CHUNK = 128
SCAN_BLOCK = 2 * CHUNK
FIN_ROWS = 32
HALO = 32
LOG2E = 1.4426950408889634
NEG = -1e30
NA_KROWS = 10
VMEM_LIMIT = 56 * 1024 * 1024


def _params(n_axes):
    return pltpu.CompilerParams(dimension_semantics=("arbitrary",) * n_axes, vmem_limit_bytes=VMEM_LIMIT)


def _resident(shape):
    nd = len(shape)
    return pl.BlockSpec(shape, lambda *_: (0,) * nd, pipeline_mode=pl.Buffered(1))


def _resident_layer(shape, layer):
    nd = len(shape)
    return pl.BlockSpec((None,) + tuple(shape[1:]), lambda *_: (layer,) + (0,) * (nd - 1),
                        pipeline_mode=pl.Buffered(1))


def _nt(a, b):
    return lax.dot_general(a, b, (((1,), (1,)), ((), ())), preferred_element_type=F32)


def _tn(a, b):
    return lax.dot_general(a, b, (((0,), (0,)), ((), ())), preferred_element_type=F32)


def _mm(a, b):
    return jnp.dot(a, b, preferred_element_type=F32)


def _silu(v):
    return v * jax.nn.sigmoid(v)


def _norm_mod(x, g, sc, sh):
    ms = jnp.mean(x * x, axis=-1, keepdims=True)
    return (x * lax.rsqrt(ms + EPS) * g) * (1.0 + sc) + sh


def _mod_kernel(cc_ref, w_ref, b_ref, o_ref):
    s = _silu(cc_ref[...])
    o_ref[0] = _mm(s.astype(BF16), w_ref[0].astype(BF16)) + b_ref[0]


def _modulation(c, c_ctx, w_mod, b_mod):
    b = c.shape[0]
    rows = -(-(b + 1) // 8) * 8
    cc = jnp.zeros((rows, D_MODEL), F32).at[:b].set(c).at[b].set(c_ctx)
    tn = 1536
    out = pl.pallas_call(
        _mod_kernel,
        grid=(DEPTH, 6 * D_MODEL // tn),
        in_specs=[pl.BlockSpec((rows, D_MODEL), lambda l, j: (0, 0)),
                  pl.BlockSpec((1, D_MODEL, tn), lambda l, j: (l, 0, j)),
                  pl.BlockSpec((1, 1, tn), lambda l, j: (l, 0, j))],
        out_specs=pl.BlockSpec((1, rows, tn), lambda l, j: (l, 0, j)),
        out_shape=jax.ShapeDtypeStruct((DEPTH, rows, 6 * D_MODEL), F32),
        compiler_params=_params(2),
        name="modulation",
    )(cc, w_mod, b_mod.reshape(DEPTH, 1, 6 * D_MODEL))
    mod = out.reshape(DEPTH, rows, 6, D_MODEL)
    return jnp.stack([jnp.broadcast_to(mod[:, b:b + 1], (DEPTH, b, 6, D_MODEL)), mod[:, :b]], axis=2)


def _mod_spec():
    return pl.BlockSpec((1, 1, 6, D_MODEL), lambda b, t: (b, jnp.minimum(t, 1), 0, 0))


def _rope_tables(seq, ctx_len):
    pos = np.arange(seq)
    nf = RET_DIM // 4
    inv = (ROPE_BASE ** (-np.arange(nf, dtype=np.float32) / nf)).astype(np.float32)

    def cs(p):
        ang = p.astype(np.float32)[:, None] * inv[None, :]
        return np.cos(ang), np.sin(ang)

    cr, sr = cs(pos // GRID_W)
    cc, sc = cs(pos % GRID_W)
    zero = np.zeros_like(sr)
    cos = np.concatenate([cr, cr, cc, cc], axis=1)
    sin_a = np.concatenate([-sr, zero, -sc, zero], axis=1)
    sin_b = np.concatenate([zero, sr, zero, sc], axis=1)
    ident = np.ones((ctx_len, RET_DIM), np.float32)
    nul = np.zeros((ctx_len, RET_DIM), np.float32)
    return tuple(jnp.asarray(np.concatenate(parts, axis=0), F32)
                 for parts in ([ident, cos], [nul, sin_a], [nul, sin_b]))


def _stream_tile(ctx_ref, x_ref):
    return jnp.where(pl.program_id(1) == 0, ctx_ref[0], x_ref[0])


def _stream_specs():
    return [pl.BlockSpec((1, TM, D_MODEL), lambda bb, tt: (bb, 0, 0)),
            pl.BlockSpec((1, TM, D_MODEL), lambda bb, tt: (bb, jnp.maximum(tt - 1, 0), 0))]


def _even_in_kernel(ctx_ref, x_ref, mod_ref, g_ref, w_ref, wvt_ref, cos_ref, sa_ref, sb_ref,
                    rq_ref, rk_ref, rv_ref, rg_ref, nq_ref, nk_ref, nvt_ref):
    m = mod_ref[0, 0]
    h = _norm_mod(_stream_tile(ctx_ref, x_ref), g_ref[...], m[1:2], m[0:1]).astype(BF16)
    cos, sa, sb = cos_ref[...], sa_ref[...], sb_ref[...]

    def proj(j):
        return _mm(h, w_ref[:, j * RET_W:(j + 1) * RET_W])

    def rope_store(o_ref, r, scale):
        for hh in range(RET_HEADS):
            t = r[:, hh * RET_DIM:(hh + 1) * RET_DIM]
            o = t * cos + pltpu.roll(t, RET_DIM - 32, 1) * sa + pltpu.roll(t, 32, 1) * sb
            o_ref[0, :, hh * RET_DIM:(hh + 1) * RET_DIM] = (o * scale).astype(BF16)

    p = [proj(j) for j in range(6)]
    vt = _nt(wvt_ref[...], h)
    rope_store(rq_ref, p[0], 1.0)
    rope_store(rk_ref, p[1], RET_DIM ** -0.5)
    rv_ref[0] = p[2].astype(BF16)
    rg_ref[0] = _silu(p[3]).astype(BF16)
    nq_ref[0] = (p[4] * (NA_DIM ** -0.5 * LOG2E)).astype(BF16)
    nk_ref[0] = p[5].astype(BF16)
    nvt_ref[0] = vt.astype(BF16)


def _even_in(ctx, x, modsel, g, w, w_vt, tables):
    b = x.shape[0]
    t = ctx.shape[1] + x.shape[1]
    tab = pl.BlockSpec((TM, RET_DIM), lambda bb, tt: (tt, 0))
    out = pl.BlockSpec((1, TM, RET_W), lambda bb, tt: (bb, tt, 0))
    return pl.pallas_call(
        _even_in_kernel,
        grid=(b, t // TM),
        in_specs=_stream_specs() + [_mod_spec(), _resident((1, D_MODEL)), _resident(w.shape),
                                    _resident(w_vt.shape), tab, tab, tab],
        out_specs=[out] * 6 + [pl.BlockSpec((1, NA_W, TM), lambda bb, tt: (bb, 0, tt))],
        out_shape=[jax.ShapeDtypeStruct((b, t, RET_W), BF16)] * 6 + [jax.ShapeDtypeStruct((b, NA_W, t), BF16)],
        compiler_params=_params(2),
        name="even_in",
    )(ctx, x, modsel, g.reshape(1, D_MODEL), w, w_vt, *tables)


def _ret_tables(decay_logit):
    log_g = jax.nn.log_sigmoid(decay_logit.astype(F32))
    idx = jnp.arange(CHUNK, dtype=F32)
    diff = idx[:, None] - idx[None, :]
    ones = jnp.ones((CHUNK, CHUNK), F32)

    def one(lg, d):
        dd = diff if d == 0 else -diff
        dm = jnp.where(dd >= 0, jnp.exp(jnp.maximum(dd, 0.0) * lg), 0.0)
        qe = (idx + 1.0) if d == 0 else (CHUNK - idx)
        ke = (CHUNK - 1.0 - idx) if d == 0 else idx
        return jnp.stack([dm, jnp.exp(qe * lg)[:, None] * ones, jnp.exp(ke * lg)[:, None] * ones,
                          jnp.exp(CHUNK * lg) * ones])

    return jnp.stack([jnp.stack([one(log_g[d, h], d) for h in range(RET_HEADS)]) for d in range(2)])


def _ret_kernel(qf_ref, kf_ref, vf_ref, qb_ref, kb_ref, vb_ref, dec_ref, yf_ref, yb_ref, s_ref):
    @pl.when(pl.program_id(1) == 0)
    def _():
        s_ref[...] = jnp.zeros_like(s_ref)

    refs = ((qf_ref, kf_ref, vf_ref, yf_ref), (qb_ref, kb_ref, vb_ref, yb_ref))
    chains = [(d, hh, slice(hh * RET_DIM, (hh + 1) * RET_DIM)) for d in range(2) for hh in range(RET_HEADS)]
    n_sub = qf_ref.shape[1] // CHUNK
    for sub in range(n_sub):
        rows = [slice(c * CHUNK, (c + 1) * CHUNK) for c in (sub, n_sub - 1 - sub)]
        stage1 = []
        for d, hh, sl in chains:
            q_ref, k_ref, _, _ = refs[d]
            q = q_ref[0, rows[d], sl]
            stage1.append((_nt(q, k_ref[0, rows[d], sl]), _mm(q, s_ref[d, hh].astype(BF16))))
        for (d, hh, sl), (att, inter) in zip(chains, stage1):
            _, _, v_ref, y_ref = refs[d]
            y_ref[0, rows[d], sl] = (_mm((att * dec_ref[d, hh, 0]).astype(BF16), v_ref[0, rows[d], sl])
                                     + inter * dec_ref[d, hh, 1])
        for d, hh, sl in chains:
            _, k_ref, v_ref, _ = refs[d]
            kk = (k_ref[0, rows[d], sl].astype(F32) * dec_ref[d, hh, 2]).astype(BF16)
            s_ref[d, hh] = dec_ref[d, hh, 3] * s_ref[d, hh] + _tn(kk, v_ref[0, rows[d], sl])


def _scan_orders(n_steps, n_ctx):
    fwd = lambda i: i
    bwd = lambda i: jnp.where(i < n_ctx, n_ctx - 1 - i, n_steps + n_ctx - 1 - i)
    return fwd, bwd


def _retention(rq, rk, rv, dec, ctx_len):
    b, t, _ = rq.shape
    ns = t // SCAN_BLOCK
    fwd, bwd = _scan_orders(ns, ctx_len // SCAN_BLOCK)

    def spec(order):
        return pl.BlockSpec((1, SCAN_BLOCK, RET_W), lambda bb, i: (bb, order(i), 0))

    return pl.pallas_call(
        _ret_kernel,
        grid=(b, ns),
        in_specs=[spec(fwd)] * 3 + [spec(bwd)] * 3 + [_resident(dec.shape)],
        out_specs=[spec(fwd), spec(bwd)],
        out_shape=[jax.ShapeDtypeStruct((b, t, RET_W), F32)] * 2,
        scratch_shapes=[pltpu.VMEM((2, RET_HEADS, RET_DIM, RET_DIM), F32)],
        compiler_params=_params(2),
        name="retention",
    )(rq, rk, rv, rq, rk, rv, dec)


def _na_bias_tables(rpb):
    qc = np.arange(GRID_W)[:, None]
    kc = np.arange(GRID_W)[None, :]
    cstart = np.clip(qc - NA_WIN_C // 2, 0, GRID_W - NA_WIN_C)
    inwin = (kc >= cstart) & (kc < cstart + NA_WIN_C)
    dcol = np.clip(kc - qc, -(NA_WIN_C - 1), NA_WIN_C - 1) + (NA_WIN_C - 1)
    onehot = np.zeros((GRID_W, GRID_W, 2 * NA_WIN_C - 1), np.float32)
    onehot[qc, kc, dcol] = 1.0
    t = jnp.einsum("hrd,qkd->hrkq", rpb.astype(F32), jnp.asarray(onehot), precision=lax.Precision.HIGHEST)
    t = jnp.where(jnp.asarray(inwin.T), t * LOG2E, NEG)
    pad = jnp.full((NA_HEADS, 2, GRID_W, GRID_W), NEG, F32)
    tp = jnp.concatenate([pad, t, pad], axis=1)
    n = 2 * NA_WIN_R + 1
    top = jnp.concatenate([tp[:, 1:1 + n], tp[:, 0:n]], axis=-1)
    bot = jnp.concatenate([tp[:, 2:2 + n], tp[:, 1:1 + n]], axis=-1)
    return jnp.concatenate([top, bot], axis=-2)


def _na_kernel(q_ref, k_ref, vt_ref, tbl_ref, o_ref, *, rows, ctx_len):
    s = pl.program_id(1)
    n_ctx = ctx_len // SCAN_BLOCK
    n_unit = SCAN_BLOCK // CHUNK
    left = lax.broadcasted_iota(jnp.int32, (1, LANES), 1) < NA_DIM
    top_half = lax.broadcasted_iota(jnp.int32, (CHUNK, LANES), 0) < GRID_W
    left_half = lax.broadcasted_iota(jnp.int32, (CHUNK, LANES), 1) < GRID_W
    n_loc = NA_KROWS * GRID_W
    n_blk = NA_KROWS // 2

    def attend(local_of_unit):
        items = [(u, hp, slice(u * CHUNK, (u + 1) * CHUNK), slice(hp * LANES, (hp + 1) * LANES))
                 for u in range(n_unit) for hp in range(NA_HEADS // 2)]
        scores = []
        for u, hp, qrows, sl in items:
            qp = q_ref[0, qrows, sl]
            zero = jnp.zeros_like(qp)
            qboth = jnp.concatenate([jnp.where(left, qp, zero), jnp.where(left, zero, qp)], axis=0)
            s_ctx = _nt(k_ref[0, 0:ctx_len, sl], qboth)
            s_loc = None
            if local_of_unit[u] is not None:
                off, idx, pen = local_of_unit[u]
                s_loc = _nt(k_ref[0, pl.ds(off, n_loc), sl], qboth)
                s_loc = s_loc + jnp.concatenate(
                    [jnp.concatenate([tbl_ref[2 * hp, idx[m]] + pen[m], tbl_ref[2 * hp + 1, idx[m]] + pen[m]],
                                     axis=1) for m in range(n_blk)], axis=0)
            scores.append((s_ctx, s_loc))
        probs = []
        for s_ctx, s_loc in scores:
            mx = jnp.max(s_ctx, axis=0, keepdims=True)
            if s_loc is not None:
                mx = jnp.maximum(mx, jnp.max(s_loc, axis=0, keepdims=True))
            p_ctx = jnp.exp2(s_ctx - mx)
            den = jnp.sum(p_ctx, axis=0, keepdims=True)
            p_loc = None
            if s_loc is not None:
                p_loc = jnp.exp2(s_loc - mx)
                den = den + jnp.sum(p_loc, axis=0, keepdims=True)
                p_loc = p_loc.astype(BF16)
            probs.append((p_ctx.astype(BF16), p_loc, den))
        for (u, hp, qrows, sl), (p_ctx, p_loc, den) in zip(items, probs):
            ot = _mm(vt_ref[0, sl, 0:ctx_len], p_ctx)
            if p_loc is not None:
                ot = ot + _mm(vt_ref[0, sl, pl.ds(local_of_unit[u][0], n_loc)], p_loc)
            ot = ot / den
            pair_t = jnp.concatenate([ot[0:NA_DIM, 0:CHUNK], ot[NA_DIM:, CHUNK:]], axis=0)
            o_ref[0, qrows, sl] = pair_t.T.astype(BF16)

    @pl.when(s < n_ctx)
    def _():
        attend([None] * n_unit)

    @pl.when(s >= n_ctx)
    def _():
        def row_pen(kr, r):
            rs = jnp.clip(r - NA_WIN_R // 2, 0, rows - NA_WIN_R)
            return jnp.where(jnp.logical_and(kr >= rs, kr < rs + NA_WIN_R), 0.0, NEG)

        local_of_unit = []
        for u in range(n_unit):
            r0 = 2 * ((s - n_ctx) * n_unit + u)
            kb = jnp.clip(r0 - NA_WIN_R // 2, 0, rows - NA_KROWS)
            off = pl.multiple_of(ctx_len + kb * GRID_W, LANES)
            idx, pen = [], []
            for m in range(n_blk):
                kp = kb + 2 * m
                idx.append(kp - r0 + NA_WIN_R)
                pen.append(jnp.where(top_half,
                                     jnp.where(left_half, row_pen(kp, r0), row_pen(kp, r0 + 1)),
                                     jnp.where(left_half, row_pen(kp + 1, r0), row_pen(kp + 1, r0 + 1))))
            local_of_unit.append((off, idx, pen))
        attend(local_of_unit)


def _na(nq, nk, nvt, tbl, ctx_len):
    b, t, _ = nq.shape
    rows = (t - ctx_len) // GRID_W
    assert rows >= NA_KROWS and CHUNK == 2 * GRID_W == LANES
    assert (t - ctx_len) % SCAN_BLOCK == 0 and ctx_len % SCAN_BLOCK == 0
    blk = pl.BlockSpec((1, SCAN_BLOCK, NA_W), lambda bb, s: (bb, s, 0))
    return pl.pallas_call(
        functools.partial(_na_kernel, rows=rows, ctx_len=ctx_len),
        grid=(b, t // SCAN_BLOCK),
        in_specs=[blk, pl.BlockSpec((1, t, NA_W), lambda bb, s: (bb, 0, 0)),
                  pl.BlockSpec((1, NA_W, t), lambda bb, s: (bb, 0, 0)), _resident(tbl.shape)],
        out_specs=blk,
        out_shape=jax.ShapeDtypeStruct((b, t, NA_W), BF16),
        compiler_params=_params(2),
        name="na",
    )(nq, nk, nvt, tbl)


def _zero_token(v):
    bits = lax.bitcast_convert_type(v[0:SUBLANES, 0:LANES], jnp.int32)
    return lax.shift_right_logical(lax.shift_right_logical(bits, 16), 16)


def _mlp_tail(x1, m, nmlp, w1_ref, w2_ref, h2_ref, fillers=()):
    fillers = list(fillers)

    def fill():
        if fillers:
            token = fillers.pop(0)()
            head = pltpu.bitcast(h2_ref[0:2 * SUBLANES, 0:LANES], jnp.int32)
            h2_ref[0:2 * SUBLANES, 0:LANES] = pltpu.bitcast(head + token, BF16)

    h2_ref[...] = _norm_mod(x1, nmlp, m[4:5], m[3:4]).astype(BF16)
    fc = D_MODEL
    n_stage = D_FF // fc
    acc = jnp.zeros_like(x1)
    for j in range(n_stage):
        while len(fillers) > 2 * (n_stage - 1 - j):
            fill()
        u = _mm(h2_ref[...], w1_ref[:, j * fc:(j + 1) * fc])
        fill()
        u = jnp.square(jnp.maximum(u, 0.0)).astype(BF16)
        acc = acc + _mm(u, w2_ref[j * fc:(j + 1) * fc, :])
        fill()
    return x1 + m[5:6] * acc


def _two_phase(step_fn, slots_ref):
    i = pl.program_id(0)

    @pl.when(i == 0)
    def _():
        slots_ref[...] = jnp.zeros_like(slots_ref)

    for parity in range(2):
        @pl.when(i % 2 == parity)
        def _():
            step_fn(slots_ref.at[1 - parity], slots_ref.at[parity])


def _two_phase_maps(n_tiles, n_total):
    def mm(i):
        j = jnp.maximum(i - 1, 0)
        return j // n_tiles, j % n_tiles

    def fin(i):
        k = jnp.minimum(i, n_total - 1)
        return k // n_tiles, k % n_tiles

    return mm, fin


def _even_out_kernel(ctx_ref, x_ref, mod_ref, yna_ref, yf_ref, yb_ref, rg_ref, wo_ref, nmlp_ref, w1_ref, w2_ref,
                     o_ref, slots_ref, h2_ref, *, n_tiles):
    def step(cur_ref, nxt_ref):
        m = mod_ref[0, 0]
        is_ctx = (jnp.maximum(pl.program_id(0) - 1, 0) % n_tiles) == 0
        def finish_head(hh):
            sl = slice(hh * RET_DIM, (hh + 1) * RET_DIM)
            yh = yf_ref[0, :, sl] + yb_ref[0, :, sl]
            dlt = yh - jnp.mean(yh, axis=-1, keepdims=True)
            yn = dlt * lax.rsqrt(jnp.mean(dlt * dlt, axis=-1, keepdims=True) + EPS)
            nxt_ref[:, sl] = (yn * rg_ref[0, :, sl].astype(F32)).astype(BF16)
            return _zero_token(yn)

        mix = _mm(yna_ref[0], wo_ref[RET_W:RET_W + NA_W, :]) + _mm(cur_ref[...], wo_ref[0:RET_W, :])
        x1 = jnp.where(is_ctx, ctx_ref[0], x_ref[0]) + m[2:3] * mix
        o_ref[0] = _mlp_tail(x1, m, nmlp_ref[...], w1_ref, w2_ref, h2_ref,
                             [functools.partial(finish_head, hh) for hh in range(RET_HEADS)])

    _two_phase(step, slots_ref)


def _even_out(ctx, x, modsel, yf, yb, rg, yna, wo, nmlp, w1, w2, layer):
    b = x.shape[0]
    t = ctx.shape[1] + x.shape[1]
    nt = t // TM
    mm, fin = _two_phase_maps(nt, b * nt)

    def mm_tok(w):
        return pl.BlockSpec((1, TM, w), lambda i: (mm(i)[0], mm(i)[1], 0))

    def fin_tok(w):
        return pl.BlockSpec((1, TM, w), lambda i: (fin(i)[0], fin(i)[1], 0))

    return pl.pallas_call(
        functools.partial(_even_out_kernel, n_tiles=nt),
        grid=(b * nt + 1,),
        in_specs=[pl.BlockSpec((1, TM, D_MODEL), lambda i: (mm(i)[0], 0, 0)),
                  pl.BlockSpec((1, TM, D_MODEL), lambda i: (mm(i)[0], jnp.maximum(mm(i)[1] - 1, 0), 0)),
                  pl.BlockSpec((1, 1, 6, D_MODEL), lambda i: (mm(i)[0], jnp.minimum(mm(i)[1], 1), 0, 0)),
                  mm_tok(NA_W), fin_tok(RET_W), fin_tok(RET_W), fin_tok(RET_W),
                  _resident(wo.shape), _resident((1, D_MODEL)), _resident_layer(w1.shape, layer),
                  _resident_layer(w2.shape, layer)],
        out_specs=mm_tok(D_MODEL),
        out_shape=jax.ShapeDtypeStruct((b, t, D_MODEL), F32),
        scratch_shapes=[pltpu.VMEM((2, TM, RET_W), BF16), pltpu.VMEM((TM, D_MODEL), BF16)],
        compiler_params=_params(1),
        name="even_out_mlp",
    )(ctx, x, modsel, yna, yf, yb, rg, wo, nmlp.reshape(1, D_MODEL), w1, w2)


def _conv_perm():
    span = TM // SUBLANES
    rho = np.arange(TM)
    p = np.zeros((TM, TM), np.float32)
    p[rho, (rho % SUBLANES) * span + rho // SUBLANES] = 1.0
    return jnp.asarray(p, BF16)


def _odd_in_kernel(x_ref, mod_ref, g_ref, w_ref, wdt_ref, dtb_ref, perm_ref, z_ref, xbc_ref, dt_ref):
    m = mod_ref[0, 0]
    h = _norm_mod(x_ref[0], g_ref[...], m[1:2], m[0:1]).astype(BF16)
    cw = 512
    for j in range(SSM_INNER // cw):
        z_ref[0, :, j * cw:(j + 1) * cw] = _silu(_mm(h, w_ref[:, j * cw:(j + 1) * cw])).astype(BF16)
    hp = _mm(perm_ref[...], h).astype(BF16)
    for j in range(SSM_CONV_CH // cw):
        c0 = SSM_INNER + j * cw
        xbc_ref[0, :, j * cw:(j + 1) * cw] = _mm(hp, w_ref[:, c0:c0 + cw]).astype(BF16)
    raw = _mm(h, wdt_ref[...]) + dtb_ref[...]
    dt_ref[0] = jnp.maximum(raw, 0.0) + jnp.log1p(jnp.exp(-jnp.abs(raw)))


def _odd_in(xs, modsel, g, w_main, w_dt, dt_bias):
    b, t, _ = xs.shape
    tok = lambda w: pl.BlockSpec((1, TM, w), lambda bb, tt: (bb, tt, 0))
    return pl.pallas_call(
        _odd_in_kernel,
        grid=(b, t // TM),
        in_specs=[tok(D_MODEL), _mod_spec(), _resident((1, D_MODEL)), _resident(w_main.shape),
                  _resident(w_dt.shape), _resident((1, LANES)), _resident((TM, TM))],
        out_specs=[tok(SSM_INNER), tok(SSM_CONV_CH), tok(LANES)],
        out_shape=[jax.ShapeDtypeStruct((b, t, SSM_INNER), BF16),
                   jax.ShapeDtypeStruct((b, t, SSM_CONV_CH), BF16),
                   jax.ShapeDtypeStruct((b, t, LANES), F32)],
        compiler_params=_params(2),
        name="odd_in",
    )(xs, modsel, g.reshape(1, D_MODEL), w_main, w_dt, dt_bias, _conv_perm())


def _conv_kernel(main_ref, prev_ref, next_ref, w_ref, b_ref, unperm_ref, o_ref, act_ref, *, n_tiles):
    t = pl.program_id(1)
    has_prev = (t >= 2).astype(F32)
    has_next = jnp.logical_and(t >= 1, t < n_tiles - 1).astype(F32)
    span = TM // SUBLANES
    half = SSM_CONV // 2
    sub = lax.broadcasted_iota(jnp.int32, (SUBLANES, LANES), 0)

    def lane_tile(j, carry):
        lanes = pl.ds(pl.multiple_of(j * LANES, LANES), LANES)
        xm = main_ref[0, :, lanes].astype(F32)
        pv = prev_ref[0, :, lanes].astype(F32) * has_prev
        nx = next_ref[0, :, lanes].astype(F32) * has_next
        reg = {m: xm[m * SUBLANES:(m + 1) * SUBLANES] for m in range(span)}
        for i in range(1, half + 1):
            row = HALO - 1 - (i - 1) * SUBLANES
            edge = jnp.broadcast_to(pv[row:row + 1], (SUBLANES, LANES))
            reg[-i] = jnp.where(sub == 0, edge, pltpu.roll(reg[span - i], 1, 0))
            row = (i - 1) * SUBLANES
            edge = jnp.broadcast_to(nx[row:row + 1], (SUBLANES, LANES))
            reg[span + i - 1] = jnp.where(sub == SUBLANES - 1, edge, pltpu.roll(reg[i - 1], SUBLANES - 1, 0))
        wk = [jnp.broadcast_to(w_ref[k:k + 1, lanes], (SUBLANES, LANES)) for k in range(SSM_CONV)]
        bias = jnp.broadcast_to(b_ref[:, lanes], (SUBLANES, LANES))
        acts = []
        for m in range(span):
            acc = bias
            for k in range(SSM_CONV):
                acc = acc + wk[k] * reg[m + k - half]
            acts.append(_silu(acc))
        act_ref[:, lanes] = jnp.concatenate(acts, axis=0).astype(BF16)
        return carry

    lax.fori_loop(0, SSM_CONV_CH // LANES, lane_tile, 0, unroll=2)
    o_ref[0] = _mm(unperm_ref[...], act_ref[...]).astype(BF16)


def _conv(xbc, w, bias, ctx_len):
    b, t, ch = xbc.shape
    assert ctx_len == TM and TM % HALO == 0 and HALO >= (SSM_CONV // 2) * SUBLANES
    nt = t // TM
    per = TM // HALO
    main = pl.BlockSpec((1, TM, ch), lambda bb, tt: (bb, tt, 0))
    prev = pl.BlockSpec((1, HALO, ch), lambda bb, tt: (bb, jnp.maximum(tt * per - 1, 0), 0))
    nxt = pl.BlockSpec((1, HALO, ch), lambda bb, tt: (bb, jnp.minimum((tt + 1) * per, nt * per - 1), 0))
    return pl.pallas_call(
        functools.partial(_conv_kernel, n_tiles=nt),
        grid=(b, nt),
        in_specs=[main, prev, nxt, _resident(w.shape), _resident((1, ch)), _resident((TM, TM))],
        out_specs=main,
        out_shape=jax.ShapeDtypeStruct((b, t, ch), BF16),
        scratch_shapes=[pltpu.VMEM((TM, ch), BF16)],
        compiler_params=_params(2),
        name="dwconv_silu",
    )(xbc, xbc, xbc, w, bias.reshape(1, ch), _conv_perm().T)


def _dot_exact_lhs(a16, x):
    hi = x.astype(BF16)
    r1 = x - hi.astype(F32)
    mid = r1.astype(BF16)
    lo = (r1 - mid.astype(F32)).astype(BF16)
    return _mm(a16, hi) + _mm(a16, mid) + _mm(a16, lo)


def _ssd_kernel(xf_ref, xb_ref, dtf_ref, dtb_ref, a_ref, tri_ref, yf_ref, yb_ref, h_ref, row_ref):
    @pl.when(pl.program_id(1) == 0)
    def _():
        h_ref[...] = jnp.zeros_like(h_ref)

    left = lax.broadcasted_iota(jnp.int32, (1, LANES), 1) < SSM_HEAD_DIM
    keep_l = jnp.where(left, jnp.uint32(0xFFFFFFFF), jnp.uint32(0))
    keep_r = jnp.where(left, jnp.uint32(0), jnp.uint32(0xFFFFFFFF))
    ii = lax.broadcasted_iota(jnp.int32, (CHUNK, CHUNK), 0)
    jj = lax.broadcasted_iota(jnp.int32, (CHUNK, CHUNK), 1)
    b_off = SSM_INNER
    c_off = SSM_INNER + SSM_GN

    directions = ((xf_ref, dtf_ref, yf_ref), (xb_ref, dtb_ref, yb_ref))
    n_sub = xf_ref.shape[1] // CHUNK
    rows_of = [[slice(c * CHUNK, (c + 1) * CHUNK) for c in (sub, n_sub - 1 - sub)] for sub in range(n_sub)]

    decays = {}
    for sub in range(n_sub):
        for d, (_, dt_ref, _) in enumerate(directions):
            dtc = dt_ref[0, rows_of[sub][d], :]
            acum = _dot_exact_lhs(tri_ref[d], dtc * a_ref[...]) * LOG2E
            acum_t = acum.T
            dt_t = dtc.T
            last = acum[CHUNK - 1:CHUNK, :] if d == 0 else acum[0:1, :]
            last_t = acum_t[:, CHUNK - 1:CHUNK] if d == 0 else acum_t[:, 0:1]
            row_ref[sub, d, 0] = acum_t - jnp.log2(dt_t)
            row_ref[sub, d, 1] = dt_t * jnp.exp2(last_t - acum_t)
            decays[sub, d] = (acum, jnp.exp2(last))

    for sub in range(n_sub):
        for d, (x_ref, _, y_ref) in enumerate(directions):
            rows = rows_of[sub][d]
            acum, elast = decays[sub, d]
            mask = (ii >= jj) if d == 0 else (jj >= ii)

            for g in range(SSM_GROUPS):
                bg = x_ref[0, rows, b_off + g * SSM_STATE:b_off + (g + 1) * SSM_STATE]
                cg = x_ref[0, rows, c_off + g * SSM_STATE:c_off + (g + 1) * SSM_STATE]
                cb = _nt(cg, bg)
                bg_t = bg.astype(F32).T
                h_t = h_ref[d, g]
                y_int = _mm(cg, h_t.astype(BF16))
                for hp in range(SSM_HPG // 2):
                    c0 = d * SSM_HEADS + g * SSM_HPG + 2 * hp
                    ch = slice(g * SSM_GW + hp * LANES, g * SSM_GW + (hp + 1) * LANES)
                    hl = slice(hp * LANES, (hp + 1) * LANES)
                    xu = pltpu.bitcast(x_ref[0, rows, ch], jnp.uint32)
                    xbd = jnp.concatenate([pltpu.bitcast(xu & keep_l, BF16), pltpu.bitcast(xu & keep_r, BF16)],
                                          axis=0)
                    intra, upd, eac = [], [], []
                    for e in range(2):
                        c = c0 + e
                        acol = jnp.broadcast_to(acum[:, c:c + 1], (CHUNK, CHUNK))
                        lmat = jnp.exp2(jnp.where(mask, acol - row_ref[sub, d, 0, c:c + 1, :], NEG))
                        intra.append((lmat * cb).astype(BF16))
                        upd.append((bg_t * row_ref[sub, d, 1, c:c + 1, :]).astype(BF16))
                        eac.append(jnp.exp2(acol))
                    lhs = jnp.concatenate([jnp.concatenate(intra, axis=1), jnp.concatenate(upd, axis=1)], axis=0)
                    r = _mm(lhs, xbd)
                    y_ref[0, rows, ch] = r[:CHUNK] + y_int[:, hl] * jnp.where(left, eac[0], eac[1])
                    el = jnp.where(left, jnp.broadcast_to(elast[:, c0:c0 + 1], (1, LANES)),
                                   jnp.broadcast_to(elast[:, c0 + 1:c0 + 2], (1, LANES)))
                    h_ref[d, g, :, hl] = el * h_t[:, hl] + r[CHUNK:]


def _ssd(xc, dt, a_row, ctx_len):
    b, t, ch = xc.shape
    ns = t // SCAN_BLOCK
    fwd, bwd = _scan_orders(ns, ctx_len // SCAN_BLOCK)
    idx = np.arange(CHUNK)
    tri = jnp.asarray(np.stack([idx[:, None] >= idx[None, :], idx[:, None] <= idx[None, :]]), BF16)

    def spec(order, w):
        return pl.BlockSpec((1, SCAN_BLOCK, w), lambda bb, i: (bb, order(i), 0))

    return pl.pallas_call(
        _ssd_kernel,
        grid=(b, ns),
        in_specs=[spec(fwd, ch), spec(bwd, ch), spec(fwd, LANES), spec(bwd, LANES),
                  _resident((1, LANES)), _resident((2, CHUNK, CHUNK))],
        out_specs=[spec(fwd, SSM_INNER), spec(bwd, SSM_INNER)],
        out_shape=[jax.ShapeDtypeStruct((b, t, SSM_INNER), F32)] * 2,
        scratch_shapes=[pltpu.VMEM((2, SSM_GROUPS, SSM_STATE, SSM_GW), F32),
                        pltpu.VMEM((SCAN_BLOCK // CHUNK, 2, 2, CHUNK, CHUNK), F32)],
        compiler_params=_params(2),
        name="ssd",
    )(xc, xc, dt, dt, a_row, tri)


def _odd_out_kernel(x_ref, mod_ref, yf_ref, yb_ref, xc_ref, z_ref, dsk_ref, nssm_ref, wo_ref,
                    nmlp_ref, w1_ref, w2_ref, nfin_ref, o_ref, slots_ref, h2_ref):
    def step(cur_ref, nxt_ref):
        m = mod_ref[0, 0]
        def finish_rows(r):
            rows = slice(r * FIN_ROWS, (r + 1) * FIN_ROWS)
            y = yf_ref[0, rows, :] + yb_ref[0, rows, :] + dsk_ref[...] * xc_ref[0, rows, :].astype(F32)
            y = y * z_ref[0, rows, :].astype(F32)
            y = y * lax.rsqrt(jnp.mean(y * y, axis=-1, keepdims=True) + EPS) * nssm_ref[...]
            nxt_ref[rows, :] = y.astype(BF16)
            return _zero_token(y)

        x1 = x_ref[0] + m[2:3] * _mm(cur_ref[...], wo_ref[...])
        x2 = _mlp_tail(x1, m, nmlp_ref[...], w1_ref, w2_ref, h2_ref,
                       [functools.partial(finish_rows, r) for r in range(TM // FIN_ROWS)])
        o_ref[0] = x2 * lax.rsqrt(jnp.mean(x2 * x2, axis=-1, keepdims=True) + EPS) * nfin_ref[...]

    _two_phase(step, slots_ref)


def _odd_out(xs, modsel, yf, yb, xc, z, dsk, nssm, wo, nmlp, w1, w2, layer, nfin, ctx_len):
    b, t, _ = xs.shape
    skip = ctx_len // TM
    nt = t // TM - skip
    mm, fin = _two_phase_maps(nt, b * nt)
    fin_tok = lambda w: pl.BlockSpec((1, TM, w), lambda i: (fin(i)[0], fin(i)[1] + skip, 0))
    vec = lambda w: _resident((1, w))
    return pl.pallas_call(
        _odd_out_kernel,
        grid=(b * nt + 1,),
        in_specs=[pl.BlockSpec((1, TM, D_MODEL), lambda i: (mm(i)[0], mm(i)[1] + skip, 0)),
                  pl.BlockSpec((1, 1, 6, D_MODEL), lambda i: (mm(i)[0], 1, 0, 0)),
                  fin_tok(SSM_INNER), fin_tok(SSM_INNER), fin_tok(SSM_INNER), fin_tok(SSM_INNER),
                  vec(SSM_INNER), vec(SSM_INNER), _resident(wo.shape), vec(D_MODEL),
                  _resident_layer(w1.shape, layer), _resident_layer(w2.shape, layer), vec(D_MODEL)],
        out_specs=pl.BlockSpec((1, TM, D_MODEL), lambda i: (mm(i)[0], mm(i)[1], 0)),
        out_shape=jax.ShapeDtypeStruct((b, t - ctx_len, D_MODEL), F32),
        scratch_shapes=[pltpu.VMEM((2, TM, SSM_INNER), BF16), pltpu.VMEM((TM, D_MODEL), BF16)],
        compiler_params=_params(1),
        name="odd_out_mlp",
    )(xs, modsel, yf, yb, xc, z, dsk.reshape(1, -1), nssm.reshape(1, -1), wo, nmlp.reshape(1, -1),
      w1, w2, nfin.reshape(1, -1))


def kernel(x, c, ctx, c_ctx, w_mod, b_mod, norm_mix, norm_mlp, w_mlp_in, w_mlp_out, w_in_even, w_out_even,
           ret_decay_logit, na_rpb, w_in_odd, conv_w, conv_b, dt_bias, a_log, d_skip, ssm_norm, w_out_odd,
           norm_final):
    assert w_mod.shape[0] == DEPTH == 2 and x.shape[2] == D_MODEL
    ctx_len = ctx.shape[1]
    seq = x.shape[1]
    assert ctx_len == TM and seq % (2 * GRID_W) == 0

    modsel = _modulation(c, c_ctx, w_mod, b_mod)
    w1 = w_mlp_in.astype(BF16)
    w2 = w_mlp_out.astype(BF16)

    n_tok = 4 * RET_W + 2 * NA_W
    w_even = w_in_even[0].astype(BF16)
    rq, rk, rv, rg, nq, nk, nvt = _even_in(ctx, x, modsel[0], norm_mix[0], w_even, w_even[:, n_tok:].T,
                                           _rope_tables(seq, ctx_len))
    yf, yb = _retention(rq, rk, rv, _ret_tables(ret_decay_logit[0]), ctx_len)
    yna = _na(nq, nk, nvt, _na_bias_tables(na_rpb[0]), ctx_len)
    xs = _even_out(ctx, x, modsel[0], yf, yb, rg, yna, w_out_even[0].astype(BF16), norm_mlp[0], w1, w2, 0)

    n_main = SSM_INNER + SSM_CONV_CH
    wi = w_in_odd[0]
    w_dt = jnp.pad(wi[:, n_main:], ((0, 0), (0, LANES - 2 * SSM_HEADS))).astype(BF16)
    pad_row = lambda v: jnp.pad(v.astype(F32).reshape(1, -1), ((0, 0), (0, LANES - 2 * SSM_HEADS)))
    z, xbc, dt = _odd_in(xs, modsel[1], norm_mix[1], wi[:, :n_main].astype(BF16), w_dt, pad_row(dt_bias[0]))
    xc = _conv(xbc, conv_w[0].astype(F32), conv_b[0].astype(F32), ctx_len)
    yf, yb = _ssd(xc, dt, pad_row(-jnp.exp(a_log[0].astype(F32))), ctx_len)
    dsk = jnp.repeat(d_skip[0].astype(F32), SSM_HEAD_DIM)
    return _odd_out(xs, modsel[1], yf, yb, xc, z, dsk, ssm_norm[0], w_out_odd[0].astype(BF16), norm_mlp[1],
                    w1, w2, 1, norm_final, ctx_len)
```

```python
import functools

import numpy as np
import jax
import jax.numpy as jnp
from jax import lax
from jax.experimental import pallas as pl
from jax.experimental.pallas import tpu as pltpu

F32 = jnp.float32
BF16 = jnp.bfloat16

D_MODEL = 1024
D_FF = 4 * D_MODEL
DEPTH = 2
GRID_W = 64
EPS = 1e-6
ROPE_BASE = 10000.0

RET_HEADS = 4
RET_DIM = 128
RET_W = RET_HEADS * RET_DIM

NA_HEADS = 8
NA_DIM = 64
NA_W = NA_HEADS * NA_DIM
NA_WIN_R = 8
NA_WIN_C = 16

SSM_INNER = 2 * D_MODEL
SSM_HEAD_DIM = 64
SSM_HEADS = SSM_INNER // SSM_HEAD_DIM
SSM_GROUPS = 4
SSM_HPG = SSM_HEADS // SSM_GROUPS
SSM_STATE = 128
SSM_CONV = 7
SSM_GN = SSM_GROUPS * SSM_STATE
SSM_CONV_CH = SSM_INNER + 2 * SSM_GN
SSM_GW = SSM_HPG * SSM_HEAD_DIM

LANES = 128
SUBLANES = 8
TM = 256
CHUNK = 128
SCAN_BLOCK = 2 * CHUNK
FIN_ROWS = 32
HALO = 32
LOG2E = 1.4426950408889634
NEG = -1e30
NA_KROWS = 10
VMEM_LIMIT = 56 * 1024 * 1024


def _params(n_axes):
    return pltpu.CompilerParams(dimension_semantics=("arbitrary",) * n_axes, vmem_limit_bytes=VMEM_LIMIT)


def _resident(shape):
    nd = len(shape)
    return pl.BlockSpec(shape, lambda *_: (0,) * nd, pipeline_mode=pl.Buffered(1))


def _resident_layer(shape, layer):
    nd = len(shape)
    return pl.BlockSpec((None,) + tuple(shape[1:]), lambda *_: (layer,) + (0,) * (nd - 1),
                        pipeline_mode=pl.Buffered(1))


def _nt(a, b):
    return lax.dot_general(a, b, (((1,), (1,)), ((), ())), preferred_element_type=F32)


def _tn(a, b):
    return lax.dot_general(a, b, (((0,), (0,)), ((), ())), preferred_element_type=F32)


def _mm(a, b):
    return jnp.dot(a, b, preferred_element_type=F32)


def _silu(v):
    return v * jax.nn.sigmoid(v)


def _norm_mod(x, g, sc, sh):
    ms = jnp.mean(x * x, axis=-1, keepdims=True)
    return (x * lax.rsqrt(ms + EPS) * g) * (1.0 + sc) + sh


def _mod_kernel(cc_ref, w_ref, b_ref, o_ref):
    s = _silu(cc_ref[...])
    o_ref[0] = _mm(s.astype(BF16), w_ref[0].astype(BF16)) + b_ref[0]


def _modulation(c, c_ctx, w_mod, b_mod):
    b = c.shape[0]
    rows = -(-(b + 1) // 8) * 8
    cc = jnp.zeros((rows, D_MODEL), F32).at[:b].set(c).at[b].set(c_ctx)
    tn = 1536
    out = pl.pallas_call(
        _mod_kernel,
        grid=(DEPTH, 6 * D_MODEL // tn),
        in_specs=[pl.BlockSpec((rows, D_MODEL), lambda l, j: (0, 0)),
                  pl.BlockSpec((1, D_MODEL, tn), lambda l, j: (l, 0, j)),
                  pl.BlockSpec((1, 1, tn), lambda l, j: (l, 0, j))],
        out_specs=pl.BlockSpec((1, rows, tn), lambda l, j: (l, 0, j)),
        out_shape=jax.ShapeDtypeStruct((DEPTH, rows, 6 * D_MODEL), F32),
        compiler_params=_params(2),
        name="modulation",
    )(cc, w_mod, b_mod.reshape(DEPTH, 1, 6 * D_MODEL))
    mod = out.reshape(DEPTH, rows, 6, D_MODEL)
    return jnp.stack([jnp.broadcast_to(mod[:, b:b + 1], (DEPTH, b, 6, D_MODEL)), mod[:, :b]], axis=2)


def _mod_spec():
    return pl.BlockSpec((1, 1, 6, D_MODEL), lambda b, t: (b, jnp.minimum(t, 1), 0, 0))


def _rope_tables(seq, ctx_len):
    pos = np.arange(seq)
    nf = RET_DIM // 4
    inv = (ROPE_BASE ** (-np.arange(nf, dtype=np.float32) / nf)).astype(np.float32)

    def cs(p):
        ang = p.astype(np.float32)[:, None] * inv[None, :]
        return np.cos(ang), np.sin(ang)

    cr, sr = cs(pos // GRID_W)
    cc, sc = cs(pos % GRID_W)
    zero = np.zeros_like(sr)
    cos = np.concatenate([cr, cr, cc, cc], axis=1)
    sin_a = np.concatenate([-sr, zero, -sc, zero], axis=1)
    sin_b = np.concatenate([zero, sr, zero, sc], axis=1)
    ident = np.ones((ctx_len, RET_DIM), np.float32)
    nul = np.zeros((ctx_len, RET_DIM), np.float32)
    return tuple(jnp.asarray(np.concatenate(parts, axis=0), F32)
                 for parts in ([ident, cos], [nul, sin_a], [nul, sin_b]))


def _stream_tile(ctx_ref, x_ref):
    return jnp.where(pl.program_id(1) == 0, ctx_ref[0], x_ref[0])


def _stream_specs():
    return [pl.BlockSpec((1, TM, D_MODEL), lambda bb, tt: (bb, 0, 0)),
            pl.BlockSpec((1, TM, D_MODEL), lambda bb, tt: (bb, jnp.maximum(tt - 1, 0), 0))]


def _even_in_kernel(ctx_ref, x_ref, mod_ref, g_ref, w_ref, wvt_ref, cos_ref, sa_ref, sb_ref,
                    rq_ref, rk_ref, rv_ref, rg_ref, nq_ref, nk_ref, nvt_ref):
    m = mod_ref[0, 0]
    h = _norm_mod(_stream_tile(ctx_ref, x_ref), g_ref[...], m[1:2], m[0:1]).astype(BF16)
    cos, sa, sb = cos_ref[...], sa_ref[...], sb_ref[...]

    def proj(j):
        return _mm(h, w_ref[:, j * RET_W:(j + 1) * RET_W])

    def rope_store(o_ref, r, scale):
        for hh in range(RET_HEADS):
            t = r[:, hh * RET_DIM:(hh + 1) * RET_DIM]
            o = t * cos + pltpu.roll(t, RET_DIM - 32, 1) * sa + pltpu.roll(t, 32, 1) * sb
            o_ref[0, :, hh * RET_DIM:(hh + 1) * RET_DIM] = (o * scale).astype(BF16)

    p = [proj(j) for j in range(6)]
    vt = _nt(wvt_ref[...], h)
    rope_store(rq_ref, p[0], 1.0)
    rope_store(rk_ref, p[1], RET_DIM ** -0.5)
    rv_ref[0] = p[2].astype(BF16)
    rg_ref[0] = _silu(p[3]).astype(BF16)
    nq_ref[0] = (p[4] * (NA_DIM ** -0.5 * LOG2E)).astype(BF16)
    nk_ref[0] = p[5].astype(BF16)
    nvt_ref[0] = vt.astype(BF16)


def _even_in(ctx, x, modsel, g, w, w_vt, tables):
    b = x.shape[0]
    t = ctx.shape[1] + x.shape[1]
    tab = pl.BlockSpec((TM, RET_DIM), lambda bb, tt: (tt, 0))
    out = pl.BlockSpec((1, TM, RET_W), lambda bb, tt: (bb, tt, 0))
    return pl.pallas_call(
        _even_in_kernel,
        grid=(b, t // TM),
        in_specs=_stream_specs() + [_mod_spec(), _resident((1, D_MODEL)), _resident(w.shape),
                                    _resident(w_vt.shape), tab, tab, tab],
        out_specs=[out] * 6 + [pl.BlockSpec((1, NA_W, TM), lambda bb, tt: (bb, 0, tt))],
        out_shape=[jax.ShapeDtypeStruct((b, t, RET_W), BF16)] * 6 + [jax.ShapeDtypeStruct((b, NA_W, t), BF16)],
        compiler_params=_params(2),
        name="even_in",
    )(ctx, x, modsel, g.reshape(1, D_MODEL), w, w_vt, *tables)


def _ret_tables(decay_logit):
    log_g = jax.nn.log_sigmoid(decay_logit.astype(F32))
    idx = jnp.arange(CHUNK, dtype=F32)
    diff = idx[:, None] - idx[None, :]
    ones = jnp.ones((CHUNK, CHUNK), F32)

    def one(lg, d):
        dd = diff if d == 0 else -diff
        dm = jnp.where(dd >= 0, jnp.exp(jnp.maximum(dd, 0.0) * lg), 0.0)
        qe = (idx + 1.0) if d == 0 else (CHUNK - idx)
        ke = (CHUNK - 1.0 - idx) if d == 0 else idx
        return jnp.stack([dm, jnp.exp(qe * lg)[:, None] * ones, jnp.exp(ke * lg)[:, None] * ones,
                          jnp.exp(CHUNK * lg) * ones])

    return jnp.stack([jnp.stack([one(log_g[d, h], d) for h in range(RET_HEADS)]) for d in range(2)])


def _ret_kernel(qf_ref, kf_ref, vf_ref, qb_ref, kb_ref, vb_ref, dec_ref, yf_ref, yb_ref, s_ref):
    @pl.when(pl.program_id(1) == 0)
    def _():
        s_ref[...] = jnp.zeros_like(s_ref)

    refs = ((qf_ref, kf_ref, vf_ref, yf_ref), (qb_ref, kb_ref, vb_ref, yb_ref))
    chains = [(d, hh, slice(hh * RET_DIM, (hh + 1) * RET_DIM)) for d in range(2) for hh in range(RET_HEADS)]
    n_sub = qf_ref.shape[1] // CHUNK
    for sub in range(n_sub):
        rows = [slice(c * CHUNK, (c + 1) * CHUNK) for c in (sub, n_sub - 1 - sub)]
        stage1 = []
        for d, hh, sl in chains:
            q_ref, k_ref, _, _ = refs[d]
            q = q_ref[0, rows[d], sl]
            stage1.append((_nt(q, k_ref[0, rows[d], sl]), _mm(q, s_ref[d, hh].astype(BF16))))
        for (d, hh, sl), (att, inter) in zip(chains, stage1):
            _, _, v_ref, y_ref = refs[d]
            y_ref[0, rows[d], sl] = (_mm((att * dec_ref[d, hh, 0]).astype(BF16), v_ref[0, rows[d], sl])
                                     + inter * dec_ref[d, hh, 1])
        for d, hh, sl in chains:
            _, k_ref, v_ref, _ = refs[d]
            kk = (k_ref[0, rows[d], sl].astype(F32) * dec_ref[d, hh, 2]).astype(BF16)
            s_ref[d, hh] = dec_ref[d, hh, 3] * s_ref[d, hh] + _tn(kk, v_ref[0, rows[d], sl])


def _scan_orders(n_steps, n_ctx):
    fwd = lambda i: i
    bwd = lambda i: jnp.where(i < n_ctx, n_ctx - 1 - i, n_steps + n_ctx - 1 - i)
    return fwd, bwd


def _retention(rq, rk, rv, dec, ctx_len):
    b, t, _ = rq.shape
    ns = t // SCAN_BLOCK
    fwd, bwd = _scan_orders(ns, ctx_len // SCAN_BLOCK)

    def spec(order):
        return pl.BlockSpec((1, SCAN_BLOCK, RET_W), lambda bb, i: (bb, order(i), 0))

    return pl.pallas_call(
        _ret_kernel,
        grid=(b, ns),
        in_specs=[spec(fwd)] * 3 + [spec(bwd)] * 3 + [_resident(dec.shape)],
        out_specs=[spec(fwd), spec(bwd)],
        out_shape=[jax.ShapeDtypeStruct((b, t, RET_W), F32)] * 2,
        scratch_shapes=[pltpu.VMEM((2, RET_HEADS, RET_DIM, RET_DIM), F32)],
        compiler_params=_params(2),
        name="retention",
    )(rq, rk, rv, rq, rk, rv, dec)


def _na_bias_tables(rpb):
    qc = np.arange(GRID_W)[:, None]
    kc = np.arange(GRID_W)[None, :]
    cstart = np.clip(qc - NA_WIN_C // 2, 0, GRID_W - NA_WIN_C)
    inwin = (kc >= cstart) & (kc < cstart + NA_WIN_C)
    dcol = np.clip(kc - qc, -(NA_WIN_C - 1), NA_WIN_C - 1) + (NA_WIN_C - 1)
    onehot = np.zeros((GRID_W, GRID_W, 2 * NA_WIN_C - 1), np.float32)
    onehot[qc, kc, dcol] = 1.0
    t = jnp.einsum("hrd,qkd->hrkq", rpb.astype(F32), jnp.asarray(onehot), precision=lax.Precision.HIGHEST)
    t = jnp.where(jnp.asarray(inwin.T), t * LOG2E, NEG)
    pad = jnp.full((NA_HEADS, 2, GRID_W, GRID_W), NEG, F32)
    tp = jnp.concatenate([pad, t, pad], axis=1)
    n = 2 * NA_WIN_R + 1
    top = jnp.concatenate([tp[:, 1:1 + n], tp[:, 0:n]], axis=-1)
    bot = jnp.concatenate([tp[:, 2:2 + n], tp[:, 1:1 + n]], axis=-1)
    return jnp.concatenate([top, bot], axis=-2)


def _na_kernel(q_ref, k_ref, vt_ref, tbl_ref, o_ref, *, rows, ctx_len):
    s = pl.program_id(1)
    n_ctx = ctx_len // SCAN_BLOCK
    n_unit = SCAN_BLOCK // CHUNK
    left = lax.broadcasted_iota(jnp.int32, (1, LANES), 1) < NA_DIM
    top_half = lax.broadcasted_iota(jnp.int32, (CHUNK, LANES), 0) < GRID_W
    left_half = lax.broadcasted_iota(jnp.int32, (CHUNK, LANES), 1) < GRID_W
    n_loc = NA_KROWS * GRID_W
    n_blk = NA_KROWS // 2

    def attend(local_of_unit):
        items = [(u, hp, slice(u * CHUNK, (u + 1) * CHUNK), slice(hp * LANES, (hp + 1) * LANES))
                 for u in range(n_unit) for hp in range(NA_HEADS // 2)]
        scores = []
        for u, hp, qrows, sl in items:
            qp = q_ref[0, qrows, sl]
            zero = jnp.zeros_like(qp)
            qboth = jnp.concatenate([jnp.where(left, qp, zero), jnp.where(left, zero, qp)], axis=0)
            s_ctx = _nt(k_ref[0, 0:ctx_len, sl], qboth)
            s_loc = None
            if local_of_unit[u] is not None:
                off, idx, pen = local_of_unit[u]
                s_loc = _nt(k_ref[0, pl.ds(off, n_loc), sl], qboth)
                s_loc = s_loc + jnp.concatenate(
                    [jnp.concatenate([tbl_ref[2 * hp, idx[m]] + pen[m], tbl_ref[2 * hp + 1, idx[m]] + pen[m]],
                                     axis=1) for m in range(n_blk)], axis=0)
            scores.append((s_ctx, s_loc))
        probs = []
        for s_ctx, s_loc in scores:
            mx = jnp.max(s_ctx, axis=0, keepdims=True)
            if s_loc is not None:
                mx = jnp.maximum(mx, jnp.max(s_loc, axis=0, keepdims=True))
            p_ctx = jnp.exp2(s_ctx - mx)
            den = jnp.sum(p_ctx, axis=0, keepdims=True)
            p_loc = None
            if s_loc is not None:
                p_loc = jnp.exp2(s_loc - mx)
                den = den + jnp.sum(p_loc, axis=0, keepdims=True)
                p_loc = p_loc.astype(BF16)
            probs.append((p_ctx.astype(BF16), p_loc, den))
        for (u, hp, qrows, sl), (p_ctx, p_loc, den) in zip(items, probs):
            ot = _mm(vt_ref[0, sl, 0:ctx_len], p_ctx)
            if p_loc is not None:
                ot = ot + _mm(vt_ref[0, sl, pl.ds(local_of_unit[u][0], n_loc)], p_loc)
            ot = ot / den
            pair_t = jnp.concatenate([ot[0:NA_DIM, 0:CHUNK], ot[NA_DIM:, CHUNK:]], axis=0)
            o_ref[0, qrows, sl] = pair_t.T.astype(BF16)

    @pl.when(s < n_ctx)
    def _():
        attend([None] * n_unit)

    @pl.when(s >= n_ctx)
    def _():
        def row_pen(kr, r):
            rs = jnp.clip(r - NA_WIN_R // 2, 0, rows - NA_WIN_R)
            return jnp.where(jnp.logical_and(kr >= rs, kr < rs + NA_WIN_R), 0.0, NEG)

        local_of_unit = []
        for u in range(n_unit):
            r0 = 2 * ((s - n_ctx) * n_unit + u)
            kb = jnp.clip(r0 - NA_WIN_R // 2, 0, rows - NA_KROWS)
            off = pl.multiple_of(ctx_len + kb * GRID_W, LANES)
            idx, pen = [], []
            for m in range(n_blk):
                kp = kb + 2 * m
                idx.append(kp - r0 + NA_WIN_R)
                pen.append(jnp.where(top_half,
                                     jnp.where(left_half, row_pen(kp, r0), row_pen(kp, r0 + 1)),
                                     jnp.where(left_half, row_pen(kp + 1, r0), row_pen(kp + 1, r0 + 1))))
            local_of_unit.append((off, idx, pen))
        attend(local_of_unit)


def _na(nq, nk, nvt, tbl, ctx_len):
    b, t, _ = nq.shape
    rows = (t - ctx_len) // GRID_W
    assert rows >= NA_KROWS and CHUNK == 2 * GRID_W == LANES
    assert (t - ctx_len) % SCAN_BLOCK == 0 and ctx_len % SCAN_BLOCK == 0
    blk = pl.BlockSpec((1, SCAN_BLOCK, NA_W), lambda bb, s: (bb, s, 0))
    return pl.pallas_call(
        functools.partial(_na_kernel, rows=rows, ctx_len=ctx_len),
        grid=(b, t // SCAN_BLOCK),
        in_specs=[blk, pl.BlockSpec((1, t, NA_W), lambda bb, s: (bb, 0, 0)),
                  pl.BlockSpec((1, NA_W, t), lambda bb, s: (bb, 0, 0)), _resident(tbl.shape)],
        out_specs=blk,
        out_shape=jax.ShapeDtypeStruct((b, t, NA_W), BF16),
        compiler_params=_params(2),
        name="na",
    )(nq, nk, nvt, tbl)


def _zero_token(v):
    bits = lax.bitcast_convert_type(v[0:SUBLANES, 0:LANES], jnp.int32)
    return lax.shift_right_logical(lax.shift_right_logical(bits, 16), 16)


def _mlp_tail(x1, m, nmlp, w1_ref, w2_ref, h2_ref, fillers=()):
    fillers = list(fillers)

    def fill():
        if fillers:
            token = fillers.pop(0)()
            head = pltpu.bitcast(h2_ref[0:2 * SUBLANES, 0:LANES], jnp.int32)
            h2_ref[0:2 * SUBLANES, 0:LANES] = pltpu.bitcast(head + token, BF16)

    h2_ref[...] = _norm_mod(x1, nmlp, m[4:5], m[3:4]).astype(BF16)
    fc = D_MODEL
    n_stage = D_FF // fc
    acc = jnp.zeros_like(x1)
    for j in range(n_stage):
        while len(fillers) > 2 * (n_stage - 1 - j):
            fill()
        u = _mm(h2_ref[...], w1_ref[:, j * fc:(j + 1) * fc])
        fill()
        u = jnp.square(jnp.maximum(u, 0.0)).astype(BF16)
        acc = acc + _mm(u, w2_ref[j * fc:(j + 1) * fc, :])
        fill()
    return x1 + m[5:6] * acc


def _two_phase(step_fn, slots_ref):
    i = pl.program_id(0)

    @pl.when(i == 0)
    def _():
        slots_ref[...] = jnp.zeros_like(slots_ref)

    for parity in range(2):
        @pl.when(i % 2 == parity)
        def _():
            step_fn(slots_ref.at[1 - parity], slots_ref.at[parity])


def _two_phase_maps(n_tiles, n_total):
    def mm(i):
        j = jnp.maximum(i - 1, 0)
        return j // n_tiles, j % n_tiles

    def fin(i):
        k = jnp.minimum(i, n_total - 1)
        return k // n_tiles, k % n_tiles

    return mm, fin


def _even_out_kernel(ctx_ref, x_ref, mod_ref, yna_ref, yf_ref, yb_ref, rg_ref, wo_ref, nmlp_ref, w1_ref, w2_ref,
                     o_ref, slots_ref, h2_ref, *, n_tiles):
    def step(cur_ref, nxt_ref):
        m = mod_ref[0, 0]
        is_ctx = (jnp.maximum(pl.program_id(0) - 1, 0) % n_tiles) == 0
        def finish_head(hh):
            sl = slice(hh * RET_DIM, (hh + 1) * RET_DIM)
            yh = yf_ref[0, :, sl] + yb_ref[0, :, sl]
            dlt = yh - jnp.mean(yh, axis=-1, keepdims=True)
            yn = dlt * lax.rsqrt(jnp.mean(dlt * dlt, axis=-1, keepdims=True) + EPS)
            nxt_ref[:, sl] = (yn * rg_ref[0, :, sl].astype(F32)).astype(BF16)
            return _zero_token(yn)

        mix = _mm(yna_ref[0], wo_ref[RET_W:RET_W + NA_W, :]) + _mm(cur_ref[...], wo_ref[0:RET_W, :])
        x1 = jnp.where(is_ctx, ctx_ref[0], x_ref[0]) + m[2:3] * mix
        o_ref[0] = _mlp_tail(x1, m, nmlp_ref[...], w1_ref, w2_ref, h2_ref,
                             [functools.partial(finish_head, hh) for hh in range(RET_HEADS)])

    _two_phase(step, slots_ref)


def _even_out(ctx, x, modsel, yf, yb, rg, yna, wo, nmlp, w1, w2, layer):
    b = x.shape[0]
    t = ctx.shape[1] + x.shape[1]
    nt = t // TM
    mm, fin = _two_phase_maps(nt, b * nt)

    def mm_tok(w):
        return pl.BlockSpec((1, TM, w), lambda i: (mm(i)[0], mm(i)[1], 0))

    def fin_tok(w):
        return pl.BlockSpec((1, TM, w), lambda i: (fin(i)[0], fin(i)[1], 0))

    return pl.pallas_call(
        functools.partial(_even_out_kernel, n_tiles=nt),
        grid=(b * nt + 1,),
        in_specs=[pl.BlockSpec((1, TM, D_MODEL), lambda i: (mm(i)[0], 0, 0)),
                  pl.BlockSpec((1, TM, D_MODEL), lambda i: (mm(i)[0], jnp.maximum(mm(i)[1] - 1, 0), 0)),
                  pl.BlockSpec((1, 1, 6, D_MODEL), lambda i: (mm(i)[0], jnp.minimum(mm(i)[1], 1), 0, 0)),
                  mm_tok(NA_W), fin_tok(RET_W), fin_tok(RET_W), fin_tok(RET_W),
                  _resident(wo.shape), _resident((1, D_MODEL)), _resident_layer(w1.shape, layer),
                  _resident_layer(w2.shape, layer)],
        out_specs=mm_tok(D_MODEL),
        out_shape=jax.ShapeDtypeStruct((b, t, D_MODEL), F32),
        scratch_shapes=[pltpu.VMEM((2, TM, RET_W), BF16), pltpu.VMEM((TM, D_MODEL), BF16)],
        compiler_params=_params(1),
        name="even_out_mlp",
    )(ctx, x, modsel, yna, yf, yb, rg, wo, nmlp.reshape(1, D_MODEL), w1, w2)


def _conv_perm():
    span = TM // SUBLANES
    rho = np.arange(TM)
    p = np.zeros((TM, TM), np.float32)
    p[rho, (rho % SUBLANES) * span + rho // SUBLANES] = 1.0
    return jnp.asarray(p, BF16)


def _odd_in_kernel(x_ref, mod_ref, g_ref, w_ref, wdt_ref, dtb_ref, perm_ref, z_ref, xbc_ref, dt_ref):
    m = mod_ref[0, 0]
    h = _norm_mod(x_ref[0], g_ref[...], m[1:2], m[0:1]).astype(BF16)
    cw = 512
    for j in range(SSM_INNER // cw):
        z_ref[0, :, j * cw:(j + 1) * cw] = _silu(_mm(h, w_ref[:, j * cw:(j + 1) * cw])).astype(BF16)
    hp = _mm(perm_ref[...], h).astype(BF16)
    for j in range(SSM_CONV_CH // cw):
        c0 = SSM_INNER + j * cw
        xbc_ref[0, :, j * cw:(j + 1) * cw] = _mm(hp, w_ref[:, c0:c0 + cw]).astype(BF16)
    raw = _mm(h, wdt_ref[...]) + dtb_ref[...]
    dt_ref[0] = jnp.maximum(raw, 0.0) + jnp.log1p(jnp.exp(-jnp.abs(raw)))


def _odd_in(xs, modsel, g, w_main, w_dt, dt_bias):
    b, t, _ = xs.shape
    tok = lambda w: pl.BlockSpec((1, TM, w), lambda bb, tt: (bb, tt, 0))
    return pl.pallas_call(
        _odd_in_kernel,
        grid=(b, t // TM),
        in_specs=[tok(D_MODEL), _mod_spec(), _resident((1, D_MODEL)), _resident(w_main.shape),
                  _resident(w_dt.shape), _resident((1, LANES)), _resident((TM, TM))],
        out_specs=[tok(SSM_INNER), tok(SSM_CONV_CH), tok(LANES)],
        out_shape=[jax.ShapeDtypeStruct((b, t, SSM_INNER), BF16),
                   jax.ShapeDtypeStruct((b, t, SSM_CONV_CH), BF16),
                   jax.ShapeDtypeStruct((b, t, LANES), F32)],
        compiler_params=_params(2),
        name="odd_in",
    )(xs, modsel, g.reshape(1, D_MODEL), w_main, w_dt, dt_bias, _conv_perm())


def _conv_kernel(main_ref, prev_ref, next_ref, w_ref, b_ref, unperm_ref, o_ref, act_ref, *, n_tiles):
    t = pl.program_id(1)
    has_prev = (t >= 2).astype(F32)
    has_next = jnp.logical_and(t >= 1, t < n_tiles - 1).astype(F32)
    span = TM // SUBLANES
    half = SSM_CONV // 2
    sub = lax.broadcasted_iota(jnp.int32, (SUBLANES, LANES), 0)

    def lane_tile(j, carry):
        lanes = pl.ds(pl.multiple_of(j * LANES, LANES), LANES)
        xm = main_ref[0, :, lanes].astype(F32)
        pv = prev_ref[0, :, lanes].astype(F32) * has_prev
        nx = next_ref[0, :, lanes].astype(F32) * has_next
        reg = {m: xm[m * SUBLANES:(m + 1) * SUBLANES] for m in range(span)}
        for i in range(1, half + 1):
            row = HALO - 1 - (i - 1) * SUBLANES
            edge = jnp.broadcast_to(pv[row:row + 1], (SUBLANES, LANES))
            reg[-i] = jnp.where(sub == 0, edge, pltpu.roll(reg[span - i], 1, 0))
            row = (i - 1) * SUBLANES
            edge = jnp.broadcast_to(nx[row:row + 1], (SUBLANES, LANES))
            reg[span + i - 1] = jnp.where(sub == SUBLANES - 1, edge, pltpu.roll(reg[i - 1], SUBLANES - 1, 0))
        wk = [jnp.broadcast_to(w_ref[k:k + 1, lanes], (SUBLANES, LANES)) for k in range(SSM_CONV)]
        bias = jnp.broadcast_to(b_ref[:, lanes], (SUBLANES, LANES))
        acts = []
        for m in range(span):
            acc = bias
            for k in range(SSM_CONV):
                acc = acc + wk[k] * reg[m + k - half]
            acts.append(acc * jnp.tanh(acc) + acc)
        act_ref[:, lanes] = jnp.concatenate(acts, axis=0).astype(BF16)
        return carry

    lax.fori_loop(0, SSM_CONV_CH // LANES, lane_tile, 0, unroll=2)
    o_ref[0] = _mm(unperm_ref[...], act_ref[...]).astype(BF16)


def _conv(xbc, w, bias, ctx_len):
    b, t, ch = xbc.shape
    assert ctx_len == TM and TM % HALO == 0 and HALO >= (SSM_CONV // 2) * SUBLANES
    nt = t // TM
    per = TM // HALO
    main = pl.BlockSpec((1, TM, ch), lambda bb, tt: (bb, tt, 0))
    prev = pl.BlockSpec((1, HALO, ch), lambda bb, tt: (bb, jnp.maximum(tt * per - 1, 0), 0))
    nxt = pl.BlockSpec((1, HALO, ch), lambda bb, tt: (bb, jnp.minimum((tt + 1) * per, nt * per - 1), 0))
    return pl.pallas_call(
        functools.partial(_conv_kernel, n_tiles=nt),
        grid=(b, nt),
        in_specs=[main, prev, nxt, _resident(w.shape), _resident((1, ch)), _resident((TM, TM))],
        out_specs=main,
        out_shape=jax.ShapeDtypeStruct((b, t, ch), BF16),
        scratch_shapes=[pltpu.VMEM((TM, ch), BF16)],
        compiler_params=_params(2),
        name="dwconv_silu",
    )(xbc, xbc, xbc, w, bias.reshape(1, ch), _conv_perm().T)


def _dot_exact_lhs(a16, x):
    hi = x.astype(BF16)
    r1 = x - hi.astype(F32)
    mid = r1.astype(BF16)
    lo = (r1 - mid.astype(F32)).astype(BF16)
    return _mm(a16, hi) + _mm(a16, mid) + _mm(a16, lo)


def _ssd_kernel(xf_ref, xb_ref, dtf_ref, dtb_ref, a_ref, tri_ref, yf_ref, yb_ref, h_ref, row_ref):
    @pl.when(pl.program_id(1) == 0)
    def _():
        h_ref[...] = jnp.zeros_like(h_ref)

    left = lax.broadcasted_iota(jnp.int32, (1, LANES), 1) < SSM_HEAD_DIM
    keep_l = jnp.where(left, jnp.uint32(0xFFFFFFFF), jnp.uint32(0))
    keep_r = jnp.where(left, jnp.uint32(0), jnp.uint32(0xFFFFFFFF))
    ii = lax.broadcasted_iota(jnp.int32, (CHUNK, CHUNK), 0)
    jj = lax.broadcasted_iota(jnp.int32, (CHUNK, CHUNK), 1)
    b_off = SSM_INNER
    c_off = SSM_INNER + SSM_GN

    directions = ((xf_ref, dtf_ref, yf_ref), (xb_ref, dtb_ref, yb_ref))
    n_sub = xf_ref.shape[1] // CHUNK
    rows_of = [[slice(c * CHUNK, (c + 1) * CHUNK) for c in (sub, n_sub - 1 - sub)] for sub in range(n_sub)]

    decays = {}
    for sub in range(n_sub):
        for d, (_, dt_ref, _) in enumerate(directions):
            dtc = dt_ref[0, rows_of[sub][d], :]
            acum = _dot_exact_lhs(tri_ref[d], dtc * a_ref[...]) * LOG2E
            acum_t = acum.T
            dt_t = dtc.T
            last = acum[CHUNK - 1:CHUNK, :] if d == 0 else acum[0:1, :]
            last_t = acum_t[:, CHUNK - 1:CHUNK] if d == 0 else acum_t[:, 0:1]
            row_ref[sub, d, 0] = acum_t - jnp.log2(dt_t)
            row_ref[sub, d, 1] = dt_t * jnp.exp2(last_t - acum_t)
            decays[sub, d] = (acum, jnp.exp2(last))

    for sub in range(n_sub):
        for d, (x_ref, _, y_ref) in enumerate(directions):
            rows = rows_of[sub][d]
            acum, elast = decays[sub, d]
            mask = (ii >= jj) if d == 0 else (jj >= ii)

            for g in range(SSM_GROUPS):
                bg = x_ref[0, rows, b_off + g * SSM_STATE:b_off + (g + 1) * SSM_STATE]
                cg = x_ref[0, rows, c_off + g * SSM_STATE:c_off + (g + 1) * SSM_STATE]
                cb = _nt(cg, bg)
                bg_t = bg.astype(F32).T
                h_t = h_ref[d, g]
                y_int = _mm(cg, h_t.astype(BF16))
                for hp in range(SSM_HPG // 2):
                    c0 = d * SSM_HEADS + g * SSM_HPG + 2 * hp
                    ch = slice(g * SSM_GW + hp * LANES, g * SSM_GW + (hp + 1) * LANES)
                    hl = slice(hp * LANES, (hp + 1) * LANES)
                    xu = pltpu.bitcast(x_ref[0, rows, ch], jnp.uint32)
                    xbd = jnp.concatenate([pltpu.bitcast(xu & keep_l, BF16), pltpu.bitcast(xu & keep_r, BF16)],
                                          axis=0)
                    intra, upd, eac = [], [], []
                    for e in range(2):
                        c = c0 + e
                        acol = jnp.broadcast_to(acum[:, c:c + 1], (CHUNK, CHUNK))
                        lmat = jnp.exp2(jnp.where(mask, acol - row_ref[sub, d, 0, c:c + 1, :], NEG))
                        intra.append((lmat * cb).astype(BF16))
                        upd.append((bg_t * row_ref[sub, d, 1, c:c + 1, :]).astype(BF16))
                        eac.append(jnp.exp2(acol))
                    lhs = jnp.concatenate([jnp.concatenate(intra, axis=1), jnp.concatenate(upd, axis=1)], axis=0)
                    r = _mm(lhs, xbd)
                    y_ref[0, rows, ch] = r[:CHUNK] + y_int[:, hl] * jnp.where(left, eac[0], eac[1])
                    el = jnp.where(left, jnp.broadcast_to(elast[:, c0:c0 + 1], (1, LANES)),
                                   jnp.broadcast_to(elast[:, c0 + 1:c0 + 2], (1, LANES)))
                    h_ref[d, g, :, hl] = el * h_t[:, hl] + r[CHUNK:]


def _ssd(xc, dt, a_row, ctx_len):
    b, t, ch = xc.shape
    ns = t // SCAN_BLOCK
    fwd, bwd = _scan_orders(ns, ctx_len // SCAN_BLOCK)
    idx = np.arange(CHUNK)
    tri = jnp.asarray(np.stack([idx[:, None] >= idx[None, :], idx[:, None] <= idx[None, :]]), BF16)

    def spec(order, w):
        return pl.BlockSpec((1, SCAN_BLOCK, w), lambda bb, i: (bb, order(i), 0))

    return pl.pallas_call(
        _ssd_kernel,
        grid=(b, ns),
        in_specs=[spec(fwd, ch), spec(bwd, ch), spec(fwd, LANES), spec(bwd, LANES),
                  _resident((1, LANES)), _resident((2, CHUNK, CHUNK))],
        out_specs=[spec(fwd, SSM_INNER), spec(bwd, SSM_INNER)],
        out_shape=[jax.ShapeDtypeStruct((b, t, SSM_INNER), F32)] * 2,
        scratch_shapes=[pltpu.VMEM((2, SSM_GROUPS, SSM_STATE, SSM_GW), F32),
                        pltpu.VMEM((SCAN_BLOCK // CHUNK, 2, 2, CHUNK, CHUNK), F32)],
        compiler_params=_params(2),
        name="ssd",
    )(xc, xc, dt, dt, a_row, tri)


def _odd_out_kernel(x_ref, mod_ref, yf_ref, yb_ref, xc_ref, z_ref, dsk_ref, nssm_ref, wo_ref,
                    nmlp_ref, w1_ref, w2_ref, nfin_ref, o_ref, slots_ref, h2_ref):
    def step(cur_ref, nxt_ref):
        m = mod_ref[0, 0]
        def finish_rows(r):
            rows = slice(r * FIN_ROWS, (r + 1) * FIN_ROWS)
            y = yf_ref[0, rows, :] + yb_ref[0, rows, :] + dsk_ref[...] * xc_ref[0, rows, :].astype(F32)
            y = y * z_ref[0, rows, :].astype(F32)
            y = y * lax.rsqrt(jnp.mean(y * y, axis=-1, keepdims=True) + EPS) * nssm_ref[...]
            nxt_ref[rows, :] = y.astype(BF16)
            return _zero_token(y)

        x1 = x_ref[0] + m[2:3] * _mm(cur_ref[...], wo_ref[...])
        x2 = _mlp_tail(x1, m, nmlp_ref[...], w1_ref, w2_ref, h2_ref,
                       [functools.partial(finish_rows, r) for r in range(TM // FIN_ROWS)])
        o_ref[0] = x2 * lax.rsqrt(jnp.mean(x2 * x2, axis=-1, keepdims=True) + EPS) * nfin_ref[...]

    _two_phase(step, slots_ref)


def _odd_out(xs, modsel, yf, yb, xc, z, dsk, nssm, wo, nmlp, w1, w2, layer, nfin, ctx_len):
    b, t, _ = xs.shape
    skip = ctx_len // TM
    nt = t // TM - skip
    mm, fin = _two_phase_maps(nt, b * nt)
    fin_tok = lambda w: pl.BlockSpec((1, TM, w), lambda i: (fin(i)[0], fin(i)[1] + skip, 0))
    vec = lambda w: _resident((1, w))
    return pl.pallas_call(
        _odd_out_kernel,
        grid=(b * nt + 1,),
        in_specs=[pl.BlockSpec((1, TM, D_MODEL), lambda i: (mm(i)[0], mm(i)[1] + skip, 0)),
                  pl.BlockSpec((1, 1, 6, D_MODEL), lambda i: (mm(i)[0], 1, 0, 0)),
                  fin_tok(SSM_INNER), fin_tok(SSM_INNER), fin_tok(SSM_INNER), fin_tok(SSM_INNER),
                  vec(SSM_INNER), vec(SSM_INNER), _resident(wo.shape), vec(D_MODEL),
                  _resident_layer(w1.shape, layer), _resident_layer(w2.shape, layer), vec(D_MODEL)],
        out_specs=pl.BlockSpec((1, TM, D_MODEL), lambda i: (mm(i)[0], mm(i)[1], 0)),
        out_shape=jax.ShapeDtypeStruct((b, t - ctx_len, D_MODEL), F32),
        scratch_shapes=[pltpu.VMEM((2, TM, SSM_INNER), BF16), pltpu.VMEM((TM, D_MODEL), BF16)],
        compiler_params=_params(1),
        name="odd_out_mlp",
    )(xs, modsel, yf, yb, xc, z, dsk.reshape(1, -1), nssm.reshape(1, -1), wo, nmlp.reshape(1, -1),
      w1, w2, nfin.reshape(1, -1))


def kernel(x, c, ctx, c_ctx, w_mod, b_mod, norm_mix, norm_mlp, w_mlp_in, w_mlp_out, w_in_even, w_out_even,
           ret_decay_logit, na_rpb, w_in_odd, conv_w, conv_b, dt_bias, a_log, d_skip, ssm_norm, w_out_odd,
           norm_final):
    assert w_mod.shape[0] == DEPTH == 2 and x.shape[2] == D_MODEL
    ctx_len = ctx.shape[1]
    seq = x.shape[1]
    assert ctx_len == TM and seq % (2 * GRID_W) == 0

    modsel = _modulation(c, c_ctx, w_mod, b_mod)
    w1 = w_mlp_in.astype(BF16)
    w2 = w_mlp_out.astype(BF16)

    n_tok = 4 * RET_W + 2 * NA_W
    w_even = w_in_even[0].astype(BF16)
    rq, rk, rv, rg, nq, nk, nvt = _even_in(ctx, x, modsel[0], norm_mix[0], w_even, w_even[:, n_tok:].T,
                                           _rope_tables(seq, ctx_len))
    yf, yb = _retention(rq, rk, rv, _ret_tables(ret_decay_logit[0]), ctx_len)
    yna = _na(nq, nk, nvt, _na_bias_tables(na_rpb[0]), ctx_len)
    xs = _even_out(ctx, x, modsel[0], yf, yb, rg, yna, w_out_even[0].astype(BF16), norm_mlp[0], w1, w2, 0)

    n_main = SSM_INNER + SSM_CONV_CH
    wi = w_in_odd[0]
    w_dt = jnp.pad(wi[:, n_main:], ((0, 0), (0, LANES - 2 * SSM_HEADS))).astype(BF16)
    pad_row = lambda v: jnp.pad(v.astype(F32).reshape(1, -1), ((0, 0), (0, LANES - 2 * SSM_HEADS)))
    z, xbc, dt = _odd_in(xs, modsel[1], norm_mix[1], wi[:, :n_main].astype(BF16), w_dt, pad_row(dt_bias[0]))
    xc = _conv(xbc, 0.5 * conv_w[0].astype(F32), 0.5 * conv_b[0].astype(F32), ctx_len)
    yf, yb = _ssd(xc, dt, pad_row(-jnp.exp(a_log[0].astype(F32))), ctx_len)
    dsk = jnp.repeat(d_skip[0].astype(F32), SSM_HEAD_DIM)
    return _odd_out(xs, modsel[1], yf, yb, xc, z, dsk, ssm_norm[0], w_out_odd[0].astype(BF16), norm_mlp[1],
                    w1, w2, 1, norm_final, ctx_len)
```

```python
import functools

import numpy as np
import jax
import jax.numpy as jnp
from jax import lax
from jax.experimental import pallas as pl
from jax.experimental.pallas import tpu as pltpu

F32 = jnp.float32
BF16 = jnp.bfloat16

D_MODEL = 1024
D_FF = 4 * D_MODEL
DEPTH = 2
GRID_W = 64
EPS = 1e-6
ROPE_BASE = 10000.0

RET_HEADS = 4
RET_DIM = 128
RET_W = RET_HEADS * RET_DIM

NA_HEADS = 8
NA_DIM = 64
NA_W = NA_HEADS * NA_DIM
NA_WIN_R = 8
NA_WIN_C = 16

SSM_INNER = 2 * D_MODEL
SSM_HEAD_DIM = 64
SSM_HEADS = SSM_INNER // SSM_HEAD_DIM
SSM_GROUPS = 4
SSM_HPG = SSM_HEADS // SSM_GROUPS
SSM_STATE = 128
SSM_CONV = 7
SSM_GN = SSM_GROUPS * SSM_STATE
SSM_CONV_CH = SSM_INNER + 2 * SSM_GN
SSM_GW = SSM_HPG * SSM_HEAD_DIM

LANES = 128
SUBLANES = 8
TM = 256
CHUNK = 128
SCAN_BLOCK = 2 * CHUNK
FIN_ROWS = 32
HALO = 32
LOG2E = 1.4426950408889634
NEG = -1e30
NA_KROWS = 10
VMEM_LIMIT = 56 * 1024 * 1024


def _params(n_axes):
    return pltpu.CompilerParams(dimension_semantics=("arbitrary",) * n_axes, vmem_limit_bytes=VMEM_LIMIT)


def _resident(shape):
    nd = len(shape)
    return pl.BlockSpec(shape, lambda *_: (0,) * nd, pipeline_mode=pl.Buffered(1))


def _resident_layer(shape, layer):
    nd = len(shape)
    return pl.BlockSpec((None,) + tuple(shape[1:]), lambda *_: (layer,) + (0,) * (nd - 1),
                        pipeline_mode=pl.Buffered(1))


def _nt(a, b):
    return lax.dot_general(a, b, (((1,), (1,)), ((), ())), preferred_element_type=F32)


def _tn(a, b):
    return lax.dot_general(a, b, (((0,), (0,)), ((), ())), preferred_element_type=F32)


def _mm(a, b):
    return jnp.dot(a, b, preferred_element_type=F32)


def _silu(v):
    return v * jax.nn.sigmoid(v)


def _norm_mod(x, g, sc, sh):
    ms = jnp.mean(x * x, axis=-1, keepdims=True)
    return (x * lax.rsqrt(ms + EPS) * g) * (1.0 + sc) + sh


def _mod_kernel(cc_ref, w_ref, b_ref, o_ref):
    s = _silu(cc_ref[...])
    o_ref[0] = _mm(s.astype(BF16), w_ref[0].astype(BF16)) + b_ref[0]


def _modulation(c, c_ctx, w_mod, b_mod):
    b = c.shape[0]
    rows = -(-(b + 1) // 8) * 8
    cc = jnp.zeros((rows, D_MODEL), F32).at[:b].set(c).at[b].set(c_ctx)
    tn = 1536
    out = pl.pallas_call(
        _mod_kernel,
        grid=(DEPTH, 6 * D_MODEL // tn),
        in_specs=[pl.BlockSpec((rows, D_MODEL), lambda l, j: (0, 0)),
                  pl.BlockSpec((1, D_MODEL, tn), lambda l, j: (l, 0, j)),
                  pl.BlockSpec((1, 1, tn), lambda l, j: (l, 0, j))],
        out_specs=pl.BlockSpec((1, rows, tn), lambda l, j: (l, 0, j)),
        out_shape=jax.ShapeDtypeStruct((DEPTH, rows, 6 * D_MODEL), F32),
        compiler_params=_params(2),
        name="modulation",
    )(cc, w_mod, b_mod.reshape(DEPTH, 1, 6 * D_MODEL))
    mod = out.reshape(DEPTH, rows, 6, D_MODEL)
    return jnp.stack([jnp.broadcast_to(mod[:, b:b + 1], (DEPTH, b, 6, D_MODEL)), mod[:, :b]], axis=2)


def _mod_spec():
    return pl.BlockSpec((1, 1, 6, D_MODEL), lambda b, t: (b, jnp.minimum(t, 1), 0, 0))


def _rope_tables(seq, ctx_len):
    pos = np.arange(seq)
    nf = RET_DIM // 4
    inv = (ROPE_BASE ** (-np.arange(nf, dtype=np.float32) / nf)).astype(np.float32)

    def cs(p):
        ang = p.astype(np.float32)[:, None] * inv[None, :]
        return np.cos(ang), np.sin(ang)

    cr, sr = cs(pos // GRID_W)
    cc, sc = cs(pos % GRID_W)
    zero = np.zeros_like(sr)
    cos = np.concatenate([cr, cr, cc, cc], axis=1)
    sin_a = np.concatenate([-sr, zero, -sc, zero], axis=1)
    sin_b = np.concatenate([zero, sr, zero, sc], axis=1)
    ident = np.ones((ctx_len, RET_DIM), np.float32)
    nul = np.zeros((ctx_len, RET_DIM), np.float32)
    return tuple(jnp.asarray(np.concatenate(parts, axis=0), F32)
                 for parts in ([ident, cos], [nul, sin_a], [nul, sin_b]))


def _stream_tile(ctx_ref, x_ref):
    return jnp.where(pl.program_id(1) == 0, ctx_ref[0], x_ref[0])


def _stream_specs():
    return [pl.BlockSpec((1, TM, D_MODEL), lambda bb, tt: (bb, 0, 0)),
            pl.BlockSpec((1, TM, D_MODEL), lambda bb, tt: (bb, jnp.maximum(tt - 1, 0), 0))]


def _even_in_kernel(ctx_ref, x_ref, mod_ref, g_ref, w_ref, wvt_ref, cos_ref, sa_ref, sb_ref,
                    rq_ref, rk_ref, rv_ref, rg_ref, nq_ref, nk_ref, nvt_ref):
    m = mod_ref[0, 0]
    h = _norm_mod(_stream_tile(ctx_ref, x_ref), g_ref[...], m[1:2], m[0:1]).astype(BF16)
    cos, sa, sb = cos_ref[...], sa_ref[...], sb_ref[...]

    def proj(j):
        return _mm(h, w_ref[:, j * RET_W:(j + 1) * RET_W])

    def rope_store(o_ref, r, scale):
        for hh in range(RET_HEADS):
            t = r[:, hh * RET_DIM:(hh + 1) * RET_DIM]
            o = t * cos + pltpu.roll(t, RET_DIM - 32, 1) * sa + pltpu.roll(t, 32, 1) * sb
            o_ref[0, :, hh * RET_DIM:(hh + 1) * RET_DIM] = (o * scale).astype(BF16)

    p = [proj(j) for j in range(6)]
    vt = _nt(wvt_ref[...], h)
    rope_store(rq_ref, p[0], 1.0)
    rope_store(rk_ref, p[1], RET_DIM ** -0.5)
    rv_ref[0] = p[2].astype(BF16)
    rg_ref[0] = _silu(p[3]).astype(BF16)
    nq_ref[0] = (p[4] * (NA_DIM ** -0.5 * LOG2E)).astype(BF16)
    nk_ref[0] = p[5].astype(BF16)
    nvt_ref[0] = vt.astype(BF16)


def _even_in(ctx, x, modsel, g, w, w_vt, tables):
    b = x.shape[0]
    t = ctx.shape[1] + x.shape[1]
    tab = pl.BlockSpec((TM, RET_DIM), lambda bb, tt: (tt, 0))
    out = pl.BlockSpec((1, TM, RET_W), lambda bb, tt: (bb, tt, 0))
    return pl.pallas_call(
        _even_in_kernel,
        grid=(b, t // TM),
        in_specs=_stream_specs() + [_mod_spec(), _resident((1, D_MODEL)), _resident(w.shape),
                                    _resident(w_vt.shape), tab, tab, tab],
        out_specs=[out] * 6 + [pl.BlockSpec((1, NA_W, TM), lambda bb, tt: (bb, 0, tt))],
        out_shape=[jax.ShapeDtypeStruct((b, t, RET_W), BF16)] * 6 + [jax.ShapeDtypeStruct((b, NA_W, t), BF16)],
        compiler_params=_params(2),
        name="even_in",
    )(ctx, x, modsel, g.reshape(1, D_MODEL), w, w_vt, *tables)


def _ret_tables(decay_logit):
    log_g = jax.nn.log_sigmoid(decay_logit.astype(F32))
    idx = jnp.arange(CHUNK, dtype=F32)
    diff = idx[:, None] - idx[None, :]
    ones = jnp.ones((CHUNK, CHUNK), F32)

    def one(lg, d):
        dd = diff if d == 0 else -diff
        dm = jnp.where(dd >= 0, jnp.exp(jnp.maximum(dd, 0.0) * lg), 0.0)
        qe = (idx + 1.0) if d == 0 else (CHUNK - idx)
        ke = (CHUNK - 1.0 - idx) if d == 0 else idx
        return jnp.stack([dm, jnp.exp(qe * lg)[:, None] * ones, jnp.exp(ke * lg)[:, None] * ones,
                          jnp.exp(CHUNK * lg) * ones])

    return jnp.stack([jnp.stack([one(log_g[d, h], d) for h in range(RET_HEADS)]) for d in range(2)])


def _ret_kernel(qf_ref, kf_ref, vf_ref, qb_ref, kb_ref, vb_ref, dec_ref, yf_ref, yb_ref, s_ref):
    @pl.when(pl.program_id(1) == 0)
    def _():
        s_ref[...] = jnp.zeros_like(s_ref)

    refs = ((qf_ref, kf_ref, vf_ref, yf_ref), (qb_ref, kb_ref, vb_ref, yb_ref))
    chains = [(d, hh, slice(hh * RET_DIM, (hh + 1) * RET_DIM)) for d in range(2) for hh in range(RET_HEADS)]
    n_sub = qf_ref.shape[1] // CHUNK
    for sub in range(n_sub):
        rows = [slice(c * CHUNK, (c + 1) * CHUNK) for c in (sub, n_sub - 1 - sub)]
        stage1 = []
        for d, hh, sl in chains:
            q_ref, k_ref, _, _ = refs[d]
            q = q_ref[0, rows[d], sl]
            stage1.append((_nt(q, k_ref[0, rows[d], sl]), _mm(q, s_ref[d, hh].astype(BF16))))
        for (d, hh, sl), (att, inter) in zip(chains, stage1):
            _, _, v_ref, y_ref = refs[d]
            y_ref[0, rows[d], sl] = (_mm((att * dec_ref[d, hh, 0]).astype(BF16), v_ref[0, rows[d], sl])
                                     + inter * dec_ref[d, hh, 1])
        for d, hh, sl in chains:
            _, k_ref, v_ref, _ = refs[d]
            kk = (k_ref[0, rows[d], sl].astype(F32) * dec_ref[d, hh, 2]).astype(BF16)
            s_ref[d, hh] = dec_ref[d, hh, 3] * s_ref[d, hh] + _tn(kk, v_ref[0, rows[d], sl])


def _scan_orders(n_steps, n_ctx):
    fwd = lambda i: i
    bwd = lambda i: jnp.where(i < n_ctx, n_ctx - 1 - i, n_steps + n_ctx - 1 - i)
    return fwd, bwd


def _retention(rq, rk, rv, dec, ctx_len):
    b, t, _ = rq.shape
    ns = t // SCAN_BLOCK
    fwd, bwd = _scan_orders(ns, ctx_len // SCAN_BLOCK)

    def spec(order):
        return pl.BlockSpec((1, SCAN_BLOCK, RET_W), lambda bb, i: (bb, order(i), 0))

    return pl.pallas_call(
        _ret_kernel,
        grid=(b, ns),
        in_specs=[spec(fwd)] * 3 + [spec(bwd)] * 3 + [_resident(dec.shape)],
        out_specs=[spec(fwd), spec(bwd)],
        out_shape=[jax.ShapeDtypeStruct((b, t, RET_W), F32)] * 2,
        scratch_shapes=[pltpu.VMEM((2, RET_HEADS, RET_DIM, RET_DIM), F32)],
        compiler_params=_params(2),
        name="retention",
    )(rq, rk, rv, rq, rk, rv, dec)


def _na_bias_tables(rpb):
    qc = np.arange(GRID_W)[:, None]
    kc = np.arange(GRID_W)[None, :]
    cstart = np.clip(qc - NA_WIN_C // 2, 0, GRID_W - NA_WIN_C)
    inwin = (kc >= cstart) & (kc < cstart + NA_WIN_C)
    dcol = np.clip(kc - qc, -(NA_WIN_C - 1), NA_WIN_C - 1) + (NA_WIN_C - 1)
    onehot = np.zeros((GRID_W, GRID_W, 2 * NA_WIN_C - 1), np.float32)
    onehot[qc, kc, dcol] = 1.0
    t = jnp.einsum("hrd,qkd->hrkq", rpb.astype(F32), jnp.asarray(onehot), precision=lax.Precision.HIGHEST)
    t = jnp.where(jnp.asarray(inwin.T), t * LOG2E, NEG)
    pad = jnp.full((NA_HEADS, 2, GRID_W, GRID_W), NEG, F32)
    tp = jnp.concatenate([pad, t, pad], axis=1)
    n = 2 * NA_WIN_R + 1
    top = jnp.concatenate([tp[:, 1:1 + n], tp[:, 0:n]], axis=-1)
    bot = jnp.concatenate([tp[:, 2:2 + n], tp[:, 1:1 + n]], axis=-1)
    return jnp.concatenate([top, bot], axis=-2)


def _na_kernel(q_ref, k_ref, vt_ref, tbl_ref, o_ref, *, rows, ctx_len):
    s = pl.program_id(1)
    n_ctx = ctx_len // SCAN_BLOCK
    n_unit = SCAN_BLOCK // CHUNK
    left = lax.broadcasted_iota(jnp.int32, (1, LANES), 1) < NA_DIM
    top_half = lax.broadcasted_iota(jnp.int32, (CHUNK, LANES), 0) < GRID_W
    left_half = lax.broadcasted_iota(jnp.int32, (CHUNK, LANES), 1) < GRID_W
    n_loc = NA_KROWS * GRID_W
    n_blk = NA_KROWS // 2

    def attend(local_of_unit):
        items = [(u, hp, slice(u * CHUNK, (u + 1) * CHUNK), slice(hp * LANES, (hp + 1) * LANES))
                 for u in range(n_unit) for hp in range(NA_HEADS // 2)]
        scores = []
        for u, hp, qrows, sl in items:
            qp = q_ref[0, qrows, sl]
            zero = jnp.zeros_like(qp)
            qboth = jnp.concatenate([jnp.where(left, qp, zero), jnp.where(left, zero, qp)], axis=0)
            s_ctx = _nt(k_ref[0, 0:ctx_len, sl], qboth)
            s_loc = None
            if local_of_unit[u] is not None:
                off, idx, pen = local_of_unit[u]
                s_loc = _nt(k_ref[0, pl.ds(off, n_loc), sl], qboth)
                s_loc = s_loc + jnp.concatenate(
                    [jnp.concatenate([tbl_ref[2 * hp, idx[m]] + pen[m], tbl_ref[2 * hp + 1, idx[m]] + pen[m]],
                                     axis=1) for m in range(n_blk)], axis=0)
            scores.append((s_ctx, s_loc))
        probs = []
        for s_ctx, s_loc in scores:
            mx = jnp.max(s_ctx, axis=0, keepdims=True)
            if s_loc is not None:
                mx = jnp.maximum(mx, jnp.max(s_loc, axis=0, keepdims=True))
            p_ctx = jnp.exp2(s_ctx - mx)
            den = jnp.sum(p_ctx, axis=0, keepdims=True)
            p_loc = None
            if s_loc is not None:
                p_loc = jnp.exp2(s_loc - mx)
                den = den + jnp.sum(p_loc, axis=0, keepdims=True)
                p_loc = p_loc.astype(BF16)
            probs.append((p_ctx.astype(BF16), p_loc, den))
        for (u, hp, qrows, sl), (p_ctx, p_loc, den) in zip(items, probs):
            ot = _mm(vt_ref[0, sl, 0:ctx_len], p_ctx)
            if p_loc is not None:
                ot = ot + _mm(vt_ref[0, sl, pl.ds(local_of_unit[u][0], n_loc)], p_loc)
            ot = ot / den
            pair_t = jnp.concatenate([ot[0:NA_DIM, 0:CHUNK], ot[NA_DIM:, CHUNK:]], axis=0)
            o_ref[0, qrows, sl] = pair_t.T.astype(BF16)

    @pl.when(s < n_ctx)
    def _():
        attend([None] * n_unit)

    @pl.when(s >= n_ctx)
    def _():
        def row_pen(kr, r):
            rs = jnp.clip(r - NA_WIN_R // 2, 0, rows - NA_WIN_R)
            return jnp.where(jnp.logical_and(kr >= rs, kr < rs + NA_WIN_R), 0.0, NEG)

        local_of_unit = []
        for u in range(n_unit):
            r0 = 2 * ((s - n_ctx) * n_unit + u)
            kb = jnp.clip(r0 - NA_WIN_R // 2, 0, rows - NA_KROWS)
            off = pl.multiple_of(ctx_len + kb * GRID_W, LANES)
            idx, pen = [], []
            for m in range(n_blk):
                kp = kb + 2 * m
                idx.append(kp - r0 + NA_WIN_R)
                pen.append(jnp.where(top_half,
                                     jnp.where(left_half, row_pen(kp, r0), row_pen(kp, r0 + 1)),
                                     jnp.where(left_half, row_pen(kp + 1, r0), row_pen(kp + 1, r0 + 1))))
            local_of_unit.append((off, idx, pen))
        attend(local_of_unit)


def _na(nq, nk, nvt, tbl, ctx_len):
    b, t, _ = nq.shape
    rows = (t - ctx_len) // GRID_W
    assert rows >= NA_KROWS and CHUNK == 2 * GRID_W == LANES
    assert (t - ctx_len) % SCAN_BLOCK == 0 and ctx_len % SCAN_BLOCK == 0
    blk = pl.BlockSpec((1, SCAN_BLOCK, NA_W), lambda bb, s: (bb, s, 0))
    return pl.pallas_call(
        functools.partial(_na_kernel, rows=rows, ctx_len=ctx_len),
        grid=(b, t // SCAN_BLOCK),
        in_specs=[blk, pl.BlockSpec((1, t, NA_W), lambda bb, s: (bb, 0, 0)),
                  pl.BlockSpec((1, NA_W, t), lambda bb, s: (bb, 0, 0)), _resident(tbl.shape)],
        out_specs=blk,
        out_shape=jax.ShapeDtypeStruct((b, t, NA_W), BF16),
        compiler_params=_params(2),
        name="na",
    )(nq, nk, nvt, tbl)


def _zero_token(v):
    bits = lax.bitcast_convert_type(v[0:SUBLANES, 0:LANES], jnp.int32)
    return lax.shift_right_logical(lax.shift_right_logical(bits, 16), 16)


def _mlp_tail(x1, m, nmlp, w1_ref, w2_ref, h2_ref, fillers=()):
    fillers = list(fillers)

    def fill():
        if fillers:
            token = fillers.pop(0)()
            head = pltpu.bitcast(h2_ref[0:2 * SUBLANES, 0:LANES], jnp.int32)
            h2_ref[0:2 * SUBLANES, 0:LANES] = pltpu.bitcast(head + token, BF16)

    h2_ref[...] = _norm_mod(x1, nmlp, m[4:5], m[3:4]).astype(BF16)
    fc = D_MODEL
    n_stage = D_FF // fc
    acc = jnp.zeros_like(x1)
    for j in range(n_stage):
        while len(fillers) > 2 * (n_stage - 1 - j):
            fill()
        u = _mm(h2_ref[...], w1_ref[:, j * fc:(j + 1) * fc])
        fill()
        u = jnp.square(jnp.maximum(u, 0.0)).astype(BF16)
        acc = acc + _mm(u, w2_ref[j * fc:(j + 1) * fc, :])
        fill()
    return x1 + m[5:6] * acc


def _two_phase(step_fn, slots_ref):
    i = pl.program_id(0)

    @pl.when(i == 0)
    def _():
        slots_ref[...] = jnp.zeros_like(slots_ref)

    for parity in range(2):
        @pl.when(i % 2 == parity)
        def _():
            step_fn(slots_ref.at[1 - parity], slots_ref.at[parity])


def _two_phase_maps(n_tiles, n_total):
    def mm(i):
        j = jnp.maximum(i - 1, 0)
        return j // n_tiles, j % n_tiles

    def fin(i):
        k = jnp.minimum(i, n_total - 1)
        return k // n_tiles, k % n_tiles

    return mm, fin


def _even_out_kernel(ctx_ref, x_ref, mod_ref, yna_ref, yf_ref, yb_ref, rg_ref, wo_ref, nmlp_ref, w1_ref, w2_ref,
                     o_ref, slots_ref, h2_ref, *, n_tiles):
    def step(cur_ref, nxt_ref):
        m = mod_ref[0, 0]
        is_ctx = (jnp.maximum(pl.program_id(0) - 1, 0) % n_tiles) == 0
        def finish_head(hh):
            sl = slice(hh * RET_DIM, (hh + 1) * RET_DIM)
            yh = yf_ref[0, :, sl] + yb_ref[0, :, sl]
            dlt = yh - jnp.mean(yh, axis=-1, keepdims=True)
            yn = dlt * lax.rsqrt(jnp.mean(dlt * dlt, axis=-1, keepdims=True) + EPS)
            nxt_ref[:, sl] = (yn * rg_ref[0, :, sl].astype(F32)).astype(BF16)
            return _zero_token(yn)

        mix = _mm(yna_ref[0], wo_ref[RET_W:RET_W + NA_W, :]) + _mm(cur_ref[...], wo_ref[0:RET_W, :])
        x1 = jnp.where(is_ctx, ctx_ref[0], x_ref[0]) + m[2:3] * mix
        o_ref[0] = _mlp_tail(x1, m, nmlp_ref[...], w1_ref, w2_ref, h2_ref,
                             [functools.partial(finish_head, hh) for hh in range(RET_HEADS)])

    _two_phase(step, slots_ref)


def _even_out(ctx, x, modsel, yf, yb, rg, yna, wo, nmlp, w1, w2, layer):
    b = x.shape[0]
    t = ctx.shape[1] + x.shape[1]
    nt = t // TM
    mm, fin = _two_phase_maps(nt, b * nt)

    def mm_tok(w):
        return pl.BlockSpec((1, TM, w), lambda i: (mm(i)[0], mm(i)[1], 0))

    def fin_tok(w):
        return pl.BlockSpec((1, TM, w), lambda i: (fin(i)[0], fin(i)[1], 0))

    return pl.pallas_call(
        functools.partial(_even_out_kernel, n_tiles=nt),
        grid=(b * nt + 1,),
        in_specs=[pl.BlockSpec((1, TM, D_MODEL), lambda i: (mm(i)[0], 0, 0)),
                  pl.BlockSpec((1, TM, D_MODEL), lambda i: (mm(i)[0], jnp.maximum(mm(i)[1] - 1, 0), 0)),
                  pl.BlockSpec((1, 1, 6, D_MODEL), lambda i: (mm(i)[0], jnp.minimum(mm(i)[1], 1), 0, 0)),
                  mm_tok(NA_W), fin_tok(RET_W), fin_tok(RET_W), fin_tok(RET_W),
                  _resident(wo.shape), _resident((1, D_MODEL)), _resident_layer(w1.shape, layer),
                  _resident_layer(w2.shape, layer)],
        out_specs=mm_tok(D_MODEL),
        out_shape=jax.ShapeDtypeStruct((b, t, D_MODEL), F32),
        scratch_shapes=[pltpu.VMEM((2, TM, RET_W), BF16), pltpu.VMEM((TM, D_MODEL), BF16)],
        compiler_params=_params(1),
        name="even_out_mlp",
    )(ctx, x, modsel, yna, yf, yb, rg, wo, nmlp.reshape(1, D_MODEL), w1, w2)


def _conv_perm():
    span = TM // SUBLANES
    rho = np.arange(TM)
    p = np.zeros((TM, TM), np.float32)
    p[rho, (rho % SUBLANES) * span + rho // SUBLANES] = 1.0
    return jnp.asarray(p, BF16)


def _odd_in_kernel(x_ref, mod_ref, g_ref, w_ref, wdt_ref, dtb_ref, perm_ref, z_ref, xbc_ref, dt_ref):
    m = mod_ref[0, 0]
    h = _norm_mod(x_ref[0], g_ref[...], m[1:2], m[0:1]).astype(BF16)
    cw = 512
    for j in range(SSM_INNER // cw):
        z_ref[0, :, j * cw:(j + 1) * cw] = _silu(_mm(h, w_ref[:, j * cw:(j + 1) * cw])).astype(BF16)
    hp = _mm(perm_ref[...], h).astype(BF16)
    for j in range(SSM_CONV_CH // cw):
        c0 = SSM_INNER + j * cw
        xbc_ref[0, :, j * cw:(j + 1) * cw] = _mm(hp, w_ref[:, c0:c0 + cw]).astype(BF16)
    raw = _mm(h, wdt_ref[...]) + dtb_ref[...]
    dt_ref[0] = jnp.maximum(raw, 0.0) + jnp.log1p(jnp.exp(-jnp.abs(raw)))


def _odd_in(xs, modsel, g, w_main, w_dt, dt_bias):
    b, t, _ = xs.shape
    tok = lambda w: pl.BlockSpec((1, TM, w), lambda bb, tt: (bb, tt, 0))
    return pl.pallas_call(
        _odd_in_kernel,
        grid=(b, t // TM),
        in_specs=[tok(D_MODEL), _mod_spec(), _resident((1, D_MODEL)), _resident(w_main.shape),
                  _resident(w_dt.shape), _resident((1, LANES)), _resident((TM, TM))],
        out_specs=[tok(SSM_INNER), tok(SSM_CONV_CH), tok(LANES)],
        out_shape=[jax.ShapeDtypeStruct((b, t, SSM_INNER), BF16),
                   jax.ShapeDtypeStruct((b, t, SSM_CONV_CH), BF16),
                   jax.ShapeDtypeStruct((b, t, LANES), F32)],
        compiler_params=_params(2),
        name="odd_in",
    )(xs, modsel, g.reshape(1, D_MODEL), w_main, w_dt, dt_bias, _conv_perm())


def _conv_kernel(main_ref, prev_ref, next_ref, w_ref, b_ref, unperm_ref, o_ref, act_ref, *, n_tiles):
    t = pl.program_id(1)
    has_prev = (t >= 2).astype(F32)
    has_next = jnp.logical_and(t >= 1, t < n_tiles - 1).astype(F32)
    span = TM // SUBLANES
    half = SSM_CONV // 2
    sub = lax.broadcasted_iota(jnp.int32, (SUBLANES, LANES), 0)

    def lane_tile(j, carry):
        lanes = pl.ds(pl.multiple_of(j * LANES, LANES), LANES)
        xm = main_ref[0, :, lanes].astype(F32)
        pv = prev_ref[0, :, lanes].astype(F32) * has_prev
        nx = next_ref[0, :, lanes].astype(F32) * has_next
        reg = {m: xm[m * SUBLANES:(m + 1) * SUBLANES] for m in range(span)}
        for i in range(1, half + 1):
            row = HALO - 1 - (i - 1) * SUBLANES
            edge = jnp.broadcast_to(pv[row:row + 1], (SUBLANES, LANES))
            reg[-i] = jnp.where(sub == 0, edge, pltpu.roll(reg[span - i], 1, 0))
            row = (i - 1) * SUBLANES
            edge = jnp.broadcast_to(nx[row:row + 1], (SUBLANES, LANES))
            reg[span + i - 1] = jnp.where(sub == SUBLANES - 1, edge, pltpu.roll(reg[i - 1], SUBLANES - 1, 0))
        wk = [jnp.broadcast_to(w_ref[k:k + 1, lanes], (SUBLANES, LANES)) for k in range(SSM_CONV)]
        bias = jnp.broadcast_to(b_ref[:, lanes], (SUBLANES, LANES))
        acts = []
        for m in range(span):
            acc = bias
            for k in range(SSM_CONV):
                acc = acc + wk[k] * reg[m + k - half]
            acts.append(acc * jnp.tanh(acc) + acc)
        act_ref[:, lanes] = jnp.concatenate(acts, axis=0).astype(BF16)
        return carry

    lax.fori_loop(0, SSM_CONV_CH // LANES, lane_tile, 0, unroll=2)
    o_ref[0] = _mm(unperm_ref[...], act_ref[...]).astype(BF16)


def _conv(xbc, w, bias, ctx_len):
    b, t, ch = xbc.shape
    assert ctx_len == TM and TM % HALO == 0 and HALO >= (SSM_CONV // 2) * SUBLANES
    nt = t // TM
    per = TM // HALO
    main = pl.BlockSpec((1, TM, ch), lambda bb, tt: (bb, tt, 0))
    prev = pl.BlockSpec((1, HALO, ch), lambda bb, tt: (bb, jnp.maximum(tt * per - 1, 0), 0))
    nxt = pl.BlockSpec((1, HALO, ch), lambda bb, tt: (bb, jnp.minimum((tt + 1) * per, nt * per - 1), 0))
    return pl.pallas_call(
        functools.partial(_conv_kernel, n_tiles=nt),
        grid=(b, nt),
        in_specs=[main, prev, nxt, _resident(w.shape), _resident((1, ch)), _resident((TM, TM))],
        out_specs=main,
        out_shape=jax.ShapeDtypeStruct((b, t, ch), BF16),
        scratch_shapes=[pltpu.VMEM((TM, ch), BF16)],
        compiler_params=_params(2),
        name="dwconv_silu",
    )(xbc, xbc, xbc, w, bias.reshape(1, ch), _conv_perm().T)


def _dot_exact_lhs(a16, x):
    hi = x.astype(BF16)
    r1 = x - hi.astype(F32)
    mid = r1.astype(BF16)
    lo = (r1 - mid.astype(F32)).astype(BF16)
    return _mm(a16, hi) + _mm(a16, mid) + _mm(a16, lo)


def _ssd_kernel(xf_ref, xb_ref, dtf_ref, dtb_ref, a_ref, tri_ref, yf_ref, yb_ref, h_ref, row_ref, *, n_ctx):
    @pl.when(pl.program_id(1) == 0)
    def _():
        h_ref[...] = jnp.zeros_like(h_ref)

    left = lax.broadcasted_iota(jnp.int32, (1, LANES), 1) < SSM_HEAD_DIM
    keep_l = jnp.where(left, jnp.uint32(0xFFFFFFFF), jnp.uint32(0))
    keep_r = jnp.where(left, jnp.uint32(0), jnp.uint32(0xFFFFFFFF))
    ii = lax.broadcasted_iota(jnp.int32, (CHUNK, CHUNK), 0)
    jj = lax.broadcasted_iota(jnp.int32, (CHUNK, CHUNK), 1)
    b_off = SSM_INNER
    c_off = SSM_INNER + SSM_GN

    directions = ((xf_ref, dtf_ref, yf_ref), (xb_ref, dtb_ref, yb_ref))
    n_sub = xf_ref.shape[1] // CHUNK
    rows_of = [[slice(c * CHUNK, (c + 1) * CHUNK) for c in (sub, n_sub - 1 - sub)] for sub in range(n_sub)]

    def scan_block(with_y):
        decays = {}
        for sub in range(n_sub):
            for d, (_, dt_ref, _) in enumerate(directions):
                dtc = dt_ref[0, rows_of[sub][d], :]
                acum = _dot_exact_lhs(tri_ref[d], dtc * a_ref[...]) * LOG2E
                acum_t = acum.T
                dt_t = dtc.T
                last = acum[CHUNK - 1:CHUNK, :] if d == 0 else acum[0:1, :]
                last_t = acum_t[:, CHUNK - 1:CHUNK] if d == 0 else acum_t[:, 0:1]
                if with_y:
                    row_ref[sub, d, 0] = acum_t - jnp.log2(dt_t)
                row_ref[sub, d, 1] = dt_t * jnp.exp2(last_t - acum_t)
                decays[sub, d] = (acum, jnp.exp2(last))

        for sub in range(n_sub):
            for d, (x_ref, _, y_ref) in enumerate(directions):
                rows = rows_of[sub][d]
                acum, elast = decays[sub, d]
                mask = (ii >= jj) if d == 0 else (jj >= ii)
                if not with_y:
                    y_ref[0, rows, :] = jnp.zeros((CHUNK, SSM_INNER), F32)

                for g in range(SSM_GROUPS):
                    bg = x_ref[0, rows, b_off + g * SSM_STATE:b_off + (g + 1) * SSM_STATE]
                    bg_t = bg.astype(F32).T
                    h_t = h_ref[d, g]
                    if with_y:
                        cg = x_ref[0, rows, c_off + g * SSM_STATE:c_off + (g + 1) * SSM_STATE]
                        cb = _nt(cg, bg)
                        y_int = _mm(cg, h_t.astype(BF16))
                    for hp in range(SSM_HPG // 2):
                        c0 = d * SSM_HEADS + g * SSM_HPG + 2 * hp
                        ch = slice(g * SSM_GW + hp * LANES, g * SSM_GW + (hp + 1) * LANES)
                        hl = slice(hp * LANES, (hp + 1) * LANES)
                        xu = pltpu.bitcast(x_ref[0, rows, ch], jnp.uint32)
                        xbd = jnp.concatenate([pltpu.bitcast(xu & keep_l, BF16), pltpu.bitcast(xu & keep_r, BF16)],
                                              axis=0)
                        intra, upd, eac = [], [], []
                        for e in range(2):
                            c = c0 + e
                            upd.append((bg_t * row_ref[sub, d, 1, c:c + 1, :]).astype(BF16))
                            if with_y:
                                acol = jnp.broadcast_to(acum[:, c:c + 1], (CHUNK, CHUNK))
                                lmat = jnp.exp2(jnp.where(mask, acol - row_ref[sub, d, 0, c:c + 1, :], NEG))
                                intra.append((lmat * cb).astype(BF16))
                                eac.append(jnp.exp2(acol))
                        el = jnp.where(left, jnp.broadcast_to(elast[:, c0:c0 + 1], (1, LANES)),
                                       jnp.broadcast_to(elast[:, c0 + 1:c0 + 2], (1, LANES)))
                        if with_y:
                            lhs = jnp.concatenate([jnp.concatenate(intra, axis=1), jnp.concatenate(upd, axis=1)],
                                                  axis=0)
                            r = _mm(lhs, xbd)
                            y_ref[0, rows, ch] = r[:CHUNK] + y_int[:, hl] * jnp.where(left, eac[0], eac[1])
                            h_ref[d, g, :, hl] = el * h_t[:, hl] + r[CHUNK:]
                        else:
                            h_ref[d, g, :, hl] = el * h_t[:, hl] + _mm(jnp.concatenate(upd, axis=1), xbd)

    is_ctx = pl.program_id(1) < n_ctx

    @pl.when(is_ctx)
    def _():
        scan_block(False)

    @pl.when(jnp.logical_not(is_ctx))
    def _():
        scan_block(True)


def _ssd(xc, dt, a_row, ctx_len):
    b, t, ch = xc.shape
    ns = t // SCAN_BLOCK
    fwd, bwd = _scan_orders(ns, ctx_len // SCAN_BLOCK)
    idx = np.arange(CHUNK)
    tri = jnp.asarray(np.stack([idx[:, None] >= idx[None, :], idx[:, None] <= idx[None, :]]), BF16)

    def spec(order, w):
        return pl.BlockSpec((1, SCAN_BLOCK, w), lambda bb, i: (bb, order(i), 0))

    return pl.pallas_call(
        functools.partial(_ssd_kernel, n_ctx=ctx_len // SCAN_BLOCK),
        grid=(b, ns),
        in_specs=[spec(fwd, ch), spec(bwd, ch), spec(fwd, LANES), spec(bwd, LANES),
                  _resident((1, LANES)), _resident((2, CHUNK, CHUNK))],
        out_specs=[spec(fwd, SSM_INNER), spec(bwd, SSM_INNER)],
        out_shape=[jax.ShapeDtypeStruct((b, t, SSM_INNER), F32)] * 2,
        scratch_shapes=[pltpu.VMEM((2, SSM_GROUPS, SSM_STATE, SSM_GW), F32),
                        pltpu.VMEM((SCAN_BLOCK // CHUNK, 2, 2, CHUNK, CHUNK), F32)],
        compiler_params=_params(2),
        name="ssd",
    )(xc, xc, dt, dt, a_row, tri)


def _odd_out_kernel(x_ref, mod_ref, yf_ref, yb_ref, xc_ref, z_ref, dsk_ref, nssm_ref, wo_ref,
                    nmlp_ref, w1_ref, w2_ref, nfin_ref, o_ref, slots_ref, h2_ref):
    def step(cur_ref, nxt_ref):
        m = mod_ref[0, 0]
        def finish_rows(r):
            rows = slice(r * FIN_ROWS, (r + 1) * FIN_ROWS)
            y = yf_ref[0, rows, :] + yb_ref[0, rows, :] + dsk_ref[...] * xc_ref[0, rows, :].astype(F32)
            y = y * z_ref[0, rows, :].astype(F32)
            y = y * lax.rsqrt(jnp.mean(y * y, axis=-1, keepdims=True) + EPS) * nssm_ref[...]
            nxt_ref[rows, :] = y.astype(BF16)
            return _zero_token(y)

        x1 = x_ref[0] + m[2:3] * _mm(cur_ref[...], wo_ref[...])
        x2 = _mlp_tail(x1, m, nmlp_ref[...], w1_ref, w2_ref, h2_ref,
                       [functools.partial(finish_rows, r) for r in range(TM // FIN_ROWS)])
        o_ref[0] = x2 * lax.rsqrt(jnp.mean(x2 * x2, axis=-1, keepdims=True) + EPS) * nfin_ref[...]

    _two_phase(step, slots_ref)


def _odd_out(xs, modsel, yf, yb, xc, z, dsk, nssm, wo, nmlp, w1, w2, layer, nfin, ctx_len):
    b, t, _ = xs.shape
    skip = ctx_len // TM
    nt = t // TM - skip
    mm, fin = _two_phase_maps(nt, b * nt)
    fin_tok = lambda w: pl.BlockSpec((1, TM, w), lambda i: (fin(i)[0], fin(i)[1] + skip, 0))
    vec = lambda w: _resident((1, w))
    return pl.pallas_call(
        _odd_out_kernel,
        grid=(b * nt + 1,),
        in_specs=[pl.BlockSpec((1, TM, D_MODEL), lambda i: (mm(i)[0], mm(i)[1] + skip, 0)),
                  pl.BlockSpec((1, 1, 6, D_MODEL), lambda i: (mm(i)[0], 1, 0, 0)),
                  fin_tok(SSM_INNER), fin_tok(SSM_INNER), fin_tok(SSM_INNER), fin_tok(SSM_INNER),
                  vec(SSM_INNER), vec(SSM_INNER), _resident(wo.shape), vec(D_MODEL),
                  _resident_layer(w1.shape, layer), _resident_layer(w2.shape, layer), vec(D_MODEL)],
        out_specs=pl.BlockSpec((1, TM, D_MODEL), lambda i: (mm(i)[0], mm(i)[1], 0)),
        out_shape=jax.ShapeDtypeStruct((b, t - ctx_len, D_MODEL), F32),
        scratch_shapes=[pltpu.VMEM((2, TM, SSM_INNER), BF16), pltpu.VMEM((TM, D_MODEL), BF16)],
        compiler_params=_params(1),
        name="odd_out_mlp",
    )(xs, modsel, yf, yb, xc, z, dsk.reshape(1, -1), nssm.reshape(1, -1), wo, nmlp.reshape(1, -1),
      w1, w2, nfin.reshape(1, -1))


def kernel(x, c, ctx, c_ctx, w_mod, b_mod, norm_mix, norm_mlp, w_mlp_in, w_mlp_out, w_in_even, w_out_even,
           ret_decay_logit, na_rpb, w_in_odd, conv_w, conv_b, dt_bias, a_log, d_skip, ssm_norm, w_out_odd,
           norm_final):
    assert w_mod.shape[0] == DEPTH == 2 and x.shape[2] == D_MODEL
    ctx_len = ctx.shape[1]
    seq = x.shape[1]
    assert ctx_len == TM and seq % (2 * GRID_W) == 0

    modsel = _modulation(c, c_ctx, w_mod, b_mod)
    w1 = w_mlp_in.astype(BF16)
    w2 = w_mlp_out.astype(BF16)

    n_tok = 4 * RET_W + 2 * NA_W
    w_even = w_in_even[0].astype(BF16)
    rq, rk, rv, rg, nq, nk, nvt = _even_in(ctx, x, modsel[0], norm_mix[0], w_even, w_even[:, n_tok:].T,
                                           _rope_tables(seq, ctx_len))
    yf, yb = _retention(rq, rk, rv, _ret_tables(ret_decay_logit[0]), ctx_len)
    yna = _na(nq, nk, nvt, _na_bias_tables(na_rpb[0]), ctx_len)
    xs = _even_out(ctx, x, modsel[0], yf, yb, rg, yna, w_out_even[0].astype(BF16), norm_mlp[0], w1, w2, 0)

    n_main = SSM_INNER + SSM_CONV_CH
    wi = w_in_odd[0]
    w_dt = jnp.pad(wi[:, n_main:], ((0, 0), (0, LANES - 2 * SSM_HEADS))).astype(BF16)
    pad_row = lambda v: jnp.pad(v.astype(F32).reshape(1, -1), ((0, 0), (0, LANES - 2 * SSM_HEADS)))
    z, xbc, dt = _odd_in(xs, modsel[1], norm_mix[1], wi[:, :n_main].astype(BF16), w_dt, pad_row(dt_bias[0]))
    xc = _conv(xbc, 0.5 * conv_w[0].astype(F32), 0.5 * conv_b[0].astype(F32), ctx_len)
    yf, yb = _ssd(xc, dt, pad_row(-jnp.exp(a_log[0].astype(F32))), ctx_len)
    dsk = jnp.repeat(d_skip[0].astype(F32), SSM_HEAD_DIM)
    return _odd_out(xs, modsel[1], yf, yb, xc, z, dsk, ssm_norm[0], w_out_odd[0].astype(BF16), norm_mlp[1],
                    w1, w2, 1, norm_final, ctx_len)
```

```python
import functools

import numpy as np
import jax
import jax.numpy as jnp
from jax import lax
from jax.experimental import pallas as pl
from jax.experimental.pallas import tpu as pltpu

F32 = jnp.float32
BF16 = jnp.bfloat16

D_MODEL = 1024
D_FF = 4 * D_MODEL
DEPTH = 2
GRID_W = 64
EPS = 1e-6
ROPE_BASE = 10000.0

RET_HEADS = 4
RET_DIM = 128
RET_W = RET_HEADS * RET_DIM

NA_HEADS = 8
NA_DIM = 64
NA_W = NA_HEADS * NA_DIM
NA_WIN_R = 8
NA_WIN_C = 16

SSM_INNER = 2 * D_MODEL
SSM_HEAD_DIM = 64
SSM_HEADS = SSM_INNER // SSM_HEAD_DIM
SSM_GROUPS = 4
SSM_HPG = SSM_HEADS // SSM_GROUPS
SSM_STATE = 128
SSM_CONV = 7
SSM_GN = SSM_GROUPS * SSM_STATE
SSM_CONV_CH = SSM_INNER + 2 * SSM_GN
SSM_GW = SSM_HPG * SSM_HEAD_DIM

LANES = 128
SUBLANES = 8
TM = 256
CHUNK = 128
SCAN_BLOCK = 2 * CHUNK
FIN_ROWS = 32
HALO = 32
LOG2E = 1.4426950408889634
NEG = -1e30
NA_KROWS = 10
VMEM_LIMIT = 56 * 1024 * 1024


def _params(n_axes):
    return pltpu.CompilerParams(dimension_semantics=("arbitrary",) * n_axes, vmem_limit_bytes=VMEM_LIMIT)


def _resident(shape):
    nd = len(shape)
    return pl.BlockSpec(shape, lambda *_: (0,) * nd, pipeline_mode=pl.Buffered(1))


def _resident_layer(shape, layer):
    nd = len(shape)
    return pl.BlockSpec((None,) + tuple(shape[1:]), lambda *_: (layer,) + (0,) * (nd - 1),
                        pipeline_mode=pl.Buffered(1))


def _nt(a, b):
    return lax.dot_general(a, b, (((1,), (1,)), ((), ())), preferred_element_type=F32)


def _tn(a, b):
    return lax.dot_general(a, b, (((0,), (0,)), ((), ())), preferred_element_type=F32)


def _mm(a, b):
    return jnp.dot(a, b, preferred_element_type=F32)


def _silu(v):
    return v * jax.nn.sigmoid(v)


def _norm_mod(x, g, sc, sh):
    ms = jnp.mean(x * x, axis=-1, keepdims=True)
    return (x * lax.rsqrt(ms + EPS) * g) * (1.0 + sc) + sh


def _mod_kernel(cc_ref, w_ref, b_ref, o_ref):
    s = _silu(cc_ref[...])
    o_ref[0] = _mm(s.astype(BF16), w_ref[0].astype(BF16)) + b_ref[0]


def _modulation(c, c_ctx, w_mod, b_mod):
    b = c.shape[0]
    rows = -(-(b + 1) // 8) * 8
    cc = jnp.zeros((rows, D_MODEL), F32).at[:b].set(c).at[b].set(c_ctx)
    tn = 1536
    out = pl.pallas_call(
        _mod_kernel,
        grid=(DEPTH, 6 * D_MODEL // tn),
        in_specs=[pl.BlockSpec((rows, D_MODEL), lambda l, j: (0, 0)),
                  pl.BlockSpec((1, D_MODEL, tn), lambda l, j: (l, 0, j)),
                  pl.BlockSpec((1, 1, tn), lambda l, j: (l, 0, j))],
        out_specs=pl.BlockSpec((1, rows, tn), lambda l, j: (l, 0, j)),
        out_shape=jax.ShapeDtypeStruct((DEPTH, rows, 6 * D_MODEL), F32),
        compiler_params=_params(2),
        name="modulation",
    )(cc, w_mod, b_mod.reshape(DEPTH, 1, 6 * D_MODEL))
    mod = out.reshape(DEPTH, rows, 6, D_MODEL)
    return jnp.stack([jnp.broadcast_to(mod[:, b:b + 1], (DEPTH, b, 6, D_MODEL)), mod[:, :b]], axis=2)


def _mod_spec():
    return pl.BlockSpec((1, 1, 6, D_MODEL), lambda b, t: (b, jnp.minimum(t, 1), 0, 0))


def _rope_tables(seq, ctx_len):
    pos = np.arange(seq)
    nf = RET_DIM // 4
    inv = (ROPE_BASE ** (-np.arange(nf, dtype=np.float32) / nf)).astype(np.float32)

    def cs(p):
        ang = p.astype(np.float32)[:, None] * inv[None, :]
        return np.cos(ang), np.sin(ang)

    cr, sr = cs(pos // GRID_W)
    cc, sc = cs(pos % GRID_W)
    zero = np.zeros_like(sr)
    cos = np.concatenate([cr, cr, cc, cc], axis=1)
    sin_a = np.concatenate([-sr, zero, -sc, zero], axis=1)
    sin_b = np.concatenate([zero, sr, zero, sc], axis=1)
    ident = np.ones((ctx_len, RET_DIM), np.float32)
    nul = np.zeros((ctx_len, RET_DIM), np.float32)
    return tuple(jnp.asarray(np.concatenate(parts, axis=0), F32)
                 for parts in ([ident, cos], [nul, sin_a], [nul, sin_b]))


def _stream_tile(ctx_ref, x_ref):
    return jnp.where(pl.program_id(1) == 0, ctx_ref[0], x_ref[0])


def _stream_specs():
    return [pl.BlockSpec((1, TM, D_MODEL), lambda bb, tt: (bb, 0, 0)),
            pl.BlockSpec((1, TM, D_MODEL), lambda bb, tt: (bb, jnp.maximum(tt - 1, 0), 0))]


def _even_in_kernel(ctx_ref, x_ref, mod_ref, g_ref, w_ref, wvt_ref, cos_ref, sa_ref, sb_ref,
                    rq_ref, rk_ref, rv_ref, rg_ref, nq_ref, nk_ref, nvt_ref):
    m = mod_ref[0, 0]
    h = _norm_mod(_stream_tile(ctx_ref, x_ref), g_ref[...], m[1:2], m[0:1]).astype(BF16)
    cos, sa, sb = cos_ref[...], sa_ref[...], sb_ref[...]

    def proj(j):
        return _mm(h, w_ref[:, j * RET_W:(j + 1) * RET_W])

    def rope_store(o_ref, r, scale):
        for hh in range(RET_HEADS):
            t = r[:, hh * RET_DIM:(hh + 1) * RET_DIM]
            o = t * cos + pltpu.roll(t, RET_DIM - 32, 1) * sa + pltpu.roll(t, 32, 1) * sb
            o_ref[0, :, hh * RET_DIM:(hh + 1) * RET_DIM] = (o * scale).astype(BF16)

    p = [proj(j) for j in range(6)]
    vt = _nt(wvt_ref[...], h)
    rope_store(rq_ref, p[0], 1.0)
    rope_store(rk_ref, p[1], RET_DIM ** -0.5)
    rv_ref[0] = p[2].astype(BF16)
    rg_ref[0] = _silu(p[3]).astype(BF16)
    nq_ref[0] = (p[4] * (NA_DIM ** -0.5 * LOG2E)).astype(BF16)
    nk_ref[0] = p[5].astype(BF16)
    nvt_ref[0] = vt.astype(BF16)


def _even_in(ctx, x, modsel, g, w, w_vt, tables):
    b = x.shape[0]
    t = ctx.shape[1] + x.shape[1]
    tab = pl.BlockSpec((TM, RET_DIM), lambda bb, tt: (tt, 0))
    out = pl.BlockSpec((1, TM, RET_W), lambda bb, tt: (bb, tt, 0))
    return pl.pallas_call(
        _even_in_kernel,
        grid=(b, t // TM),
        in_specs=_stream_specs() + [_mod_spec(), _resident((1, D_MODEL)), _resident(w.shape),
                                    _resident(w_vt.shape), tab, tab, tab],
        out_specs=[out] * 6 + [pl.BlockSpec((1, NA_W, TM), lambda bb, tt: (bb, 0, tt))],
        out_shape=[jax.ShapeDtypeStruct((b, t, RET_W), BF16)] * 6 + [jax.ShapeDtypeStruct((b, NA_W, t), BF16)],
        compiler_params=_params(2),
        name="even_in",
    )(ctx, x, modsel, g.reshape(1, D_MODEL), w, w_vt, *tables)


def _ret_tables(decay_logit):
    log_g = jax.nn.log_sigmoid(decay_logit.astype(F32))
    idx = jnp.arange(CHUNK, dtype=F32)
    diff = idx[:, None] - idx[None, :]
    ones = jnp.ones((CHUNK, CHUNK), F32)

    def one(lg, d):
        dd = diff if d == 0 else -diff
        dm = jnp.where(dd >= 0, jnp.exp(jnp.maximum(dd, 0.0) * lg), 0.0)
        qe = (idx + 1.0) if d == 0 else (CHUNK - idx)
        ke = (CHUNK - 1.0 - idx) if d == 0 else idx
        return jnp.stack([dm, jnp.exp(qe * lg)[:, None] * ones, jnp.exp(ke * lg)[:, None] * ones,
                          jnp.exp(CHUNK * lg) * ones])

    return jnp.stack([jnp.stack([one(log_g[d, h], d) for h in range(RET_HEADS)]) for d in range(2)])


def _ret_kernel(qf_ref, kf_ref, vf_ref, qb_ref, kb_ref, vb_ref, dec_ref, yf_ref, yb_ref, s_ref):
    @pl.when(pl.program_id(1) == 0)
    def _():
        s_ref[...] = jnp.zeros_like(s_ref)

    refs = ((qf_ref, kf_ref, vf_ref, yf_ref), (qb_ref, kb_ref, vb_ref, yb_ref))
    chains = [(d, hh, slice(hh * RET_DIM, (hh + 1) * RET_DIM)) for d in range(2) for hh in range(RET_HEADS)]
    n_sub = qf_ref.shape[1] // CHUNK
    for sub in range(n_sub):
        rows = [slice(c * CHUNK, (c + 1) * CHUNK) for c in (sub, n_sub - 1 - sub)]
        stage1 = []
        for d, hh, sl in chains:
            q_ref, k_ref, _, _ = refs[d]
            q = q_ref[0, rows[d], sl]
            stage1.append((_nt(q, k_ref[0, rows[d], sl]), _mm(q, s_ref[d, hh].astype(BF16))))
        for (d, hh, sl), (att, inter) in zip(chains, stage1):
            _, _, v_ref, y_ref = refs[d]
            y_ref[0, rows[d], sl] = (_mm((att * dec_ref[d, hh, 0]).astype(BF16), v_ref[0, rows[d], sl])
                                     + inter * dec_ref[d, hh, 1])
        for d, hh, sl in chains:
            _, k_ref, v_ref, _ = refs[d]
            kk = (k_ref[0, rows[d], sl].astype(F32) * dec_ref[d, hh, 2]).astype(BF16)
            s_ref[d, hh] = dec_ref[d, hh, 3] * s_ref[d, hh] + _tn(kk, v_ref[0, rows[d], sl])


def _scan_orders(n_steps, n_ctx):
    fwd = lambda i: i
    bwd = lambda i: jnp.where(i < n_ctx, n_ctx - 1 - i, n_steps + n_ctx - 1 - i)
    return fwd, bwd


def _retention(rq, rk, rv, dec, ctx_len):
    b, t, _ = rq.shape
    ns = t // SCAN_BLOCK
    fwd, bwd = _scan_orders(ns, ctx_len // SCAN_BLOCK)

    def spec(order):
        return pl.BlockSpec((1, SCAN_BLOCK, RET_W), lambda bb, i: (bb, order(i), 0))

    return pl.pallas_call(
        _ret_kernel,
        grid=(b, ns),
        in_specs=[spec(fwd)] * 3 + [spec(bwd)] * 3 + [_resident(dec.shape)],
        out_specs=[spec(fwd), spec(bwd)],
        out_shape=[jax.ShapeDtypeStruct((b, t, RET_W), F32)] * 2,
        scratch_shapes=[pltpu.VMEM((2, RET_HEADS, RET_DIM, RET_DIM), F32)],
        compiler_params=_params(2),
        name="retention",
    )(rq, rk, rv, rq, rk, rv, dec)


def _na_bias_tables(rpb):
    qc = np.arange(GRID_W)[:, None]
    kc = np.arange(GRID_W)[None, :]
    cstart = np.clip(qc - NA_WIN_C // 2, 0, GRID_W - NA_WIN_C)
    inwin = (kc >= cstart) & (kc < cstart + NA_WIN_C)
    dcol = np.clip(kc - qc, -(NA_WIN_C - 1), NA_WIN_C - 1) + (NA_WIN_C - 1)
    onehot = np.zeros((GRID_W, GRID_W, 2 * NA_WIN_C - 1), np.float32)
    onehot[qc, kc, dcol] = 1.0
    t = jnp.einsum("hrd,qkd->hrkq", rpb.astype(F32), jnp.asarray(onehot), precision=lax.Precision.HIGHEST)
    t = jnp.where(jnp.asarray(inwin.T), t * LOG2E, NEG)
    pad = jnp.full((NA_HEADS, 2, GRID_W, GRID_W), NEG, F32)
    tp = jnp.concatenate([pad, t, pad], axis=1)
    n = 2 * NA_WIN_R + 1
    top = jnp.concatenate([tp[:, 1:1 + n], tp[:, 0:n]], axis=-1)
    bot = jnp.concatenate([tp[:, 2:2 + n], tp[:, 1:1 + n]], axis=-1)
    return jnp.concatenate([top, bot], axis=-2)


def _na_kernel(q_ref, k_ref, vt_ref, tbl_ref, o_ref, *, rows, ctx_len):
    s = pl.program_id(1)
    n_ctx = ctx_len // SCAN_BLOCK
    n_unit = SCAN_BLOCK // CHUNK
    left = lax.broadcasted_iota(jnp.int32, (1, LANES), 1) < NA_DIM
    top_half = lax.broadcasted_iota(jnp.int32, (CHUNK, LANES), 0) < GRID_W
    left_half = lax.broadcasted_iota(jnp.int32, (CHUNK, LANES), 1) < GRID_W
    n_loc = NA_KROWS * GRID_W
    n_blk = NA_KROWS // 2

    def attend(local_of_unit):
        items = [(u, hp, slice(u * CHUNK, (u + 1) * CHUNK), slice(hp * LANES, (hp + 1) * LANES))
                 for u in range(n_unit) for hp in range(NA_HEADS // 2)]
        scores = []
        for u, hp, qrows, sl in items:
            qp = q_ref[0, qrows, sl]
            zero = jnp.zeros_like(qp)
            qboth = jnp.concatenate([jnp.where(left, qp, zero), jnp.where(left, zero, qp)], axis=0)
            s_ctx = _nt(k_ref[0, 0:ctx_len, sl], qboth)
            s_loc = None
            if local_of_unit[u] is not None:
                off, idx, pen = local_of_unit[u]
                s_loc = _nt(k_ref[0, pl.ds(off, n_loc), sl], qboth)
                s_loc = s_loc + jnp.concatenate(
                    [jnp.concatenate([tbl_ref[2 * hp, idx[m]] + pen[m], tbl_ref[2 * hp + 1, idx[m]] + pen[m]],
                                     axis=1) for m in range(n_blk)], axis=0)
            scores.append((s_ctx, s_loc))
        probs = []
        for s_ctx, s_loc in scores:
            mx = jnp.max(s_ctx, axis=0, keepdims=True)
            if s_loc is not None:
                mx = jnp.maximum(mx, jnp.max(s_loc, axis=0, keepdims=True))
            p_ctx = jnp.exp2(s_ctx - mx)
            den = jnp.sum(p_ctx, axis=0, keepdims=True)
            p_loc = None
            if s_loc is not None:
                p_loc = jnp.exp2(s_loc - mx)
                den = den + jnp.sum(p_loc, axis=0, keepdims=True)
                p_loc = p_loc.astype(BF16)
            probs.append((p_ctx.astype(BF16), p_loc, den))
        for (u, hp, qrows, sl), (p_ctx, p_loc, den) in zip(items, probs):
            ot = _mm(vt_ref[0, sl, 0:ctx_len], p_ctx)
            if p_loc is not None:
                ot = ot + _mm(vt_ref[0, sl, pl.ds(local_of_unit[u][0], n_loc)], p_loc)
            ot = ot / den
            pair_t = jnp.concatenate([ot[0:NA_DIM, 0:CHUNK], ot[NA_DIM:, CHUNK:]], axis=0)
            o_ref[0, qrows, sl] = pair_t.T.astype(BF16)

    @pl.when(s < n_ctx)
    def _():
        attend([None] * n_unit)

    @pl.when(s >= n_ctx)
    def _():
        def row_pen(kr, r):
            rs = jnp.clip(r - NA_WIN_R // 2, 0, rows - NA_WIN_R)
            return jnp.where(jnp.logical_and(kr >= rs, kr < rs + NA_WIN_R), 0.0, NEG)

        local_of_unit = []
        for u in range(n_unit):
            r0 = 2 * ((s - n_ctx) * n_unit + u)
            kb = jnp.clip(r0 - NA_WIN_R // 2, 0, rows - NA_KROWS)
            off = pl.multiple_of(ctx_len + kb * GRID_W, LANES)
            idx, pen = [], []
            for m in range(n_blk):
                kp = kb + 2 * m
                idx.append(kp - r0 + NA_WIN_R)
                pen.append(jnp.where(top_half,
                                     jnp.where(left_half, row_pen(kp, r0), row_pen(kp, r0 + 1)),
                                     jnp.where(left_half, row_pen(kp + 1, r0), row_pen(kp + 1, r0 + 1))))
            local_of_unit.append((off, idx, pen))
        attend(local_of_unit)


def _na(nq, nk, nvt, tbl, ctx_len):
    b, t, _ = nq.shape
    rows = (t - ctx_len) // GRID_W
    assert rows >= NA_KROWS and CHUNK == 2 * GRID_W == LANES
    assert (t - ctx_len) % SCAN_BLOCK == 0 and ctx_len % SCAN_BLOCK == 0
    blk = pl.BlockSpec((1, SCAN_BLOCK, NA_W), lambda bb, s: (bb, s, 0))
    return pl.pallas_call(
        functools.partial(_na_kernel, rows=rows, ctx_len=ctx_len),
        grid=(b, t // SCAN_BLOCK),
        in_specs=[blk, pl.BlockSpec((1, t, NA_W), lambda bb, s: (bb, 0, 0)),
                  pl.BlockSpec((1, NA_W, t), lambda bb, s: (bb, 0, 0)), _resident(tbl.shape)],
        out_specs=blk,
        out_shape=jax.ShapeDtypeStruct((b, t, NA_W), BF16),
        compiler_params=_params(2),
        name="na",
    )(nq, nk, nvt, tbl)


def _zero_token(v):
    bits = lax.bitcast_convert_type(v[0:SUBLANES, 0:LANES], jnp.int32)
    return lax.shift_right_logical(lax.shift_right_logical(bits, 16), 16)


def _mlp_tail(x1, m, nmlp, w1_ref, w2_ref, h2_ref, fillers=()):
    fillers = list(fillers)

    def fill():
        if fillers:
            token = fillers.pop(0)()
            head = pltpu.bitcast(h2_ref[0:2 * SUBLANES, 0:LANES], jnp.int32)
            h2_ref[0:2 * SUBLANES, 0:LANES] = pltpu.bitcast(head + token, BF16)

    h2_ref[...] = _norm_mod(x1, nmlp, m[4:5], m[3:4]).astype(BF16)
    fc = D_MODEL
    n_stage = D_FF // fc
    acc = jnp.zeros_like(x1)
    for j in range(n_stage):
        while len(fillers) > 2 * (n_stage - 1 - j):
            fill()
        u = _mm(h2_ref[...], w1_ref[:, j * fc:(j + 1) * fc])
        fill()
        u = jnp.square(jnp.maximum(u, 0.0)).astype(BF16)
        acc = acc + _mm(u, w2_ref[j * fc:(j + 1) * fc, :])
        fill()
    return x1 + m[5:6] * acc


def _two_phase(step_fn, slots_ref):
    i = pl.program_id(0)

    @pl.when(i == 0)
    def _():
        slots_ref[...] = jnp.zeros_like(slots_ref)

    for parity in range(2):
        @pl.when(i % 2 == parity)
        def _():
            step_fn(slots_ref.at[1 - parity], slots_ref.at[parity])


def _two_phase_maps(n_tiles, n_total):
    def mm(i):
        j = jnp.maximum(i - 1, 0)
        return j // n_tiles, j % n_tiles

    def fin(i):
        k = jnp.minimum(i, n_total - 1)
        return k // n_tiles, k % n_tiles

    return mm, fin


def _even_out_kernel(ctx_ref, x_ref, mod_ref, yna_ref, yf_ref, yb_ref, rg_ref, wo_ref, nmlp_ref, w1_ref, w2_ref,
                     o_ref, slots_ref, h2_ref, *, n_tiles):
    def step(cur_ref, nxt_ref):
        m = mod_ref[0, 0]
        is_ctx = (jnp.maximum(pl.program_id(0) - 1, 0) % n_tiles) == 0
        def finish_head(hh):
            sl = slice(hh * RET_DIM, (hh + 1) * RET_DIM)
            yh = yf_ref[0, :, sl] + yb_ref[0, :, sl]
            dlt = yh - jnp.mean(yh, axis=-1, keepdims=True)
            yn = dlt * lax.rsqrt(jnp.mean(dlt * dlt, axis=-1, keepdims=True) + EPS)
            nxt_ref[:, sl] = (yn * rg_ref[0, :, sl].astype(F32)).astype(BF16)
            return _zero_token(yn)

        mix = _mm(yna_ref[0], wo_ref[RET_W:RET_W + NA_W, :]) + _mm(cur_ref[...], wo_ref[0:RET_W, :])
        x1 = jnp.where(is_ctx, ctx_ref[0], x_ref[0]) + m[2:3] * mix
        o_ref[0] = _mlp_tail(x1, m, nmlp_ref[...], w1_ref, w2_ref, h2_ref,
                             [functools.partial(finish_head, hh) for hh in range(RET_HEADS)])

    _two_phase(step, slots_ref)


def _even_out(ctx, x, modsel, yf, yb, rg, yna, wo, nmlp, w1, w2, layer):
    b = x.shape[0]
    t = ctx.shape[1] + x.shape[1]
    nt = t // TM
    mm, fin = _two_phase_maps(nt, b * nt)

    def mm_tok(w):
        return pl.BlockSpec((1, TM, w), lambda i: (mm(i)[0], mm(i)[1], 0))

    def fin_tok(w):
        return pl.BlockSpec((1, TM, w), lambda i: (fin(i)[0], fin(i)[1], 0))

    return pl.pallas_call(
        functools.partial(_even_out_kernel, n_tiles=nt),
        grid=(b * nt + 1,),
        in_specs=[pl.BlockSpec((1, TM, D_MODEL), lambda i: (mm(i)[0], 0, 0)),
                  pl.BlockSpec((1, TM, D_MODEL), lambda i: (mm(i)[0], jnp.maximum(mm(i)[1] - 1, 0), 0)),
                  pl.BlockSpec((1, 1, 6, D_MODEL), lambda i: (mm(i)[0], jnp.minimum(mm(i)[1], 1), 0, 0)),
                  mm_tok(NA_W), fin_tok(RET_W), fin_tok(RET_W), fin_tok(RET_W),
                  _resident(wo.shape), _resident((1, D_MODEL)), _resident_layer(w1.shape, layer),
                  _resident_layer(w2.shape, layer)],
        out_specs=mm_tok(D_MODEL),
        out_shape=jax.ShapeDtypeStruct((b, t, D_MODEL), F32),
        scratch_shapes=[pltpu.VMEM((2, TM, RET_W), BF16), pltpu.VMEM((TM, D_MODEL), BF16)],
        compiler_params=_params(1),
        name="even_out_mlp",
    )(ctx, x, modsel, yna, yf, yb, rg, wo, nmlp.reshape(1, D_MODEL), w1, w2)


def _conv_perm():
    span = TM // SUBLANES
    rho = np.arange(TM)
    p = np.zeros((TM, TM), np.float32)
    p[rho, (rho % SUBLANES) * span + rho // SUBLANES] = 1.0
    return jnp.asarray(p, BF16)


def _odd_in_kernel(x_ref, mod_ref, g_ref, w_ref, wdt_ref, dtb_ref, perm_ref, z_ref, xbc_ref, dt_ref):
    m = mod_ref[0, 0]
    h = _norm_mod(x_ref[0], g_ref[...], m[1:2], m[0:1]).astype(BF16)
    cw = 512
    for j in range(SSM_INNER // cw):
        z_ref[0, :, j * cw:(j + 1) * cw] = _silu(_mm(h, w_ref[:, j * cw:(j + 1) * cw])).astype(BF16)
    hp = _mm(perm_ref[...], h).astype(BF16)
    for j in range(SSM_CONV_CH // cw):
        c0 = SSM_INNER + j * cw
        xbc_ref[0, :, j * cw:(j + 1) * cw] = _mm(hp, w_ref[:, c0:c0 + cw]).astype(BF16)
    raw = _mm(h, wdt_ref[...]) + dtb_ref[...]
    dt_ref[0] = jnp.maximum(raw, 0.0) + jnp.log1p(jnp.exp(-jnp.abs(raw)))


def _odd_in(xs, modsel, g, w_main, w_dt, dt_bias):
    b, t, _ = xs.shape
    tok = lambda w: pl.BlockSpec((1, TM, w), lambda bb, tt: (bb, tt, 0))
    return pl.pallas_call(
        _odd_in_kernel,
        grid=(b, t // TM),
        in_specs=[tok(D_MODEL), _mod_spec(), _resident((1, D_MODEL)), _resident(w_main.shape),
                  _resident(w_dt.shape), _resident((1, LANES)), _resident((TM, TM))],
        out_specs=[tok(SSM_INNER), tok(SSM_CONV_CH), tok(LANES)],
        out_shape=[jax.ShapeDtypeStruct((b, t, SSM_INNER), BF16),
                   jax.ShapeDtypeStruct((b, t, SSM_CONV_CH), BF16),
                   jax.ShapeDtypeStruct((b, t, LANES), F32)],
        compiler_params=_params(2),
        name="odd_in",
    )(xs, modsel, g.reshape(1, D_MODEL), w_main, w_dt, dt_bias, _conv_perm())


def _conv_kernel(main_ref, prev_ref, next_ref, w_ref, b_ref, unperm_ref, o_ref, act_ref, *, n_tiles):
    t = pl.program_id(1)
    has_prev = (t >= 2).astype(F32)
    has_next = jnp.logical_and(t >= 1, t < n_tiles - 1).astype(F32)
    span = TM // SUBLANES
    half = SSM_CONV // 2
    sub = lax.broadcasted_iota(jnp.int32, (SUBLANES, LANES), 0)

    def lane_tile(j, carry):
        lanes = pl.ds(pl.multiple_of(j * LANES, LANES), LANES)
        xm = main_ref[0, :, lanes].astype(F32)
        pv = prev_ref[0, :, lanes].astype(F32) * has_prev
        nx = next_ref[0, :, lanes].astype(F32) * has_next
        reg = {m: xm[m * SUBLANES:(m + 1) * SUBLANES] for m in range(span)}
        for i in range(1, half + 1):
            row = HALO - 1 - (i - 1) * SUBLANES
            edge = jnp.broadcast_to(pv[row:row + 1], (SUBLANES, LANES))
            reg[-i] = jnp.where(sub == 0, edge, pltpu.roll(reg[span - i], 1, 0))
            row = (i - 1) * SUBLANES
            edge = jnp.broadcast_to(nx[row:row + 1], (SUBLANES, LANES))
            reg[span + i - 1] = jnp.where(sub == SUBLANES - 1, edge, pltpu.roll(reg[i - 1], SUBLANES - 1, 0))
        wk = [jnp.broadcast_to(w_ref[k:k + 1, lanes], (SUBLANES, LANES)) for k in range(SSM_CONV)]
        bias = jnp.broadcast_to(b_ref[:, lanes], (SUBLANES, LANES))
        acts = []
        for m in range(span):
            acc = bias
            for k in range(SSM_CONV):
                acc = acc + wk[k] * reg[m + k - half]
            acts.append(acc * jnp.tanh(acc) + acc)
        act_ref[:, lanes] = jnp.concatenate(acts, axis=0).astype(BF16)
        return carry

    lax.fori_loop(0, SSM_CONV_CH // LANES, lane_tile, 0, unroll=4)
    o_ref[0] = _mm(unperm_ref[...], act_ref[...]).astype(BF16)


def _conv(xbc, w, bias, ctx_len):
    b, t, ch = xbc.shape
    assert ctx_len == TM and TM % HALO == 0 and HALO >= (SSM_CONV // 2) * SUBLANES
    nt = t // TM
    per = TM // HALO
    main = pl.BlockSpec((1, TM, ch), lambda bb, tt: (bb, tt, 0))
    prev = pl.BlockSpec((1, HALO, ch), lambda bb, tt: (bb, jnp.maximum(tt * per - 1, 0), 0))
    nxt = pl.BlockSpec((1, HALO, ch), lambda bb, tt: (bb, jnp.minimum((tt + 1) * per, nt * per - 1), 0))
    return pl.pallas_call(
        functools.partial(_conv_kernel, n_tiles=nt),
        grid=(b, nt),
        in_specs=[main, prev, nxt, _resident(w.shape), _resident((1, ch)), _resident((TM, TM))],
        out_specs=main,
        out_shape=jax.ShapeDtypeStruct((b, t, ch), BF16),
        scratch_shapes=[pltpu.VMEM((TM, ch), BF16)],
        compiler_params=_params(2),
        name="dwconv_silu",
    )(xbc, xbc, xbc, w, bias.reshape(1, ch), _conv_perm().T)


def _dot_exact_lhs(a16, x):
    hi = x.astype(BF16)
    r1 = x - hi.astype(F32)
    mid = r1.astype(BF16)
    lo = (r1 - mid.astype(F32)).astype(BF16)
    return _mm(a16, hi) + _mm(a16, mid) + _mm(a16, lo)


def _ssd_kernel(xf_ref, xb_ref, dtf_ref, dtb_ref, a_ref, tri_ref, yf_ref, yb_ref, h_ref, row_ref, *, n_ctx):
    @pl.when(pl.program_id(1) == 0)
    def _():
        h_ref[...] = jnp.zeros_like(h_ref)

    left = lax.broadcasted_iota(jnp.int32, (1, LANES), 1) < SSM_HEAD_DIM
    keep_l = jnp.where(left, jnp.uint32(0xFFFFFFFF), jnp.uint32(0))
    keep_r = jnp.where(left, jnp.uint32(0), jnp.uint32(0xFFFFFFFF))
    ii = lax.broadcasted_iota(jnp.int32, (CHUNK, CHUNK), 0)
    jj = lax.broadcasted_iota(jnp.int32, (CHUNK, CHUNK), 1)
    b_off = SSM_INNER
    c_off = SSM_INNER + SSM_GN

    directions = ((xf_ref, dtf_ref, yf_ref), (xb_ref, dtb_ref, yb_ref))
    n_sub = xf_ref.shape[1] // CHUNK
    rows_of = [[slice(c * CHUNK, (c + 1) * CHUNK) for c in (sub, n_sub - 1 - sub)] for sub in range(n_sub)]

    def scan_block(with_y):
        decays = {}
        for sub in range(n_sub):
            for d, (_, dt_ref, _) in enumerate(directions):
                dtc = dt_ref[0, rows_of[sub][d], :]
                acum = _dot_exact_lhs(tri_ref[d], dtc * a_ref[...]) * LOG2E
                acum_t = acum.T
                dt_t = dtc.T
                last = acum[CHUNK - 1:CHUNK, :] if d == 0 else acum[0:1, :]
                last_t = acum_t[:, CHUNK - 1:CHUNK] if d == 0 else acum_t[:, 0:1]
                if with_y:
                    row_ref[sub, d, 0] = acum_t - jnp.log2(dt_t)
                row_ref[sub, d, 1] = dt_t * jnp.exp2(last_t - acum_t)
                decays[sub, d] = (acum, jnp.exp2(last))

        for sub in range(n_sub):
            for d, (x_ref, _, y_ref) in enumerate(directions):
                rows = rows_of[sub][d]
                acum, elast = decays[sub, d]
                mask = (ii >= jj) if d == 0 else (jj >= ii)
                if not with_y:
                    y_ref[0, rows, :] = jnp.zeros((CHUNK, SSM_INNER), F32)

                for g in range(SSM_GROUPS):
                    bg = x_ref[0, rows, b_off + g * SSM_STATE:b_off + (g + 1) * SSM_STATE]
                    bg_t = bg.astype(F32).T
                    h_t = h_ref[d, g]
                    if with_y:
                        cg = x_ref[0, rows, c_off + g * SSM_STATE:c_off + (g + 1) * SSM_STATE]
                        cb = _nt(cg, bg)
                        y_int = _mm(cg, h_t.astype(BF16))
                    for hp in range(SSM_HPG // 2):
                        c0 = d * SSM_HEADS + g * SSM_HPG + 2 * hp
                        ch = slice(g * SSM_GW + hp * LANES, g * SSM_GW + (hp + 1) * LANES)
                        hl = slice(hp * LANES, (hp + 1) * LANES)
                        xu = pltpu.bitcast(x_ref[0, rows, ch], jnp.uint32)
                        xbd = jnp.concatenate([pltpu.bitcast(xu & keep_l, BF16), pltpu.bitcast(xu & keep_r, BF16)],
                                              axis=0)
                        intra, upd, eac = [], [], []
                        for e in range(2):
                            c = c0 + e
                            upd.append((bg_t * row_ref[sub, d, 1, c:c + 1, :]).astype(BF16))
                            if with_y:
                                acol = jnp.broadcast_to(acum[:, c:c + 1], (CHUNK, CHUNK))
                                lmat = jnp.exp2(jnp.where(mask, acol - row_ref[sub, d, 0, c:c + 1, :], NEG))
                                intra.append((lmat * cb).astype(BF16))
                                eac.append(jnp.exp2(acol))
                        el = jnp.where(left, jnp.broadcast_to(elast[:, c0:c0 + 1], (1, LANES)),
                                       jnp.broadcast_to(elast[:, c0 + 1:c0 + 2], (1, LANES)))
                        if with_y:
                            lhs = jnp.concatenate([jnp.concatenate(intra, axis=1), jnp.concatenate(upd, axis=1)],
                                                  axis=0)
                            r = _mm(lhs, xbd)
                            y_ref[0, rows, ch] = r[:CHUNK] + y_int[:, hl] * jnp.where(left, eac[0], eac[1])
                            h_ref[d, g, :, hl] = el * h_t[:, hl] + r[CHUNK:]
                        else:
                            h_ref[d, g, :, hl] = el * h_t[:, hl] + _mm(jnp.concatenate(upd, axis=1), xbd)

    is_ctx = pl.program_id(1) < n_ctx

    @pl.when(is_ctx)
    def _():
        scan_block(False)

    @pl.when(jnp.logical_not(is_ctx))
    def _():
        scan_block(True)


def _ssd(xc, dt, a_row, ctx_len):
    b, t, ch = xc.shape
    ns = t // SCAN_BLOCK
    fwd, bwd = _scan_orders(ns, ctx_len // SCAN_BLOCK)
    idx = np.arange(CHUNK)
    tri = jnp.asarray(np.stack([idx[:, None] >= idx[None, :], idx[:, None] <= idx[None, :]]), BF16)

    def spec(order, w):
        return pl.BlockSpec((1, SCAN_BLOCK, w), lambda bb, i: (bb, order(i), 0))

    return pl.pallas_call(
        functools.partial(_ssd_kernel, n_ctx=ctx_len // SCAN_BLOCK),
        grid=(b, ns),
        in_specs=[spec(fwd, ch), spec(bwd, ch), spec(fwd, LANES), spec(bwd, LANES),
                  _resident((1, LANES)), _resident((2, CHUNK, CHUNK))],
        out_specs=[spec(fwd, SSM_INNER), spec(bwd, SSM_INNER)],
        out_shape=[jax.ShapeDtypeStruct((b, t, SSM_INNER), F32)] * 2,
        scratch_shapes=[pltpu.VMEM((2, SSM_GROUPS, SSM_STATE, SSM_GW), F32),
                        pltpu.VMEM((SCAN_BLOCK // CHUNK, 2, 2, CHUNK, CHUNK), F32)],
        compiler_params=_params(2),
        name="ssd",
    )(xc, xc, dt, dt, a_row, tri)


def _odd_out_kernel(x_ref, mod_ref, yf_ref, yb_ref, xc_ref, z_ref, dsk_ref, nssm_ref, wo_ref,
                    nmlp_ref, w1_ref, w2_ref, nfin_ref, o_ref, slots_ref, h2_ref):
    def step(cur_ref, nxt_ref):
        m = mod_ref[0, 0]
        def finish_rows(r):
            rows = slice(r * FIN_ROWS, (r + 1) * FIN_ROWS)
            y = yf_ref[0, rows, :] + yb_ref[0, rows, :] + dsk_ref[...] * xc_ref[0, rows, :].astype(F32)
            y = y * z_ref[0, rows, :].astype(F32)
            y = y * lax.rsqrt(jnp.mean(y * y, axis=-1, keepdims=True) + EPS) * nssm_ref[...]
            nxt_ref[rows, :] = y.astype(BF16)
            return _zero_token(y)

        x1 = x_ref[0] + m[2:3] * _mm(cur_ref[...], wo_ref[...])
        x2 = _mlp_tail(x1, m, nmlp_ref[...], w1_ref, w2_ref, h2_ref,
                       [functools.partial(finish_rows, r) for r in range(TM // FIN_ROWS)])
        o_ref[0] = x2 * lax.rsqrt(jnp.mean(x2 * x2, axis=-1, keepdims=True) + EPS) * nfin_ref[...]

    _two_phase(step, slots_ref)


def _odd_out(xs, modsel, yf, yb, xc, z, dsk, nssm, wo, nmlp, w1, w2, layer, nfin, ctx_len):
    b, t, _ = xs.shape
    skip = ctx_len // TM
    nt = t // TM - skip
    mm, fin = _two_phase_maps(nt, b * nt)
    fin_tok = lambda w: pl.BlockSpec((1, TM, w), lambda i: (fin(i)[0], fin(i)[1] + skip, 0))
    vec = lambda w: _resident((1, w))
    return pl.pallas_call(
        _odd_out_kernel,
        grid=(b * nt + 1,),
        in_specs=[pl.BlockSpec((1, TM, D_MODEL), lambda i: (mm(i)[0], mm(i)[1] + skip, 0)),
                  pl.BlockSpec((1, 1, 6, D_MODEL), lambda i: (mm(i)[0], 1, 0, 0)),
                  fin_tok(SSM_INNER), fin_tok(SSM_INNER), fin_tok(SSM_INNER), fin_tok(SSM_INNER),
                  vec(SSM_INNER), vec(SSM_INNER), _resident(wo.shape), vec(D_MODEL),
                  _resident_layer(w1.shape, layer), _resident_layer(w2.shape, layer), vec(D_MODEL)],
        out_specs=pl.BlockSpec((1, TM, D_MODEL), lambda i: (mm(i)[0], mm(i)[1], 0)),
        out_shape=jax.ShapeDtypeStruct((b, t - ctx_len, D_MODEL), F32),
        scratch_shapes=[pltpu.VMEM((2, TM, SSM_INNER), BF16), pltpu.VMEM((TM, D_MODEL), BF16)],
        compiler_params=_params(1),
        name="odd_out_mlp",
    )(xs, modsel, yf, yb, xc, z, dsk.reshape(1, -1), nssm.reshape(1, -1), wo, nmlp.reshape(1, -1),
      w1, w2, nfin.reshape(1, -1))


def kernel(x, c, ctx, c_ctx, w_mod, b_mod, norm_mix, norm_mlp, w_mlp_in, w_mlp_out, w_in_even, w_out_even,
           ret_decay_logit, na_rpb, w_in_odd, conv_w, conv_b, dt_bias, a_log, d_skip, ssm_norm, w_out_odd,
           norm_final):
    assert w_mod.shape[0] == DEPTH == 2 and x.shape[2] == D_MODEL
    ctx_len = ctx.shape[1]
    seq = x.shape[1]
    assert ctx_len == TM and seq % (2 * GRID_W) == 0

    modsel = _modulation(c, c_ctx, w_mod, b_mod)
    w1 = w_mlp_in.astype(BF16)
    w2 = w_mlp_out.astype(BF16)

    n_tok = 4 * RET_W + 2 * NA_W
    w_even = w_in_even[0].astype(BF16)
    rq, rk, rv, rg, nq, nk, nvt = _even_in(ctx, x, modsel[0], norm_mix[0], w_even, w_even[:, n_tok:].T,
                                           _rope_tables(seq, ctx_len))
    yf, yb = _retention(rq, rk, rv, _ret_tables(ret_decay_logit[0]), ctx_len)
    yna = _na(nq, nk, nvt, _na_bias_tables(na_rpb[0]), ctx_len)
    xs = _even_out(ctx, x, modsel[0], yf, yb, rg, yna, w_out_even[0].astype(BF16), norm_mlp[0], w1, w2, 0)

    n_main = SSM_INNER + SSM_CONV_CH
    wi = w_in_odd[0]
    w_dt = jnp.pad(wi[:, n_main:], ((0, 0), (0, LANES - 2 * SSM_HEADS))).astype(BF16)
    pad_row = lambda v: jnp.pad(v.astype(F32).reshape(1, -1), ((0, 0), (0, LANES - 2 * SSM_HEADS)))
    z, xbc, dt = _odd_in(xs, modsel[1], norm_mix[1], wi[:, :n_main].astype(BF16), w_dt, pad_row(dt_bias[0]))
    xc = _conv(xbc, 0.5 * conv_w[0].astype(F32), 0.5 * conv_b[0].astype(F32), ctx_len)
    yf, yb = _ssd(xc, dt, pad_row(-jnp.exp(a_log[0].astype(F32))), ctx_len)
    dsk = jnp.repeat(d_skip[0].astype(F32), SSM_HEAD_DIM)
    return _odd_out(xs, modsel[1], yf, yb, xc, z, dsk, ssm_norm[0], w_out_odd[0].astype(BF16), norm_mlp[1],
                    w1, w2, 1, norm_final, ctx_len)
```

```python
import functools

import numpy as np
import jax
import jax.numpy as jnp
from jax import lax
from jax.experimental import pallas as pl
from jax.experimental.pallas import tpu as pltpu

F32 = jnp.float32
BF16 = jnp.bfloat16

D_MODEL = 1024
D_FF = 4 * D_MODEL
DEPTH = 2
GRID_W = 64
EPS = 1e-6
ROPE_BASE = 10000.0

RET_HEADS = 4
RET_DIM = 128
RET_W = RET_HEADS * RET_DIM

NA_HEADS = 8
NA_DIM = 64
NA_W = NA_HEADS * NA_DIM
NA_WIN_R = 8
NA_WIN_C = 16

SSM_INNER = 2 * D_MODEL
SSM_HEAD_DIM = 64
SSM_HEADS = SSM_INNER // SSM_HEAD_DIM
SSM_GROUPS = 4
SSM_HPG = SSM_HEADS // SSM_GROUPS
SSM_STATE = 128
SSM_CONV = 7
SSM_GN = SSM_GROUPS * SSM_STATE
SSM_CONV_CH = SSM_INNER + 2 * SSM_GN
SSM_GW = SSM_HPG * SSM_HEAD_DIM

LANES = 128
SUBLANES = 8
TM = 256
CHUNK = 128
SCAN_BLOCK = 2 * CHUNK
FIN_ROWS = 32
HALO = 32
LOG2E = 1.4426950408889634
NEG = -1e30
NA_KROWS = 10
VMEM_LIMIT = 56 * 1024 * 1024


def _params(n_axes):
    return pltpu.CompilerParams(dimension_semantics=("arbitrary",) * n_axes, vmem_limit_bytes=VMEM_LIMIT)


def _resident(shape):
    nd = len(shape)
    return pl.BlockSpec(shape, lambda *_: (0,) * nd, pipeline_mode=pl.Buffered(1))


def _resident_layer(shape, layer):
    nd = len(shape)
    return pl.BlockSpec((None,) + tuple(shape[1:]), lambda *_: (layer,) + (0,) * (nd - 1),
                        pipeline_mode=pl.Buffered(1))


def _nt(a, b):
    return lax.dot_general(a, b, (((1,), (1,)), ((), ())), preferred_element_type=F32)


def _tn(a, b):
    return lax.dot_general(a, b, (((0,), (0,)), ((), ())), preferred_element_type=F32)


def _mm(a, b):
    return jnp.dot(a, b, preferred_element_type=F32)


def _silu(v):
    return v * jax.nn.sigmoid(v)


def _norm_mod(x, g, sc, sh):
    ms = jnp.mean(x * x, axis=-1, keepdims=True)
    return (x * lax.rsqrt(ms + EPS) * g) * (1.0 + sc) + sh


def _mod_kernel(cc_ref, w_ref, b_ref, o_ref):
    s = _silu(cc_ref[...])
    o_ref[0] = _mm(s.astype(BF16), w_ref[0].astype(BF16)) + b_ref[0]


def _modulation(c, c_ctx, w_mod, b_mod):
    b = c.shape[0]
    rows = -(-(b + 1) // 8) * 8
    cc = jnp.zeros((rows, D_MODEL), F32).at[:b].set(c).at[b].set(c_ctx)
    tn = 1536
    out = pl.pallas_call(
        _mod_kernel,
        grid=(DEPTH, 6 * D_MODEL // tn),
        in_specs=[pl.BlockSpec((rows, D_MODEL), lambda l, j: (0, 0)),
                  pl.BlockSpec((1, D_MODEL, tn), lambda l, j: (l, 0, j)),
                  pl.BlockSpec((1, 1, tn), lambda l, j: (l, 0, j))],
        out_specs=pl.BlockSpec((1, rows, tn), lambda l, j: (l, 0, j)),
        out_shape=jax.ShapeDtypeStruct((DEPTH, rows, 6 * D_MODEL), F32),
        compiler_params=_params(2),
        name="modulation",
    )(cc, w_mod, b_mod.reshape(DEPTH, 1, 6 * D_MODEL))
    mod = out.reshape(DEPTH, rows, 6, D_MODEL)
    return jnp.stack([jnp.broadcast_to(mod[:, b:b + 1], (DEPTH, b, 6, D_MODEL)), mod[:, :b]], axis=2)


def _mod_spec():
    return pl.BlockSpec((1, 1, 6, D_MODEL), lambda b, t: (b, jnp.minimum(t, 1), 0, 0))


def _rope_tables(seq, ctx_len):
    pos = np.arange(seq)
    nf = RET_DIM // 4
    inv = (ROPE_BASE ** (-np.arange(nf, dtype=np.float32) / nf)).astype(np.float32)

    def cs(p):
        ang = p.astype(np.float32)[:, None] * inv[None, :]
        return np.cos(ang), np.sin(ang)

    cr, sr = cs(pos // GRID_W)
    cc, sc = cs(pos % GRID_W)
    zero = np.zeros_like(sr)
    cos = np.concatenate([cr, cr, cc, cc], axis=1)
    sin_a = np.concatenate([-sr, zero, -sc, zero], axis=1)
    sin_b = np.concatenate([zero, sr, zero, sc], axis=1)
    ident = np.ones((ctx_len, RET_DIM), np.float32)
    nul = np.zeros((ctx_len, RET_DIM), np.float32)
    return tuple(jnp.asarray(np.concatenate(parts, axis=0), F32)
                 for parts in ([ident, cos], [nul, sin_a], [nul, sin_b]))


def _stream_tile(ctx_ref, x_ref):
    return jnp.where(pl.program_id(1) == 0, ctx_ref[0], x_ref[0])


def _stream_specs():
    return [pl.BlockSpec((1, TM, D_MODEL), lambda bb, tt: (bb, 0, 0)),
            pl.BlockSpec((1, TM, D_MODEL), lambda bb, tt: (bb, jnp.maximum(tt - 1, 0), 0))]


def _even_in_kernel(ctx_ref, x_ref, mod_ref, g_ref, w_ref, wvt_ref, cos_ref, sa_ref, sb_ref,
                    rq_ref, rk_ref, rv_ref, rg_ref, nq_ref, nk_ref, nvt_ref):
    m = mod_ref[0, 0]
    h = _norm_mod(_stream_tile(ctx_ref, x_ref), g_ref[...], m[1:2], m[0:1]).astype(BF16)
    cos, sa, sb = cos_ref[...], sa_ref[...], sb_ref[...]

    def proj(j):
        return _mm(h, w_ref[:, j * RET_W:(j + 1) * RET_W])

    def rope_store(o_ref, r, scale):
        for hh in range(RET_HEADS):
            t = r[:, hh * RET_DIM:(hh + 1) * RET_DIM]
            o = t * cos + pltpu.roll(t, RET_DIM - 32, 1) * sa + pltpu.roll(t, 32, 1) * sb
            o_ref[0, :, hh * RET_DIM:(hh + 1) * RET_DIM] = (o * scale).astype(BF16)

    p = [proj(j) for j in range(6)]
    vt = _nt(wvt_ref[...], h)
    rope_store(rq_ref, p[0], 1.0)
    rope_store(rk_ref, p[1], RET_DIM ** -0.5)
    rv_ref[0] = p[2].astype(BF16)
    rg_ref[0] = _silu(p[3]).astype(BF16)
    nq_ref[0] = (p[4] * (NA_DIM ** -0.5 * LOG2E)).astype(BF16)
    nk_ref[0] = p[5].astype(BF16)
    nvt_ref[0] = vt.astype(BF16)


def _even_in(ctx, x, modsel, g, w, w_vt, tables):
    b = x.shape[0]
    t = ctx.shape[1] + x.shape[1]
    tab = pl.BlockSpec((TM, RET_DIM), lambda bb, tt: (tt, 0))
    out = pl.BlockSpec((1, TM, RET_W), lambda bb, tt: (bb, tt, 0))
    return pl.pallas_call(
        _even_in_kernel,
        grid=(b, t // TM),
        in_specs=_stream_specs() + [_mod_spec(), _resident((1, D_MODEL)), _resident(w.shape),
                                    _resident(w_vt.shape), tab, tab, tab],
        out_specs=[out] * 6 + [pl.BlockSpec((1, NA_W, TM), lambda bb, tt: (bb, 0, tt))],
        out_shape=[jax.ShapeDtypeStruct((b, t, RET_W), BF16)] * 6 + [jax.ShapeDtypeStruct((b, NA_W, t), BF16)],
        compiler_params=_params(2),
        name="even_in",
    )(ctx, x, modsel, g.reshape(1, D_MODEL), w, w_vt, *tables)


def _ret_tables(decay_logit):
    log_g = jax.nn.log_sigmoid(decay_logit.astype(F32))
    idx = jnp.arange(CHUNK, dtype=F32)
    diff = idx[:, None] - idx[None, :]
    ones = jnp.ones((CHUNK, CHUNK), F32)

    def one(lg, d):
        dd = diff if d == 0 else -diff
        dm = jnp.where(dd >= 0, jnp.exp(jnp.maximum(dd, 0.0) * lg), 0.0)
        qe = (idx + 1.0) if d == 0 else (CHUNK - idx)
        ke = (CHUNK - 1.0 - idx) if d == 0 else idx
        return jnp.stack([dm, jnp.exp(qe * lg)[:, None] * ones, jnp.exp(ke * lg)[:, None] * ones,
                          jnp.exp(CHUNK * lg) * ones])

    return jnp.stack([jnp.stack([one(log_g[d, h], d) for h in range(RET_HEADS)]) for d in range(2)])


def _ret_kernel(qf_ref, kf_ref, vf_ref, qb_ref, kb_ref, vb_ref, dec_ref, yf_ref, yb_ref, s_ref):
    @pl.when(pl.program_id(1) == 0)
    def _():
        s_ref[...] = jnp.zeros_like(s_ref)

    refs = ((qf_ref, kf_ref, vf_ref, yf_ref), (qb_ref, kb_ref, vb_ref, yb_ref))
    chains = [(d, hh, slice(hh * RET_DIM, (hh + 1) * RET_DIM)) for d in range(2) for hh in range(RET_HEADS)]
    n_sub = qf_ref.shape[1] // CHUNK
    for sub in range(n_sub):
        rows = [slice(c * CHUNK, (c + 1) * CHUNK) for c in (sub, n_sub - 1 - sub)]
        stage1 = []
        for d, hh, sl in chains:
            q_ref, k_ref, _, _ = refs[d]
            q = q_ref[0, rows[d], sl]
            stage1.append((_nt(q, k_ref[0, rows[d], sl]), _mm(q, s_ref[d, hh].astype(BF16))))
        for (d, hh, sl), (att, inter) in zip(chains, stage1):
            _, _, v_ref, y_ref = refs[d]
            y_ref[0, rows[d], sl] = (_mm((att * dec_ref[d, hh, 0]).astype(BF16), v_ref[0, rows[d], sl])
                                     + inter * dec_ref[d, hh, 1])
        for d, hh, sl in chains:
            _, k_ref, v_ref, _ = refs[d]
            kk = (k_ref[0, rows[d], sl].astype(F32) * dec_ref[d, hh, 2]).astype(BF16)
            s_ref[d, hh] = dec_ref[d, hh, 3] * s_ref[d, hh] + _tn(kk, v_ref[0, rows[d], sl])


def _scan_orders(n_steps, n_ctx):
    fwd = lambda i: i
    bwd = lambda i: jnp.where(i < n_ctx, n_ctx - 1 - i, n_steps + n_ctx - 1 - i)
    return fwd, bwd


def _retention(rq, rk, rv, dec, ctx_len):
    b, t, _ = rq.shape
    ns = t // SCAN_BLOCK
    fwd, bwd = _scan_orders(ns, ctx_len // SCAN_BLOCK)

    def spec(order):
        return pl.BlockSpec((1, SCAN_BLOCK, RET_W), lambda bb, i: (bb, order(i), 0))

    return pl.pallas_call(
        _ret_kernel,
        grid=(b, ns),
        in_specs=[spec(fwd)] * 3 + [spec(bwd)] * 3 + [_resident(dec.shape)],
        out_specs=[spec(fwd), spec(bwd)],
        out_shape=[jax.ShapeDtypeStruct((b, t, RET_W), F32)] * 2,
        scratch_shapes=[pltpu.VMEM((2, RET_HEADS, RET_DIM, RET_DIM), F32)],
        compiler_params=_params(2),
        name="retention",
    )(rq, rk, rv, rq, rk, rv, dec)


def _na_bias_tables(rpb):
    qc = np.arange(GRID_W)[:, None]
    kc = np.arange(GRID_W)[None, :]
    cstart = np.clip(qc - NA_WIN_C // 2, 0, GRID_W - NA_WIN_C)
    inwin = (kc >= cstart) & (kc < cstart + NA_WIN_C)
    dcol = np.clip(kc - qc, -(NA_WIN_C - 1), NA_WIN_C - 1) + (NA_WIN_C - 1)
    onehot = np.zeros((GRID_W, GRID_W, 2 * NA_WIN_C - 1), np.float32)
    onehot[qc, kc, dcol] = 1.0
    t = jnp.einsum("hrd,qkd->hrkq", rpb.astype(F32), jnp.asarray(onehot), precision=lax.Precision.HIGHEST)
    t = jnp.where(jnp.asarray(inwin.T), t * LOG2E, NEG)
    pad = jnp.full((NA_HEADS, 2, GRID_W, GRID_W), NEG, F32)
    tp = jnp.concatenate([pad, t, pad], axis=1)
    n = 2 * NA_WIN_R + 1
    top = jnp.concatenate([tp[:, 1:1 + n], tp[:, 0:n]], axis=-1)
    bot = jnp.concatenate([tp[:, 2:2 + n], tp[:, 1:1 + n]], axis=-1)
    return jnp.concatenate([top, bot], axis=-2)


def _na_kernel(q_ref, k_ref, vt_ref, tbl_ref, o_ref, *, rows, ctx_len):
    s = pl.program_id(1)
    n_ctx = ctx_len // SCAN_BLOCK
    n_unit = SCAN_BLOCK // CHUNK
    left = lax.broadcasted_iota(jnp.int32, (1, LANES), 1) < NA_DIM
    top_half = lax.broadcasted_iota(jnp.int32, (CHUNK, LANES), 0) < GRID_W
    left_half = lax.broadcasted_iota(jnp.int32, (CHUNK, LANES), 1) < GRID_W
    n_loc = NA_KROWS * GRID_W
    n_blk = NA_KROWS // 2

    def attend(local_of_unit):
        items = [(u, hp, slice(u * CHUNK, (u + 1) * CHUNK), slice(hp * LANES, (hp + 1) * LANES))
                 for u in range(n_unit) for hp in range(NA_HEADS // 2)]
        scores = []
        for u, hp, qrows, sl in items:
            qp = q_ref[0, qrows, sl]
            zero = jnp.zeros_like(qp)
            qboth = jnp.concatenate([jnp.where(left, qp, zero), jnp.where(left, zero, qp)], axis=0)
            s_ctx = _nt(k_ref[0, 0:ctx_len, sl], qboth)
            s_loc = None
            if local_of_unit[u] is not None:
                off, idx, pen = local_of_unit[u]
                s_loc = _nt(k_ref[0, pl.ds(off, n_loc), sl], qboth)
                s_loc = s_loc + jnp.concatenate(
                    [jnp.concatenate([tbl_ref[2 * hp, idx[m]] + pen[m], tbl_ref[2 * hp + 1, idx[m]] + pen[m]],
                                     axis=1) for m in range(n_blk)], axis=0)
            scores.append((s_ctx, s_loc))
        probs = []
        for s_ctx, s_loc in scores:
            mx = jnp.max(s_ctx, axis=0, keepdims=True)
            if s_loc is not None:
                mx = jnp.maximum(mx, jnp.max(s_loc, axis=0, keepdims=True))
            p_ctx = jnp.exp2(s_ctx - mx)
            den = jnp.sum(p_ctx, axis=0, keepdims=True)
            p_loc = None
            if s_loc is not None:
                p_loc = jnp.exp2(s_loc - mx)
                den = den + jnp.sum(p_loc, axis=0, keepdims=True)
                p_loc = p_loc.astype(BF16)
            probs.append((p_ctx.astype(BF16), p_loc, den))
        for (u, hp, qrows, sl), (p_ctx, p_loc, den) in zip(items, probs):
            ot = _mm(vt_ref[0, sl, 0:ctx_len], p_ctx)
            if p_loc is not None:
                ot = ot + _mm(vt_ref[0, sl, pl.ds(local_of_unit[u][0], n_loc)], p_loc)
            ot = ot / den
            pair_t = jnp.concatenate([ot[0:NA_DIM, 0:CHUNK], ot[NA_DIM:, CHUNK:]], axis=0)
            o_ref[0, qrows, sl] = pair_t.T.astype(BF16)

    @pl.when(s < n_ctx)
    def _():
        attend([None] * n_unit)

    @pl.when(s >= n_ctx)
    def _():
        def row_pen(kr, r):
            rs = jnp.clip(r - NA_WIN_R // 2, 0, rows - NA_WIN_R)
            return jnp.where(jnp.logical_and(kr >= rs, kr < rs + NA_WIN_R), 0.0, NEG)

        local_of_unit = []
        for u in range(n_unit):
            r0 = 2 * ((s - n_ctx) * n_unit + u)
            kb = jnp.clip(r0 - NA_WIN_R // 2, 0, rows - NA_KROWS)
            off = pl.multiple_of(ctx_len + kb * GRID_W, LANES)
            idx, pen = [], []
            for m in range(n_blk):
                kp = kb + 2 * m
                idx.append(kp - r0 + NA_WIN_R)
                pen.append(jnp.where(top_half,
                                     jnp.where(left_half, row_pen(kp, r0), row_pen(kp, r0 + 1)),
                                     jnp.where(left_half, row_pen(kp + 1, r0), row_pen(kp + 1, r0 + 1))))
            local_of_unit.append((off, idx, pen))
        attend(local_of_unit)


def _na(nq, nk, nvt, tbl, ctx_len):
    b, t, _ = nq.shape
    rows = (t - ctx_len) // GRID_W
    assert rows >= NA_KROWS and CHUNK == 2 * GRID_W == LANES
    assert (t - ctx_len) % SCAN_BLOCK == 0 and ctx_len % SCAN_BLOCK == 0
    blk = pl.BlockSpec((1, SCAN_BLOCK, NA_W), lambda bb, s: (bb, s, 0))
    return pl.pallas_call(
        functools.partial(_na_kernel, rows=rows, ctx_len=ctx_len),
        grid=(b, t // SCAN_BLOCK),
        in_specs=[blk, pl.BlockSpec((1, t, NA_W), lambda bb, s: (bb, 0, 0)),
                  pl.BlockSpec((1, NA_W, t), lambda bb, s: (bb, 0, 0)), _resident(tbl.shape)],
        out_specs=blk,
        out_shape=jax.ShapeDtypeStruct((b, t, NA_W), BF16),
        compiler_params=_params(2),
        name="na",
    )(nq, nk, nvt, tbl)


def _zero_token(v):
    bits = lax.bitcast_convert_type(v[0:SUBLANES, 0:LANES], jnp.int32)
    return lax.shift_right_logical(lax.shift_right_logical(bits, 16), 16)


def _mlp_tail(x1, m, nmlp, w1_ref, w2_ref, h2_ref, fillers=()):
    fillers = list(fillers)

    def fill():
        if fillers:
            token = fillers.pop(0)()
            head = pltpu.bitcast(h2_ref[0:2 * SUBLANES, 0:LANES], jnp.int32)
            h2_ref[0:2 * SUBLANES, 0:LANES] = pltpu.bitcast(head + token, BF16)

    h2_ref[...] = _norm_mod(x1, nmlp, m[4:5], m[3:4]).astype(BF16)
    fc = D_MODEL
    n_stage = D_FF // fc
    acc = jnp.zeros_like(x1)
    for j in range(n_stage):
        while len(fillers) > 2 * (n_stage - 1 - j):
            fill()
        u = _mm(h2_ref[...], w1_ref[:, j * fc:(j + 1) * fc])
        fill()
        u = jnp.square(jnp.maximum(u, 0.0)).astype(BF16)
        acc = acc + _mm(u, w2_ref[j * fc:(j + 1) * fc, :])
        fill()
    return x1 + m[5:6] * acc


def _two_phase(step_fn, slots_ref):
    i = pl.program_id(0)

    @pl.when(i == 0)
    def _():
        slots_ref[...] = jnp.zeros_like(slots_ref)

    for parity in range(2):
        @pl.when(i % 2 == parity)
        def _():
            step_fn(slots_ref.at[1 - parity], slots_ref.at[parity])


def _two_phase_maps(n_tiles, n_total):
    def mm(i):
        j = jnp.maximum(i - 1, 0)
        return j // n_tiles, j % n_tiles

    def fin(i):
        k = jnp.minimum(i, n_total - 1)
        return k // n_tiles, k % n_tiles

    return mm, fin


def _even_out_kernel(ctx_ref, x_ref, mod_ref, yna_ref, yf_ref, yb_ref, rg_ref, wo_ref, nmlp_ref, w1_ref, w2_ref,
                     o_ref, slots_ref, h2_ref, *, n_tiles):
    def step(cur_ref, nxt_ref):
        m = mod_ref[0, 0]
        is_ctx = (jnp.maximum(pl.program_id(0) - 1, 0) % n_tiles) == 0
        def finish_head(hh):
            sl = slice(hh * RET_DIM, (hh + 1) * RET_DIM)
            yh = yf_ref[0, :, sl] + yb_ref[0, :, sl]
            dlt = yh - jnp.mean(yh, axis=-1, keepdims=True)
            yn = dlt * lax.rsqrt(jnp.mean(dlt * dlt, axis=-1, keepdims=True) + EPS)
            nxt_ref[:, sl] = (yn * rg_ref[0, :, sl].astype(F32)).astype(BF16)
            return _zero_token(yn)

        mix = _mm(yna_ref[0], wo_ref[RET_W:RET_W + NA_W, :]) + _mm(cur_ref[...], wo_ref[0:RET_W, :])
        x1 = jnp.where(is_ctx, ctx_ref[0], x_ref[0]) + m[2:3] * mix
        o_ref[0] = _mlp_tail(x1, m, nmlp_ref[...], w1_ref, w2_ref, h2_ref,
                             [functools.partial(finish_head, hh) for hh in range(RET_HEADS)])

    _two_phase(step, slots_ref)


def _even_out(ctx, x, modsel, yf, yb, rg, yna, wo, nmlp, w1, w2, layer):
    b = x.shape[0]
    t = ctx.shape[1] + x.shape[1]
    nt = t // TM
    mm, fin = _two_phase_maps(nt, b * nt)

    def mm_tok(w):
        return pl.BlockSpec((1, TM, w), lambda i: (mm(i)[0], mm(i)[1], 0))

    def fin_tok(w):
        return pl.BlockSpec((1, TM, w), lambda i: (fin(i)[0], fin(i)[1], 0))

    return pl.pallas_call(
        functools.partial(_even_out_kernel, n_tiles=nt),
        grid=(b * nt + 1,),
        in_specs=[pl.BlockSpec((1, TM, D_MODEL), lambda i: (mm(i)[0], 0, 0)),
                  pl.BlockSpec((1, TM, D_MODEL), lambda i: (mm(i)[0], jnp.maximum(mm(i)[1] - 1, 0), 0)),
                  pl.BlockSpec((1, 1, 6, D_MODEL), lambda i: (mm(i)[0], jnp.minimum(mm(i)[1], 1), 0, 0)),
                  mm_tok(NA_W), fin_tok(RET_W), fin_tok(RET_W), fin_tok(RET_W),
                  _resident(wo.shape), _resident((1, D_MODEL)), _resident_layer(w1.shape, layer),
                  _resident_layer(w2.shape, layer)],
        out_specs=mm_tok(D_MODEL),
        out_shape=jax.ShapeDtypeStruct((b, t, D_MODEL), F32),
        scratch_shapes=[pltpu.VMEM((2, TM, RET_W), BF16), pltpu.VMEM((TM, D_MODEL), BF16)],
        compiler_params=_params(1),
        name="even_out_mlp",
    )(ctx, x, modsel, yna, yf, yb, rg, wo, nmlp.reshape(1, D_MODEL), w1, w2)


def _conv_perm():
    span = TM // SUBLANES
    rho = np.arange(TM)
    p = np.zeros((TM, TM), np.float32)
    p[rho, (rho % SUBLANES) * span + rho // SUBLANES] = 1.0
    return jnp.asarray(p, BF16)


def _odd_in_kernel(x_ref, mod_ref, g_ref, w_ref, wdt_ref, dtb_ref, perm_ref, z_ref, xbc_ref, dt_ref):
    m = mod_ref[0, 0]
    h = _norm_mod(x_ref[0], g_ref[...], m[1:2], m[0:1]).astype(BF16)
    cw = 512
    for j in range(SSM_INNER // cw):
        z_ref[0, :, j * cw:(j + 1) * cw] = _silu(_mm(h, w_ref[:, j * cw:(j + 1) * cw])).astype(BF16)
    hp = _mm(perm_ref[...], h).astype(BF16)
    for j in range(SSM_CONV_CH // cw):
        c0 = SSM_INNER + j * cw
        xbc_ref[0, :, j * cw:(j + 1) * cw] = _mm(hp, w_ref[:, c0:c0 + cw]).astype(BF16)
    raw = _mm(h, wdt_ref[...]) + dtb_ref[...]
    dt_ref[0] = jnp.maximum(raw, 0.0) + jnp.log1p(jnp.exp(-jnp.abs(raw)))


def _odd_in(xs, modsel, g, w_main, w_dt, dt_bias):
    b, t, _ = xs.shape
    tok = lambda w: pl.BlockSpec((1, TM, w), lambda bb, tt: (bb, tt, 0))
    return pl.pallas_call(
        _odd_in_kernel,
        grid=(b, t // TM),
        in_specs=[tok(D_MODEL), _mod_spec(), _resident((1, D_MODEL)), _resident(w_main.shape),
                  _resident(w_dt.shape), _resident((1, LANES)), _resident((TM, TM))],
        out_specs=[tok(SSM_INNER), tok(SSM_CONV_CH), tok(LANES)],
        out_shape=[jax.ShapeDtypeStruct((b, t, SSM_INNER), BF16),
                   jax.ShapeDtypeStruct((b, t, SSM_CONV_CH), BF16),
                   jax.ShapeDtypeStruct((b, t, LANES), F32)],
        compiler_params=_params(2),
        name="odd_in",
    )(xs, modsel, g.reshape(1, D_MODEL), w_main, w_dt, dt_bias, _conv_perm())


def _conv_kernel(main_ref, prev_ref, next_ref, w_ref, b_ref, unperm_ref, o_ref, act_ref, *, n_tiles):
    t = pl.program_id(1)
    has_prev = (t >= 2).astype(F32)
    has_next = jnp.logical_and(t >= 1, t < n_tiles - 1).astype(F32)
    span = TM // SUBLANES
    half = SSM_CONV // 2
    sub = lax.broadcasted_iota(jnp.int32, (SUBLANES, LANES), 0)

    def lane_tile(j, carry):
        lanes = pl.ds(pl.multiple_of(j * LANES, LANES), LANES)
        xm = main_ref[0, :, lanes].astype(F32)
        pv = prev_ref[0, :, lanes].astype(F32) * has_prev
        nx = next_ref[0, :, lanes].astype(F32) * has_next
        reg = {m: xm[m * SUBLANES:(m + 1) * SUBLANES] for m in range(span)}
        for i in range(1, half + 1):
            row = HALO - 1 - (i - 1) * SUBLANES
            edge = jnp.broadcast_to(pv[row:row + 1], (SUBLANES, LANES))
            reg[-i] = jnp.where(sub == 0, edge, pltpu.roll(reg[span - i], 1, 0))
            row = (i - 1) * SUBLANES
            edge = jnp.broadcast_to(nx[row:row + 1], (SUBLANES, LANES))
            reg[span + i - 1] = jnp.where(sub == SUBLANES - 1, edge, pltpu.roll(reg[i - 1], SUBLANES - 1, 0))
        wk = [jnp.broadcast_to(w_ref[k:k + 1, lanes], (SUBLANES, LANES)) for k in range(SSM_CONV)]
        bias = jnp.broadcast_to(b_ref[:, lanes], (SUBLANES, LANES))
        acts = []
        for m in range(span):
            acc = bias
            for k in range(SSM_CONV):
                acc = acc + wk[k] * reg[m + k - half]
            acts.append(acc * jnp.tanh(acc) + acc)
        act_ref[:, lanes] = jnp.concatenate(acts, axis=0).astype(BF16)
        return carry

    lax.fori_loop(0, SSM_CONV_CH // LANES, lane_tile, 0, unroll=2)
    o_ref[0] = _mm(unperm_ref[...], act_ref[...]).astype(BF16)


def _conv(xbc, w, bias, ctx_len):
    b, t, ch = xbc.shape
    assert ctx_len == TM and TM % HALO == 0 and HALO >= (SSM_CONV // 2) * SUBLANES
    nt = t // TM
    per = TM // HALO
    main = pl.BlockSpec((1, TM, ch), lambda bb, tt: (bb, tt, 0))
    prev = pl.BlockSpec((1, HALO, ch), lambda bb, tt: (bb, jnp.maximum(tt * per - 1, 0), 0))
    nxt = pl.BlockSpec((1, HALO, ch), lambda bb, tt: (bb, jnp.minimum((tt + 1) * per, nt * per - 1), 0))
    return pl.pallas_call(
        functools.partial(_conv_kernel, n_tiles=nt),
        grid=(b, nt),
        in_specs=[main, prev, nxt, _resident(w.shape), _resident((1, ch)), _resident((TM, TM))],
        out_specs=main,
        out_shape=jax.ShapeDtypeStruct((b, t, ch), BF16),
        scratch_shapes=[pltpu.VMEM((TM, ch), BF16)],
        compiler_params=_params(2),
        name="dwconv_silu",
    )(xbc, xbc, xbc, w, bias.reshape(1, ch), _conv_perm().T)


def _dot_exact_lhs(a16, x):
    hi = x.astype(BF16)
    r1 = x - hi.astype(F32)
    mid = r1.astype(BF16)
    lo = (r1 - mid.astype(F32)).astype(BF16)
    return _mm(a16, hi) + _mm(a16, mid) + _mm(a16, lo)


def _ssd_kernel(xf_ref, xb_ref, dtf_ref, dtb_ref, a_ref, tri_ref, yf_ref, yb_ref, h_ref, row_ref, stage_ref, *, n_ctx):
    @pl.when(pl.program_id(1) == 0)
    def _():
        h_ref[...] = jnp.zeros_like(h_ref)

    left = lax.broadcasted_iota(jnp.int32, (1, LANES), 1) < SSM_HEAD_DIM
    keep_l = jnp.where(left, jnp.uint32(0xFFFFFFFF), jnp.uint32(0))
    keep_r = jnp.where(left, jnp.uint32(0), jnp.uint32(0xFFFFFFFF))
    ii = lax.broadcasted_iota(jnp.int32, (CHUNK, CHUNK), 0)
    jj = lax.broadcasted_iota(jnp.int32, (CHUNK, CHUNK), 1)
    b_off = SSM_INNER
    c_off = SSM_INNER + SSM_GN

    directions = ((xf_ref, dtf_ref, yf_ref), (xb_ref, dtb_ref, yb_ref))
    n_sub = xf_ref.shape[1] // CHUNK
    rows_of = [[slice(c * CHUNK, (c + 1) * CHUNK) for c in (sub, n_sub - 1 - sub)] for sub in range(n_sub)]

    def scan_block(with_y):
        decays = {}
        for sub in range(n_sub):
            for d, (_, dt_ref, _) in enumerate(directions):
                dtc = dt_ref[0, rows_of[sub][d], :]
                acum = _dot_exact_lhs(tri_ref[d], dtc * a_ref[...]) * LOG2E
                acum_t = acum.T
                dt_t = dtc.T
                last = acum[CHUNK - 1:CHUNK, :] if d == 0 else acum[0:1, :]
                last_t = acum_t[:, CHUNK - 1:CHUNK] if d == 0 else acum_t[:, 0:1]
                if with_y:
                    row_ref[sub, d, 0] = acum_t - jnp.log2(dt_t)
                row_ref[sub, d, 1] = dt_t * jnp.exp2(last_t - acum_t)
                row_ref[sub, d, 2] = acum
                decays[sub, d] = jnp.exp2(last)

        for sub in range(n_sub):
            for d, (x_ref, _, y_ref) in enumerate(directions):
                rows = rows_of[sub][d]
                elast = decays[sub, d]
                mask = (ii >= jj) if d == 0 else (jj >= ii)
                if not with_y:
                    y_ref[0, rows, :] = jnp.zeros((CHUNK, SSM_INNER), F32)

                for g in range(SSM_GROUPS):
                    bg = x_ref[0, rows, b_off + g * SSM_STATE:b_off + (g + 1) * SSM_STATE]
                    stage_ref[d, g, 0] = bg.astype(F32).T
                    h_t = h_ref[d, g]
                    if with_y:
                        cg = x_ref[0, rows, c_off + g * SSM_STATE:c_off + (g + 1) * SSM_STATE]
                        stage_ref[d, g, 1] = _nt(cg, bg)
                        y_int = _mm(cg, h_t.astype(BF16))
                    for hp in range(SSM_HPG // 2):
                        c0 = d * SSM_HEADS + g * SSM_HPG + 2 * hp
                        ch = slice(g * SSM_GW + hp * LANES, g * SSM_GW + (hp + 1) * LANES)
                        hl = slice(hp * LANES, (hp + 1) * LANES)
                        xu = pltpu.bitcast(x_ref[0, rows, ch], jnp.uint32)
                        xbd = jnp.concatenate([pltpu.bitcast(xu & keep_l, BF16), pltpu.bitcast(xu & keep_r, BF16)],
                                              axis=0)
                        intra, upd, eac = [], [], []
                        for e in range(2):
                            c = c0 + e
                            upd.append((stage_ref[d, g, 0] * row_ref[sub, d, 1, c:c + 1, :]).astype(BF16))
                            if with_y:
                                acol = jnp.broadcast_to(row_ref[sub, d, 2, :, c:c + 1], (CHUNK, CHUNK))
                                lmat = jnp.exp2(jnp.where(mask, acol - row_ref[sub, d, 0, c:c + 1, :], NEG))
                                intra.append((lmat * stage_ref[d, g, 1]).astype(BF16))
                                eac.append(jnp.exp2(acol))
                        el = jnp.where(left, jnp.broadcast_to(elast[:, c0:c0 + 1], (1, LANES)),
                                       jnp.broadcast_to(elast[:, c0 + 1:c0 + 2], (1, LANES)))
                        if with_y:
                            lhs = jnp.concatenate([jnp.concatenate(intra, axis=1), jnp.concatenate(upd, axis=1)],
                                                  axis=0)
                            r = _mm(lhs, xbd)
                            y_ref[0, rows, ch] = r[:CHUNK] + y_int[:, hl] * jnp.where(left, eac[0], eac[1])
                            h_ref[d, g, :, hl] = el * h_t[:, hl] + r[CHUNK:]
                        else:
                            h_ref[d, g, :, hl] = el * h_t[:, hl] + _mm(jnp.concatenate(upd, axis=1), xbd)

    is_ctx = pl.program_id(1) < n_ctx

    @pl.when(is_ctx)
    def _():
        scan_block(False)

    @pl.when(jnp.logical_not(is_ctx))
    def _():
        scan_block(True)


def _ssd(xc, dt, a_row, ctx_len):
    b, t, ch = xc.shape
    ns = t // SCAN_BLOCK
    fwd, bwd = _scan_orders(ns, ctx_len // SCAN_BLOCK)
    idx = np.arange(CHUNK)
    tri = jnp.asarray(np.stack([idx[:, None] >= idx[None, :], idx[:, None] <= idx[None, :]]), BF16)

    def spec(order, w):
        return pl.BlockSpec((1, SCAN_BLOCK, w), lambda bb, i: (bb, order(i), 0))

    return pl.pallas_call(
        functools.partial(_ssd_kernel, n_ctx=ctx_len // SCAN_BLOCK),
        grid=(b, ns),
        in_specs=[spec(fwd, ch), spec(bwd, ch), spec(fwd, LANES), spec(bwd, LANES),
                  _resident((1, LANES)), _resident((2, CHUNK, CHUNK))],
        out_specs=[spec(fwd, SSM_INNER), spec(bwd, SSM_INNER)],
        out_shape=[jax.ShapeDtypeStruct((b, t, SSM_INNER), F32)] * 2,
        scratch_shapes=[pltpu.VMEM((2, SSM_GROUPS, SSM_STATE, SSM_GW), F32),
                        pltpu.VMEM((SCAN_BLOCK // CHUNK, 2, 3, CHUNK, CHUNK), F32),
                        pltpu.VMEM((2, SSM_GROUPS, 2, CHUNK, CHUNK), F32)],
        compiler_params=_params(2),
        name="ssd",
    )(xc, xc, dt, dt, a_row, tri)


def _odd_out_kernel(x_ref, mod_ref, yf_ref, yb_ref, xc_ref, z_ref, dsk_ref, nssm_ref, wo_ref,
                    nmlp_ref, w1_ref, w2_ref, nfin_ref, o_ref, slots_ref, h2_ref):
    def step(cur_ref, nxt_ref):
        m = mod_ref[0, 0]
        def finish_rows(r):
            rows = slice(r * FIN_ROWS, (r + 1) * FIN_ROWS)
            y = yf_ref[0, rows, :] + yb_ref[0, rows, :] + dsk_ref[...] * xc_ref[0, rows, :].astype(F32)
            y = y * z_ref[0, rows, :].astype(F32)
            y = y * lax.rsqrt(jnp.mean(y * y, axis=-1, keepdims=True) + EPS) * nssm_ref[...]
            nxt_ref[rows, :] = y.astype(BF16)
            return _zero_token(y)

        x1 = x_ref[0] + m[2:3] * _mm(cur_ref[...], wo_ref[...])
        x2 = _mlp_tail(x1, m, nmlp_ref[...], w1_ref, w2_ref, h2_ref,
                       [functools.partial(finish_rows, r) for r in range(TM // FIN_ROWS)])
        o_ref[0] = x2 * lax.rsqrt(jnp.mean(x2 * x2, axis=-1, keepdims=True) + EPS) * nfin_ref[...]

    _two_phase(step, slots_ref)


def _odd_out(xs, modsel, yf, yb, xc, z, dsk, nssm, wo, nmlp, w1, w2, layer, nfin, ctx_len):
    b, t, _ = xs.shape
    skip = ctx_len // TM
    nt = t // TM - skip
    mm, fin = _two_phase_maps(nt, b * nt)
    fin_tok = lambda w: pl.BlockSpec((1, TM, w), lambda i: (fin(i)[0], fin(i)[1] + skip, 0))
    vec = lambda w: _resident((1, w))
    return pl.pallas_call(
        _odd_out_kernel,
        grid=(b * nt + 1,),
        in_specs=[pl.BlockSpec((1, TM, D_MODEL), lambda i: (mm(i)[0], mm(i)[1] + skip, 0)),
                  pl.BlockSpec((1, 1, 6, D_MODEL), lambda i: (mm(i)[0], 1, 0, 0)),
                  fin_tok(SSM_INNER), fin_tok(SSM_INNER), fin_tok(SSM_INNER), fin_tok(SSM_INNER),
                  vec(SSM_INNER), vec(SSM_INNER), _resident(wo.shape), vec(D_MODEL),
                  _resident_layer(w1.shape, layer), _resident_layer(w2.shape, layer), vec(D_MODEL)],
        out_specs=pl.BlockSpec((1, TM, D_MODEL), lambda i: (mm(i)[0], mm(i)[1], 0)),
        out_shape=jax.ShapeDtypeStruct((b, t - ctx_len, D_MODEL), F32),
        scratch_shapes=[pltpu.VMEM((2, TM, SSM_INNER), BF16), pltpu.VMEM((TM, D_MODEL), BF16)],
        compiler_params=_params(1),
        name="odd_out_mlp",
    )(xs, modsel, yf, yb, xc, z, dsk.reshape(1, -1), nssm.reshape(1, -1), wo, nmlp.reshape(1, -1),
      w1, w2, nfin.reshape(1, -1))


def kernel(x, c, ctx, c_ctx, w_mod, b_mod, norm_mix, norm_mlp, w_mlp_in, w_mlp_out, w_in_even, w_out_even,
           ret_decay_logit, na_rpb, w_in_odd, conv_w, conv_b, dt_bias, a_log, d_skip, ssm_norm, w_out_odd,
           norm_final):
    assert w_mod.shape[0] == DEPTH == 2 and x.shape[2] == D_MODEL
    ctx_len = ctx.shape[1]
    seq = x.shape[1]
    assert ctx_len == TM and seq % (2 * GRID_W) == 0

    modsel = _modulation(c, c_ctx, w_mod, b_mod)
    w1 = w_mlp_in.astype(BF16)
    w2 = w_mlp_out.astype(BF16)

    n_tok = 4 * RET_W + 2 * NA_W
    w_even = w_in_even[0].astype(BF16)
    rq, rk, rv, rg, nq, nk, nvt = _even_in(ctx, x, modsel[0], norm_mix[0], w_even, w_even[:, n_tok:].T,
                                           _rope_tables(seq, ctx_len))
    yf, yb = _retention(rq, rk, rv, _ret_tables(ret_decay_logit[0]), ctx_len)
    yna = _na(nq, nk, nvt, _na_bias_tables(na_rpb[0]), ctx_len)
    xs = _even_out(ctx, x, modsel[0], yf, yb, rg, yna, w_out_even[0].astype(BF16), norm_mlp[0], w1, w2, 0)

    n_main = SSM_INNER + SSM_CONV_CH
    wi = w_in_odd[0]
    w_dt = jnp.pad(wi[:, n_main:], ((0, 0), (0, LANES - 2 * SSM_HEADS))).astype(BF16)
    pad_row = lambda v: jnp.pad(v.astype(F32).reshape(1, -1), ((0, 0), (0, LANES - 2 * SSM_HEADS)))
    z, xbc, dt = _odd_in(xs, modsel[1], norm_mix[1], wi[:, :n_main].astype(BF16), w_dt, pad_row(dt_bias[0]))
    xc = _conv(xbc, 0.5 * conv_w[0].astype(F32), 0.5 * conv_b[0].astype(F32), ctx_len)
    yf, yb = _ssd(xc, dt, pad_row(-jnp.exp(a_log[0].astype(F32))), ctx_len)
    dsk = jnp.repeat(d_skip[0].astype(F32), SSM_HEAD_DIM)
    return _odd_out(xs, modsel[1], yf, yb, xc, z, dsk, ssm_norm[0], w_out_odd[0].astype(BF16), norm_mlp[1],
                    w1, w2, 1, norm_final, ctx_len)
```
